```python
import jax
import jax.numpy as jnp
from jax import lax
import numpy as np

D_MODEL = 1024
BATCH = 4
SEQ = 8192
DEPTH = 2

N_META = 16
LEAD = 128
PAD = LEAD - N_META
Q_BLOCK = 128
CHUNK = 64
EPS = 1e-6
NEG_INF = -1e30

MLA_HEADS = 8
MLA_Q_LORA = 256
MLA_KV_LORA = 128
MLA_NOPE = 64
MLA_ROPE = 32
MLA_V = 64
MLA_QK = MLA_NOPE + MLA_ROPE
MLA_VW = MLA_HEADS * MLA_V
ROPE_THETA = 10000.0

GLA_HEADS = 4
GLA_DK = 64
GLA_DV = 128
GLA_KW = GLA_HEADS * GLA_DK
GLA_VW = GLA_HEADS * GLA_DV
GLA_GATE_RANK = 16
GLA_GATE_NORM = 16.0

HGRN_HEADS = 4
HGRN_DK = 128
HGRN_DV = 128
HGRN_KW = HGRN_HEADS * HGRN_DK
HGRN_VW = HGRN_HEADS * HGRN_DV

N_BRANCH = 3
IN_SPLITS = (MLA_Q_LORA, MLA_KV_LORA, MLA_ROPE, GLA_KW, GLA_KW, GLA_VW, GLA_GATE_RANK, GLA_VW, HGRN_KW, HGRN_KW, HGRN_VW, HGRN_VW, N_BRANCH * D_MODEL)
IN_COLS = sum(IN_SPLITS)

D_FF = 2816
N_EXPERTS = 8
TOP_K = 2
D_EXPERT = 3584
MOE_BLOCK = 128
N_DENSE = (DEPTH + 1) // 2
N_MOE = DEPTH // 2

kernel_name = 'hybrid_mla_gla_hgrn2_moe_block'


def rmsnorm(x, g):
    xf = x.astype(jnp.float32)
    y = xf * lax.rsqrt(jnp.mean(xf * xf, axis=-1, keepdims=True) + EPS)
    return (y * g.astype(jnp.float32)).astype(x.dtype)


def rope(x, pos):
    half = x.shape[-1] // 2
    inv = ROPE_THETA ** (-jnp.arange(half, dtype=jnp.float32) / half)
    ang = pos.astype(jnp.float32)[:, None] * inv[None, :]
    cos = jnp.cos(ang)[None, :, None, :]
    sin = jnp.sin(ang)[None, :, None, :]
    xf = x.astype(jnp.float32)
    x1, x2 = xf[..., :half], xf[..., half:]
    return jnp.concatenate([x1 * cos - x2 * sin, x1 * sin + x2 * cos], axis=-1).astype(x.dtype)


def causal_block_attention(q, k, v):
    B, L, H, dq = q.shape
    padw = ((0, 0), (PAD, 0), (0, 0), (0, 0))
    qp, kp, vp = (jnp.pad(t, padw) for t in (q, k, v))
    Lp = L + PAD
    nb = Lp // Q_BLOCK
    key_pos = jnp.arange(Lp)
    key_ok = key_pos >= PAD
    q_blocks = qp.reshape(B, nb, Q_BLOCK, H, dq).transpose(1, 0, 2, 3, 4)
    scale = dq ** -0.5

    def attend(args):
        qb, i = args
        s = jnp.einsum('bqhd,bkhd->bhqk', qb, kp).astype(jnp.float32) * scale
        q_pos = i * Q_BLOCK + jnp.arange(Q_BLOCK)
        mask = (key_pos[None, :] <= q_pos[:, None]) & key_ok[None, :]
        p = jax.nn.softmax(jnp.where(mask, s, NEG_INF), axis=-1).astype(vp.dtype)
        return jnp.einsum('bhqk,bkhd->bqhd', p, vp)

    o = lax.map(attend, (q_blocks, jnp.arange(nb)))
    o = o.transpose(1, 0, 2, 3, 4).reshape(B, Lp, H, v.shape[-1])
    return o[:, PAD:]


def chunk_gated_linear_attention(q, k, v, log_g):
    dtype = v.dtype
    B, L, H, dk = q.shape
    dv = v.shape[-1]
    Lp = L + PAD
    N = Lp // CHUNK
    padw = ((0, 0), (PAD, 0), (0, 0), (0, 0))

    def to_chunks(t):
        return jnp.pad(t.astype(jnp.float32), padw).reshape(B, N, CHUNK, H, t.shape[-1])

    q, k, v, log_g = (to_chunks(t) for t in (q, k, v, log_g))
    b = jnp.cumsum(log_g, axis=2)
    b_ref = b[:, :, CHUNK // 2 - 1:CHUNK // 2]
    b_last = b[:, :, CHUNK - 1:CHUNK]
    causal = jnp.tril(jnp.ones((CHUNK, CHUNK), dtype=bool))
    a = jnp.einsum('bnihd,bnjhd->bnhij', q * jnp.exp(b - b_ref), k * jnp.exp(b_ref - b))
    a = jnp.where(causal, a, 0.0)
    o_intra = jnp.einsum('bnhij,bnjhv->bnihv', a, v)
    d_state = jnp.einsum('bnjhd,bnjhv->bnhdv', k * jnp.exp(b_last - b), v)
    decay = jnp.exp(b_last[:, :, 0])

    def step(S, inp):
        ds_n, dec_n = inp
        return dec_n[..., None] * S + ds_n, S

    _, s_prev = lax.scan(step, jnp.zeros((B, H, dk, dv), jnp.float32),
                         (jnp.moveaxis(d_state, 1, 0), jnp.moveaxis(decay, 1, 0)))
    s_prev = jnp.moveaxis(s_prev, 0, 1)
    o_inter = jnp.einsum('bnihd,bnhdv->bnihv', q * jnp.exp(b), s_prev)
    o = (o_intra + o_inter).reshape(B, Lp, H, dv)[:, PAD:]
    return o.astype(dtype)


def mla_branch(x_dq, x_dkv, x_kr, cq_norm, w_uq, ckv_norm, w_ukv, q_norm, k_norm):
    B, L, _ = x_dq.shape
    pos = jnp.arange(L, dtype=jnp.int32)
    q = (rmsnorm(x_dq, cq_norm) @ w_uq).reshape(B, L, MLA_HEADS, MLA_QK)
    kv = (rmsnorm(x_dkv, ckv_norm) @ w_ukv).reshape(B, L, MLA_HEADS, MLA_NOPE + MLA_V)
    k_rope = jnp.broadcast_to(x_kr[:, :, None, :], (B, L, MLA_HEADS, MLA_ROPE))
    k = jnp.concatenate([kv[..., :MLA_NOPE], k_rope], axis=-1)
    v = kv[..., MLA_NOPE:]
    q = rmsnorm(q, q_norm)
    k = rmsnorm(k, k_norm)
    q = jnp.concatenate([q[..., :MLA_NOPE], rope(q[..., MLA_NOPE:], pos)], axis=-1)
    k = jnp.concatenate([k[..., :MLA_NOPE], rope(k[..., MLA_NOPE:], pos)], axis=-1)
    return causal_block_attention(q, k, v).reshape(B, L, MLA_VW)


def gla_branch(x_q, x_k, x_v, x_gd, x_r, w_g2, b_g, o_norm):
    B, L, _ = x_q.shape
    log_g = jax.nn.log_sigmoid((x_gd @ w_g2 + b_g).astype(jnp.float32)) / GLA_GATE_NORM
    q = x_q.reshape(B, L, GLA_HEADS, GLA_DK) * (GLA_DK ** -0.5)
    k = x_k.reshape(B, L, GLA_HEADS, GLA_DK)
    v = x_v.reshape(B, L, GLA_HEADS, GLA_DV)
    o = chunk_gated_linear_attention(q, k, v, log_g.reshape(B, L, GLA_HEADS, GLA_DK))
    return rmsnorm(o, o_norm).reshape(B, L, GLA_VW) * jax.nn.silu(x_r)


def hgrn2_branch(x_q, x_f, x_i, x_g, lower_bound, o_norm):
    B, L, _ = x_q.shape
    f = lower_bound + (1.0 - lower_bound) * jax.nn.sigmoid(x_f.astype(jnp.float32))
    log_f = jnp.log(f)
    k = 1.0 - f
    q = jax.nn.silu(x_q)
    o = chunk_gated_linear_attention(q.reshape(B, L, HGRN_HEADS, HGRN_DK), k.reshape(B, L, HGRN_HEADS, HGRN_DK),
                                     x_i.reshape(B, L, HGRN_HEADS, HGRN_DV), log_f.reshape(B, L, HGRN_HEADS, HGRN_DK))
    return rmsnorm(o, o_norm).reshape(B, L, HGRN_VW) * jax.nn.silu(x_g)


def swiglu(u, w_gate, w_up, w_down):
    return (jax.nn.silu(u @ w_gate) * (u @ w_up)) @ w_down


def moe_swiglu(u, w_router, b_router, w_gate, w_up, w_down):
    B, L, D = u.shape
    T = B * L
    A = T * TOP_K
    xt = u.reshape(T, D)
    logits = (xt @ w_router + b_router).astype(jnp.float32)
    top_val, top_idx = lax.top_k(logits, TOP_K)
    gate = jax.nn.softmax(top_val, axis=-1)
    e_flat = top_idx.reshape(A)
    tok_flat = jnp.repeat(jnp.arange(T, dtype=jnp.int32), TOP_K)
    order = jnp.argsort(e_flat)
    e_sorted = e_flat[order]
    counts = jnp.bincount(e_flat, length=N_EXPERTS)
    start = jnp.cumsum(counts) - counts
    padded = (counts + MOE_BLOCK - 1) // MOE_BLOCK * MOE_BLOCK
    pend = jnp.cumsum(padded)
    pstart = pend - padded
    dest = pstart[e_sorted] + jnp.arange(A) - start[e_sorted]
    n_blocks = -(-A // MOE_BLOCK) + N_EXPERTS
    P = n_blocks * MOE_BLOCK
    slot_tok = jnp.zeros((P,), jnp.int32).at[dest].set(tok_flat[order])
    slot_w = jnp.zeros((P,), jnp.float32).at[dest].set(gate.reshape(A)[order])
    block_expert = jnp.minimum(jnp.searchsorted(pend, jnp.arange(n_blocks) * MOE_BLOCK, side='right'), N_EXPERTS - 1)

    def expert_block(args):
        tok, e = args
        xb = xt[tok]
        hb = jax.nn.silu(xb @ w_gate[e]) * (xb @ w_up[e])
        return hb @ w_down[e]

    y_slots = lax.map(expert_block, (slot_tok.reshape(n_blocks, MOE_BLOCK), block_expert))
    y_slots = y_slots.reshape(P, D) * slot_w[:, None].astype(u.dtype)
    return jnp.zeros((T, D), u.dtype).at[slot_tok].add(y_slots).reshape(B, L, D)


def setup_inputs(seed: int = 0) -> dict:
    key = jax.random.key(seed)
    keys = iter(jax.random.split(key, 40))
    f32 = jnp.float32

    def w(shape, fan_in):
        return jax.random.normal(next(keys), shape, f32) * (fan_in ** -0.5)

    def gain(shape):
        return 1.0 + 0.02 * jax.random.normal(next(keys), shape, f32)

    def small(shape, s):
        return s * jax.random.normal(next(keys), shape, f32)

    return {
        'x': jax.random.normal(next(keys), (BATCH, SEQ, D_MODEL), f32),
        'meta_tokens': small((N_META, D_MODEL), 1.0),
        'attn_norm': gain((DEPTH, D_MODEL)),
        'w_in': w((DEPTH, D_MODEL, IN_COLS), D_MODEL),
        'mla_cq_norm': gain((DEPTH, MLA_Q_LORA)),
        'w_mla_uq': w((DEPTH, MLA_Q_LORA, MLA_HEADS * MLA_QK), MLA_Q_LORA),
        'mla_ckv_norm': gain((DEPTH, MLA_KV_LORA)),
        'w_mla_ukv': w((DEPTH, MLA_KV_LORA, MLA_HEADS * (MLA_NOPE + MLA_V)), MLA_KV_LORA),
        'mla_q_norm': gain((DEPTH, MLA_QK)),
        'mla_k_norm': gain((DEPTH, MLA_QK)),
        'w_gla_g2': w((DEPTH, GLA_GATE_RANK, GLA_KW), GLA_GATE_RANK),
        'b_gla_g': small((DEPTH, GLA_KW), 0.1),
        'gla_o_norm': gain((DEPTH, GLA_DV)),
        'hgrn_lower_bounds': small((DEPTH, HGRN_KW), 0.1),
        'hgrn_o_norm': gain((DEPTH, HGRN_DV)),
        'w_mla_o': w((DEPTH, MLA_VW, D_MODEL), MLA_VW),
        'w_gla_o': w((DEPTH, GLA_VW, D_MODEL), GLA_VW),
        'w_hgrn_o': w((DEPTH, HGRN_VW, D_MODEL), HGRN_VW),
        'w_out': w((DEPTH, D_MODEL, D_MODEL), D_MODEL),
        'ffn_norm': gain((DEPTH, D_MODEL)),
        'w_ff_gate': w((N_DENSE, D_MODEL, D_FF), D_MODEL),
        'w_ff_up': w((N_DENSE, D_MODEL, D_FF), D_MODEL),
        'w_ff_down': w((N_DENSE, D_FF, D_MODEL), D_FF),
        'w_router': w((N_MOE, D_MODEL, N_EXPERTS), D_MODEL),
        'b_router': small((N_MOE, N_EXPERTS), 0.01),
        'w_exp_gate': w((N_MOE, N_EXPERTS, D_MODEL, D_EXPERT), D_MODEL),
        'w_exp_up': w((N_MOE, N_EXPERTS, D_MODEL, D_EXPERT), D_MODEL),
        'w_exp_down': w((N_MOE, N_EXPERTS, D_EXPERT, D_MODEL), D_EXPERT),
    }


def reference(x, meta_tokens, attn_norm, w_in, mla_cq_norm, w_mla_uq, mla_ckv_norm, w_mla_ukv, mla_q_norm, mla_k_norm,
              w_gla_g2, b_gla_g, gla_o_norm, hgrn_lower_bounds, hgrn_o_norm, w_mla_o, w_gla_o, w_hgrn_o, w_out,
              ffn_norm, w_ff_gate, w_ff_up, w_ff_down, w_router, b_router, w_exp_gate, w_exp_up, w_exp_down):
    B = x.shape[0]
    meta = jnp.broadcast_to(meta_tokens[None].astype(x.dtype), (B, N_META, D_MODEL))
    h = jnp.concatenate([meta, x], axis=1)
    L = h.shape[1]
    lbs = jax.nn.softmax(hgrn_lower_bounds.astype(jnp.float32), axis=0)
    lbs = jnp.cumsum(lbs, axis=0) - lbs[0]
    offsets = np.cumsum(IN_SPLITS)[:-1]
    for layer in range(DEPTH):
        u = rmsnorm(h, attn_norm[layer])
        (mla_dq, mla_dkv, mla_kr, gla_q, gla_k, gla_v, gla_gd, gla_r,
         hgrn_q, hgrn_f, hgrn_i, hgrn_g, gate_logits) = jnp.split(u @ w_in[layer], offsets, axis=-1)
        o_mla = mla_branch(mla_dq, mla_dkv, mla_kr, mla_cq_norm[layer], w_mla_uq[layer], mla_ckv_norm[layer],
                           w_mla_ukv[layer], mla_q_norm[layer], mla_k_norm[layer])
        o_gla = gla_branch(gla_q, gla_k, gla_v, gla_gd, gla_r, w_gla_g2[layer], b_gla_g[layer], gla_o_norm[layer])
        o_hgrn = hgrn2_branch(hgrn_q, hgrn_f, hgrn_i, hgrn_g, lbs[layer], hgrn_o_norm[layer])
        gates = jax.nn.sigmoid(gate_logits).reshape(B, L, N_BRANCH, D_MODEL)
        mixed = (gates[:, :, 0] * (o_mla @ w_mla_o[layer])
                 + gates[:, :, 1] * (o_gla @ w_gla_o[layer])
                 + gates[:, :, 2] * (o_hgrn @ w_hgrn_o[layer]))
        h = h + mixed @ w_out[layer]
        u = rmsnorm(h, ffn_norm[layer])
        if layer % 2 == 0:
            i = layer // 2
            h = h + swiglu(u, w_ff_gate[i], w_ff_up[i], w_ff_down[i])
        else:
            i = layer // 2
            h = h + moe_swiglu(u, w_router[i], b_router[i], w_exp_gate[i], w_exp_up[i], w_exp_down[i])
    return h[:, N_META:]
```

```python
import functools

import numpy as np
import jax
import jax.numpy as jnp
from jax import lax
from jax.experimental import pallas as pl
from jax.experimental.pallas import tpu as pltpu

f32 = jnp.float32
bf16 = jnp.bfloat16

D_MODEL = 1024
N_META = 16
PAD = 112
CHUNK = 64
EPS = 1e-6
NEG_INF = -1e30
MLA_HEADS = 8
MLA_Q_LORA = 256
MLA_KV_LORA = 128
MLA_NOPE = 64
MLA_ROPE = 32
MLA_V = 64
MLA_QK = MLA_NOPE + MLA_ROPE
ROPE_THETA = 10000.0
GLA_HEADS = 4
GLA_DK = 64
GLA_DV = 128
GLA_KW = GLA_HEADS * GLA_DK
GLA_VW = GLA_HEADS * GLA_DV
GLA_GATE_RANK = 16
GLA_GATE_NORM = 16.0
HGRN_HEADS = 4
HGRN_DK = 128
HGRN_KW = 512
HGRN_VW = 512
N_BRANCH = 3
D_FF = 2816
N_EXPERTS = 8
TOP_K = 2
D_EXPERT = 3584

LANES = 128
HEAD_LANES = 128
VMEM_LIMIT = 56 * 1024 * 1024

ROW_TILE = 640
KV_TILE = 640
FF_CHUNK = 1408
EXP_CHUNK = 896

COL_MLA = 0
COL_GLA_Q = 512
COL_GLA_K = 768
COL_GLA_V = 1024
COL_GLA_R = 1536
COL_HG_Q = 2048
COL_HG_F = 2560
COL_HG_I = 3072
COL_HG_G = 3584
COL_GATES = 4096
COL_GLA_GD = 7168
IN_COLS_P = 7296
IN_COL_TILE = 2432


def _cparams(sem):
    return pltpu.CompilerParams(dimension_semantics=sem, vmem_limit_bytes=VMEM_LIMIT)


def _resident(shape):
    nd = len(shape)
    return pl.BlockSpec(shape, lambda *_: (0,) * nd, pipeline_mode=pl.Buffered(1))


def _silu(x):
    return x * (1.0 / (1.0 + jnp.exp(-x)))


def _sigmoid(x):
    return 1.0 / (1.0 + jnp.exp(-x))


def _rms(x, gain):
    ms = jnp.mean(x * x, axis=-1, keepdims=True)
    return x * lax.rsqrt(ms + EPS) * gain


def _in_proj_kernel(h_ref, g_ref, w_ref, o_ref, u_sc):
    @pl.when(pl.program_id(1) == 0)
    def _():
        u_sc[...] = _rms(h_ref[...], g_ref[...]).astype(bf16)

    o_ref[...] = jnp.dot(u_sc[...], w_ref[...], preferred_element_type=f32).astype(bf16)


def _in_proj(h, gain, w):
    tp = h.shape[0]
    return pl.pallas_call(
        _in_proj_kernel,
        out_shape=jax.ShapeDtypeStruct((tp, IN_COLS_P), bf16),
        grid=(tp // ROW_TILE, IN_COLS_P // IN_COL_TILE),
        in_specs=[
            pl.BlockSpec((ROW_TILE, D_MODEL), lambda i, j: (i, 0)),
            pl.BlockSpec((1, D_MODEL), lambda i, j: (0, 0)),
            pl.BlockSpec((D_MODEL, IN_COL_TILE), lambda i, j: (0, j)),
        ],
        out_specs=pl.BlockSpec((ROW_TILE, IN_COL_TILE), lambda i, j: (i, j)),
        scratch_shapes=[pltpu.VMEM((ROW_TILE, D_MODEL), bf16)],
        compiler_params=_cparams(("parallel", "arbitrary")),
        name="in_proj",
    )(h, gain, w)


def _rope(x, c, s_up, s_dn):
    return x * c + pltpu.roll(x, HEAD_LANES - MLA_ROPE // 2, 1) * s_up + pltpu.roll(x, MLA_ROPE // 2, 1) * s_dn


def _mla_proj_kernel(p_ref, cqn_ref, ckvn_ref, wuq_ref, wk_ref, wv_ref, qg_ref, kg_ref,
                     c_ref, su_ref, sd_ref, q_ref, k_ref, v_ref):
    p = p_ref[...].astype(f32)
    cq = _rms(p[:, :MLA_Q_LORA], cqn_ref[...]).astype(bf16)
    ckv = _rms(p[:, MLA_Q_LORA:MLA_Q_LORA + MLA_KV_LORA], ckvn_ref[...]).astype(bf16)
    k_rope = p[:, MLA_Q_LORA + MLA_KV_LORA:]
    q_all = jnp.dot(cq, wuq_ref[...], preferred_element_type=f32)
    k_all = jnp.dot(ckv, wk_ref[...], preferred_element_type=f32)
    v_all = jnp.dot(ckv, wv_ref[...], preferred_element_type=f32)
    c, su, sd = c_ref[...], su_ref[...], sd_ref[...]
    for h in range(MLA_HEADS):
        sl = slice(h * HEAD_LANES, (h + 1) * HEAD_LANES)
        qh = q_all[:, sl]
        kh = k_all[:, sl] + k_rope
        qh = qh * lax.rsqrt(jnp.sum(qh * qh, axis=-1, keepdims=True) * (1.0 / MLA_QK) + EPS) * qg_ref[...]
        kh = kh * lax.rsqrt(jnp.sum(kh * kh, axis=-1, keepdims=True) * (1.0 / MLA_QK) + EPS) * kg_ref[...]
        q_ref[0, h] = _rope(qh, c, su, sd).astype(bf16)
        k_ref[0, h] = _rope(kh, c, su, sd).astype(bf16)
        v_ref[0, h] = v_all[:, sl].astype(bf16)


def _mla_proj(proj, cqn, ckvn, wuq, wk, wv, qg, kg, rope_c, rope_su, rope_sd, batch, lp):
    nt = lp // ROW_TILE
    hl = MLA_HEADS * HEAD_LANES
    head_spec = pl.BlockSpec((1, MLA_HEADS, ROW_TILE, HEAD_LANES), lambda i: (i // nt, 0, i % nt, 0))
    tab_spec = pl.BlockSpec((ROW_TILE, HEAD_LANES), lambda i: (i % nt, 0))
    shp = jax.ShapeDtypeStruct((batch, MLA_HEADS, lp, HEAD_LANES), bf16)
    return pl.pallas_call(
        _mla_proj_kernel,
        out_shape=(shp, shp, shp),
        grid=(batch * nt,),
        in_specs=[
            pl.BlockSpec((ROW_TILE, 512), lambda i: (i, COL_MLA // 512)),
            _resident((1, MLA_Q_LORA)), _resident((1, MLA_KV_LORA)),
            _resident((MLA_Q_LORA, hl)), _resident((MLA_KV_LORA, hl)), _resident((MLA_KV_LORA, hl)),
            _resident((1, HEAD_LANES)), _resident((1, HEAD_LANES)),
            tab_spec, tab_spec, tab_spec,
        ],
        out_specs=(head_spec, head_spec, head_spec),
        compiler_params=_cparams(("parallel",)),
        name="mla_proj",
    )(proj, cqn, ckvn, wuq, wk, wv, qg, kg, rope_c, rope_su, rope_sd)


def _attn_kernel(q_ref, k_ref, v_ref, o_ref, m_sc, l_sc, acc_sc):
    qi = pl.program_id(1)
    rep = KV_TILE // LANES

    def tile(h, t, masked):
        start = pl.multiple_of(t * KV_TILE, KV_TILE)
        q = q_ref[0, h]
        k = k_ref[0, h, pl.ds(start, KV_TILE), :]
        s = lax.dot_general(q, k, (((1,), (1,)), ((), ())), preferred_element_type=f32)
        if masked:
            row = qi * ROW_TILE + lax.broadcasted_iota(jnp.int32, s.shape, 0)
            col = t * KV_TILE + lax.broadcasted_iota(jnp.int32, s.shape, 1)
            s = jnp.where((col <= row) & (col >= PAD), s, NEG_INF)
        m_prev = m_sc[...]
        m_next = jnp.maximum(m_prev, jnp.max(s, axis=-1, keepdims=True))
        alpha = jnp.exp(m_prev - m_next)
        p = jnp.exp(s - jnp.tile(m_next, (1, rep)))
        l_sc[...] = alpha * l_sc[...] + jnp.sum(p, axis=-1, keepdims=True)
        v = v_ref[0, h, pl.ds(start, KV_TILE), :]
        acc_sc[...] = alpha * acc_sc[...] + jnp.dot(p.astype(bf16), v, preferred_element_type=f32)
        m_sc[...] = m_next

    def one_head(h):
        m_sc[...] = jnp.full(m_sc.shape, NEG_INF, f32)
        l_sc[...] = jnp.zeros(l_sc.shape, f32)
        acc_sc[...] = jnp.zeros(acc_sc.shape, f32)
        tile(h, 0, True)

        @pl.when(qi > 0)
        def _():
            def body(t, carry):
                tile(h, t, False)
                return carry
            lax.fori_loop(1, qi, body, 0)
            tile(h, qi, True)
        return acc_sc[...] * (1.0 / l_sc[...])

    def pair(j, carry):
        o_ref[0, j] = (one_head(2 * j) + one_head(2 * j + 1)).astype(bf16)
        return carry

    lax.fori_loop(0, MLA_HEADS // 2, pair, 0)


def _attention(q, k, v):
    batch, _, lp, _ = q.shape
    nt = lp // ROW_TILE
    kv_spec = pl.BlockSpec((1, MLA_HEADS, lp, HEAD_LANES), lambda b, i: (b, 0, 0, 0), pipeline_mode=pl.Buffered(1))
    return pl.pallas_call(
        _attn_kernel,
        out_shape=jax.ShapeDtypeStruct((batch, MLA_HEADS // 2, lp, HEAD_LANES), bf16),
        grid=(batch, nt),
        in_specs=[
            pl.BlockSpec((1, MLA_HEADS, ROW_TILE, HEAD_LANES), lambda b, i: (b, 0, i, 0)),
            kv_spec, kv_spec,
        ],
        out_specs=pl.BlockSpec((1, MLA_HEADS // 2, ROW_TILE, HEAD_LANES), lambda b, i: (b, 0, i, 0)),
        scratch_shapes=[pltpu.VMEM((ROW_TILE, LANES), f32)] * 3,
        compiler_params=_cparams(("parallel", "arbitrary")),
        name="mla_attention",
    )(q, k, v)


def _chunk_cumsum(lg, tril_ref):
    hi = lg.astype(bf16)
    r1 = lg - hi.astype(f32)
    mid = r1.astype(bf16)
    lo = (r1 - mid.astype(f32)).astype(bf16)
    t = tril_ref[...]
    return (jnp.dot(t, hi, preferred_element_type=f32) + jnp.dot(t, mid, preferred_element_type=f32)
            + jnp.dot(t, lo, preferred_element_type=f32))


def _gated_scan(sub_heads, q_sc, k_sc, b_sc, v_ref, s_sc, o_sc):
    n_groups = q_sc.shape[1] // LANES
    gv = sub_heads * LANES
    dk = LANES // sub_heads
    ci = lax.broadcasted_iota(jnp.int32, (CHUNK, CHUNK), 0)
    cj = lax.broadcasted_iota(jnp.int32, (CHUNK, CHUNK), 1)
    causal = ci >= cj
    lane = lax.broadcasted_iota(jnp.int32, (CHUNK, LANES), 1)
    if sub_heads > 1:
        srow = lax.broadcasted_iota(jnp.int32, (gv, LANES), 0) // LANES
        scol = lax.broadcasted_iota(jnp.int32, (gv, LANES), 1) // dk
        own_block = srow == scol

    def chunk(c, carry):
        rows = pl.ds(pl.multiple_of(c * CHUNK, CHUNK), CHUNK)
        b = b_sc[rows, :]
        q = q_sc[rows, :]
        k = k_sc[rows, :]
        v = v_ref[rows, :]
        b_mid = b[CHUNK // 2 - 1:CHUNK // 2, :]
        b_last = b[CHUNK - 1:CHUNK, :]
        qe = (q * jnp.exp(b - b_mid)).astype(bf16)
        ke = (k * jnp.exp(b_mid - b)).astype(bf16)
        kd = (k * jnp.exp(b_last - b)).astype(bf16)
        qb = (q * jnp.exp(b)).astype(bf16)
        decay = jnp.exp(b_last)
        for g in range(n_groups):
            kl = slice(g * LANES, (g + 1) * LANES)
            vg = v[:, g * gv:(g + 1) * gv]
            st = s_sc[g]
            o_inter = lax.dot_general(qb[:, kl], st.astype(bf16), (((1,), (1,)), ((), ())),
                                      preferred_element_type=f32)
            for s in range(sub_heads):
                qs = qe[:, kl]
                if sub_heads > 1:
                    qs = jnp.where(lane // dk == s, qs, jnp.zeros_like(qs))
                a = lax.dot_general(qs, ke[:, kl], (((1,), (1,)), ((), ())), preferred_element_type=f32)
                a = jnp.where(causal, a, 0.0).astype(bf16)
                hs = slice(s * LANES, (s + 1) * LANES)
                o_h = jnp.dot(a, vg[:, hs], preferred_element_type=f32) + o_inter[:, hs]
                head = g * sub_heads + s
                o_sc[rows, head * LANES:(head + 1) * LANES] = o_h
            d_state = lax.dot_general(vg, kd[:, kl], (((0,), (0,)), ((), ())), preferred_element_type=f32)
            if sub_heads > 1:
                d_state = jnp.where(own_block, d_state, 0.0)
            s_sc[g] = st * decay[:, kl] + d_state
        return carry

    lax.fori_loop(0, q_sc.shape[0] // CHUNK, chunk, 0)


def _gate_and_norm(o_sc, gain_ref, r_ref, out_ref, n_heads):
    for h in range(n_heads):
        sl = slice(h * LANES, (h + 1) * LANES)
        out_ref[:, sl] = (_rms(o_sc[:, sl], gain_ref[...]) * _silu(r_ref[:, sl].astype(f32))).astype(bf16)


def _pad_row_mask(shape):
    row = pl.program_id(1) * ROW_TILE + lax.broadcasted_iota(jnp.int32, shape, 0)
    return row >= PAD


def _gla_kernel(q_ref, k_ref, v_ref, r_ref, gd_ref, wg2_ref, bg_ref, gain_ref, tril_ref, out_ref,
                q_sc, k_sc, b_sc, s_sc, o_sc):
    @pl.when(pl.program_id(1) == 0)
    def _():
        s_sc[...] = jnp.zeros(s_sc.shape, f32)

    x = jnp.dot(gd_ref[...], wg2_ref[...], preferred_element_type=f32) + bg_ref[...]
    log_g = (jnp.minimum(x, 0.0) - jnp.log(1.0 + jnp.exp(-jnp.abs(x)))) * (1.0 / GLA_GATE_NORM)
    log_g = jnp.where(_pad_row_mask(log_g.shape), log_g, 0.0)
    b_sc[...] = _chunk_cumsum(log_g, tril_ref)
    q_sc[...] = q_ref[...].astype(f32) * (GLA_DK ** -0.5)
    k_sc[...] = k_ref[...].astype(f32)
    _gated_scan(LANES // GLA_DK, q_sc, k_sc, b_sc, v_ref, s_sc, o_sc)
    _gate_and_norm(o_sc, gain_ref, r_ref, out_ref, GLA_HEADS)


def _hgrn_kernel(q_ref, f_ref, v_ref, r_ref, lb_ref, gain_ref, tril_ref, out_ref,
                 q_sc, k_sc, b_sc, s_sc, o_sc):
    @pl.when(pl.program_id(1) == 0)
    def _():
        s_sc[...] = jnp.zeros(s_sc.shape, f32)

    lb = lb_ref[...]
    f = lb + (1.0 - lb) * _sigmoid(f_ref[...].astype(f32))
    log_f = jnp.where(_pad_row_mask(f.shape), jnp.log(f), 0.0)
    b_sc[...] = _chunk_cumsum(log_f, tril_ref)
    q_sc[...] = _silu(q_ref[...].astype(f32))
    k_sc[...] = 1.0 - f
    _gated_scan(1, q_sc, k_sc, b_sc, v_ref, s_sc, o_sc)
    _gate_and_norm(o_sc, gain_ref, r_ref, out_ref, HGRN_HEADS)


def _seq_col_spec(width, col, nt):
    return pl.BlockSpec((ROW_TILE, width), lambda b, i: (b * nt + i, col // width))


def _gla(proj, wg2, bg, gain, tril, batch, lp):
    nt = lp // ROW_TILE
    return pl.pallas_call(
        _gla_kernel,
        out_shape=jax.ShapeDtypeStruct((batch * lp, GLA_VW), bf16),
        grid=(batch, nt),
        in_specs=[
            _seq_col_spec(GLA_KW, COL_GLA_Q, nt), _seq_col_spec(GLA_KW, COL_GLA_K, nt),
            _seq_col_spec(GLA_VW, COL_GLA_V, nt), _seq_col_spec(GLA_VW, COL_GLA_R, nt),
            _seq_col_spec(LANES, COL_GLA_GD, nt),
            _resident((LANES, GLA_KW)), _resident((1, GLA_KW)), _resident((1, GLA_DV)),
            _resident((ROW_TILE, ROW_TILE)),
        ],
        out_specs=pl.BlockSpec((ROW_TILE, GLA_VW), lambda b, i: (b * nt + i, 0)),
        scratch_shapes=[
            pltpu.VMEM((ROW_TILE, GLA_KW), f32), pltpu.VMEM((ROW_TILE, GLA_KW), f32),
            pltpu.VMEM((ROW_TILE, GLA_KW), f32),
            pltpu.VMEM((GLA_KW // LANES, 2 * LANES, LANES), f32),
            pltpu.VMEM((ROW_TILE, GLA_VW), f32),
        ],
        compiler_params=_cparams(("parallel", "arbitrary")),
        name="gla",
    )(proj, proj, proj, proj, proj, wg2, bg, gain, tril)


def _hgrn(proj, lb, gain, tril, batch, lp):
    nt = lp // ROW_TILE
    return pl.pallas_call(
        _hgrn_kernel,
        out_shape=jax.ShapeDtypeStruct((batch * lp, HGRN_VW), bf16),
        grid=(batch, nt),
        in_specs=[
            _seq_col_spec(HGRN_KW, COL_HG_Q, nt), _seq_col_spec(HGRN_KW, COL_HG_F, nt),
            _seq_col_spec(HGRN_VW, COL_HG_I, nt), _seq_col_spec(HGRN_VW, COL_HG_G, nt),
            _resident((1, HGRN_KW)), _resident((1, LANES)), _resident((ROW_TILE, ROW_TILE)),
        ],
        out_specs=pl.BlockSpec((ROW_TILE, HGRN_VW), lambda b, i: (b * nt + i, 0)),
        scratch_shapes=[
            pltpu.VMEM((ROW_TILE, HGRN_KW), f32), pltpu.VMEM((ROW_TILE, HGRN_KW), f32),
            pltpu.VMEM((ROW_TILE, HGRN_KW), f32),
            pltpu.VMEM((HGRN_KW // LANES, LANES, LANES), f32),
            pltpu.VMEM((ROW_TILE, HGRN_VW), f32),
        ],
        compiler_params=_cparams(("parallel", "arbitrary")),
        name="hgrn2",
    )(proj, proj, proj, proj, lb, gain, tril)


def _mix_kernel(h_ref, om_ref, og_ref, oh_ref, g0_ref, g1_ref, g2_ref, wm_ref, wg_ref, wh_ref, wo_ref, out_ref):
    om = jnp.concatenate([om_ref[0, j] for j in range(MLA_HEADS // 2)], axis=-1)
    mixed = (_sigmoid(g0_ref[...].astype(f32)) * jnp.dot(om, wm_ref[...], preferred_element_type=f32)
             + _sigmoid(g1_ref[...].astype(f32)) * jnp.dot(og_ref[...], wg_ref[...], preferred_element_type=f32)
             + _sigmoid(g2_ref[...].astype(f32)) * jnp.dot(oh_ref[...], wh_ref[...], preferred_element_type=f32))
    h2 = h_ref[...] + jnp.dot(mixed.astype(bf16), wo_ref[...], preferred_element_type=f32)
    out_ref[...] = jnp.where(_pad_row_mask(h2.shape), h2, 0.0)


def _mix(h, o_mla, o_gla, o_hgrn, proj, wm, wg, wh, wo, batch, lp):
    nt = lp // ROW_TILE
    row = lambda b, i: (b * nt + i, 0)
    return pl.pallas_call(
        _mix_kernel,
        out_shape=jax.ShapeDtypeStruct((batch * lp, D_MODEL), f32),
        grid=(batch, nt),
        in_specs=[
            pl.BlockSpec((ROW_TILE, D_MODEL), row),
            pl.BlockSpec((1, MLA_HEADS // 2, ROW_TILE, HEAD_LANES), lambda b, i: (b, 0, i, 0)),
            pl.BlockSpec((ROW_TILE, GLA_VW), row), pl.BlockSpec((ROW_TILE, HGRN_VW), row),
            _seq_col_spec(D_MODEL, COL_GATES, nt), _seq_col_spec(D_MODEL, COL_GATES + D_MODEL, nt),
            _seq_col_spec(D_MODEL, COL_GATES + 2 * D_MODEL, nt),
            _resident((MLA_HEADS * MLA_V, D_MODEL)), _resident((GLA_VW, D_MODEL)),
            _resident((HGRN_VW, D_MODEL)), _resident((D_MODEL, D_MODEL)),
        ],
        out_specs=pl.BlockSpec((ROW_TILE, D_MODEL), row),
        compiler_params=_cparams(("parallel", "arbitrary")),
        name="mix_out",
    )(h, o_mla, o_gla, o_hgrn, proj, proj, proj, wm, wg, wh, wo)


def _ffn_kernel(h_ref, g_ref, wg_ref, wu_ref, wd_ref, out_ref):
    h = h_ref[...]
    u = _rms(h, g_ref[...]).astype(bf16)
    acc = h
    for c in range(D_FF // FF_CHUNK):
        sl = slice(c * FF_CHUNK, (c + 1) * FF_CHUNK)
        gate = jnp.dot(u, wg_ref[:, sl], preferred_element_type=f32)
        up = jnp.dot(u, wu_ref[:, sl], preferred_element_type=f32)
        acc = acc + jnp.dot((_silu(gate) * up).astype(bf16), wd_ref[sl, :], preferred_element_type=f32)
    out_ref[...] = acc


def _ffn(h, gain, wg, wu, wd):
    tp = h.shape[0]
    row = lambda i: (i, 0)
    return pl.pallas_call(
        _ffn_kernel,
        out_shape=jax.ShapeDtypeStruct((tp, D_MODEL), f32),
        grid=(tp // ROW_TILE,),
        in_specs=[
            pl.BlockSpec((ROW_TILE, D_MODEL), row), _resident((1, D_MODEL)),
            _resident((D_MODEL, D_FF)), _resident((D_MODEL, D_FF)), _resident((D_FF, D_MODEL)),
        ],
        out_specs=pl.BlockSpec((ROW_TILE, D_MODEL), row),
        compiler_params=_cparams(("parallel",)),
        name="dense_ffn",
    )(h, gain, wg, wu, wd)


def _split3(x):
    hi = x.astype(bf16)
    lo = (x - hi.astype(f32)).astype(bf16)
    return hi, lo


def _router_kernel(h_ref, g_ref, whi_ref, wlo_ref, b_ref, out_ref):
    u = _rms(h_ref[...], g_ref[...])
    u_hi, u_lo = _split3(u)
    logits = (jnp.dot(u_hi, whi_ref[...], preferred_element_type=f32)
              + jnp.dot(u_lo, whi_ref[...], preferred_element_type=f32)
              + jnp.dot(u_hi, wlo_ref[...], preferred_element_type=f32)) + b_ref[...]
    lane = lax.broadcasted_iota(jnp.int32, logits.shape, 1)
    logits = jnp.where(lane < N_EXPERTS, logits, -jnp.inf)
    v1 = jnp.max(logits, axis=-1, keepdims=True)
    i1 = jnp.min(jnp.where(logits == v1, lane, LANES), axis=-1, keepdims=True)
    rest = jnp.where(lane == i1, -jnp.inf, logits)
    v2 = jnp.max(rest, axis=-1, keepdims=True)
    i2 = jnp.min(jnp.where(rest == v2, lane, LANES), axis=-1, keepdims=True)
    e = jnp.exp(v2 - v1)
    g1 = 1.0 / (1.0 + e)
    g2 = e * g1
    out = jnp.where(lane == 0, g1, jnp.where(lane == 1, g2, 0.0))
    out = jnp.where(lane == 2, i1.astype(f32), jnp.where(lane == 3, i2.astype(f32), out))
    out_ref[...] = out


def _router(h, gain, w_hi, w_lo, b):
    tp = h.shape[0]
    return pl.pallas_call(
        _router_kernel,
        out_shape=jax.ShapeDtypeStruct((tp, LANES), f32),
        grid=(tp // ROW_TILE,),
        in_specs=[
            pl.BlockSpec((ROW_TILE, D_MODEL), lambda i: (i, 0)), _resident((1, D_MODEL)),
            _resident((D_MODEL, LANES)), _resident((D_MODEL, LANES)), _resident((1, LANES)),
        ],
        out_specs=pl.BlockSpec((ROW_TILE, LANES), lambda i: (i, 0)),
        compiler_params=_cparams(("parallel",)),
        name="moe_router",
    )(h, gain, w_hi, w_lo, b)


def _gather_rows(idx_ref, src_hbm, dst, sem):
    def body(r, carry):
        pltpu.make_async_copy(src_hbm.at[pl.ds(idx_ref[0, 0, r], 1), :], dst.at[pl.ds(r, 1), :], sem).start()
        return carry
    lax.fori_loop(0, ROW_TILE, body, 0)


def _wait_rows(src_hbm, dst, sem):
    pltpu.make_async_copy(src_hbm.at[pl.ds(0, ROW_TILE), :], dst, sem).wait()


def _expert_kernel(be_ref, nu_ref, tok_ref, tok_next_ref, w_ref, g_ref, h_hbm, wg_ref, wu_ref, wd_ref, y_ref,
                   xbuf, sems, x_sc, acc_sc):
    i = pl.program_id(0)
    k = pl.program_id(1)
    n_used = nu_ref[0]
    slot = i % 2

    @pl.when((k == 0) & (i < n_used))
    def _():
        @pl.when(i == 0)
        def _():
            _gather_rows(tok_ref, h_hbm, xbuf.at[0], sems.at[0])

        @pl.when(i + 1 < n_used)
        def _():
            _gather_rows(tok_next_ref, h_hbm, xbuf.at[1 - slot], sems.at[1 - slot])

        _wait_rows(h_hbm, xbuf.at[slot], sems.at[slot])
        x_sc[...] = _rms(xbuf[slot], g_ref[...]).astype(bf16)
        acc_sc[...] = jnp.zeros(acc_sc.shape, f32)

    @pl.when(i < n_used)
    def _():
        x = x_sc[...]
        gate = jnp.dot(x, wg_ref[...], preferred_element_type=f32)
        up = jnp.dot(x, wu_ref[...], preferred_element_type=f32)
        acc_sc[...] += jnp.dot((_silu(gate) * up).astype(bf16), wd_ref[...], preferred_element_type=f32)

    @pl.when(k == pl.num_programs(1) - 1)
    def _():
        @pl.when(i < n_used)
        def _():
            y_ref[...] = acc_sc[...] * w_ref[...]

        @pl.when(i >= n_used)
        def _():
            y_ref[...] = jnp.zeros(y_ref.shape, f32)


def _experts(block_expert, n_used, slot_row, slot_w, gain, h, wg, wu, wd):
    n_blocks = slot_row.shape[0]
    nk = D_EXPERT // EXP_CHUNK

    def live(i, nu):
        return jnp.minimum(i, nu[0] - 1)

    def kk(i, k, nu):
        return jnp.where(i < nu[0], k, nk - 1)

    grid_spec = pltpu.PrefetchScalarGridSpec(
        num_scalar_prefetch=2,
        grid=(n_blocks, nk),
        in_specs=[
            pl.BlockSpec((1, 1, ROW_TILE), lambda i, k, be, nu: (i, 0, 0), memory_space=pltpu.SMEM),
            pl.BlockSpec((1, 1, ROW_TILE), lambda i, k, be, nu: (jnp.minimum(i + 1, n_blocks - 1), 0, 0),
                         memory_space=pltpu.SMEM),
            pl.BlockSpec((ROW_TILE, 1), lambda i, k, be, nu: (i, 0)),
            pl.BlockSpec((1, D_MODEL), lambda i, k, be, nu: (0, 0)),
            pl.BlockSpec(memory_space=pl.ANY),
            pl.BlockSpec((None, D_MODEL, EXP_CHUNK), lambda i, k, be, nu: (be[live(i, nu)], 0, kk(i, k, nu))),
            pl.BlockSpec((None, D_MODEL, EXP_CHUNK), lambda i, k, be, nu: (be[live(i, nu)], 0, kk(i, k, nu))),
            pl.BlockSpec((None, EXP_CHUNK, D_MODEL), lambda i, k, be, nu: (be[live(i, nu)], kk(i, k, nu), 0)),
        ],
        out_specs=pl.BlockSpec((ROW_TILE, D_MODEL), lambda i, k, be, nu: (i, 0)),
        scratch_shapes=[
            pltpu.VMEM((2, ROW_TILE, D_MODEL), f32),
            pltpu.SemaphoreType.DMA((2,)),
            pltpu.VMEM((ROW_TILE, D_MODEL), bf16),
            pltpu.VMEM((ROW_TILE, D_MODEL), f32),
        ],
    )
    return pl.pallas_call(
        _expert_kernel,
        out_shape=jax.ShapeDtypeStruct((n_blocks * ROW_TILE, D_MODEL), f32),
        grid_spec=grid_spec,
        compiler_params=_cparams(("arbitrary", "arbitrary")),
        name="moe_experts",
    )(block_expert, n_used, slot_row, slot_row, slot_w, gain, h, wg, wu, wd)


def _combine_kernel(s0_ref, s1_ref, h_ref, y_hbm, out_ref, buf0, buf1, sems):
    _gather_rows(s0_ref, y_hbm, buf0, sems.at[0])
    _gather_rows(s1_ref, y_hbm, buf1, sems.at[1])
    _wait_rows(y_hbm, buf0, sems.at[0])
    _wait_rows(y_hbm, buf1, sems.at[1])
    out = h_ref[...] + buf0[...] + buf1[...]
    out_ref[...] = jnp.where(_pad_row_mask(out.shape), out, 0.0)


def _combine(slot0, slot1, h, y, batch, lp):
    nt = lp // ROW_TILE
    idx_spec = pl.BlockSpec((1, 1, ROW_TILE), lambda b, i: (b * nt + i, 0, 0), memory_space=pltpu.SMEM)
    row = lambda b, i: (b * nt + i, 0)
    return pl.pallas_call(
        _combine_kernel,
        out_shape=jax.ShapeDtypeStruct((batch * lp, D_MODEL), f32),
        grid=(batch, nt),
        in_specs=[idx_spec, idx_spec, pl.BlockSpec((ROW_TILE, D_MODEL), row),
                  pl.BlockSpec(memory_space=pl.ANY)],
        out_specs=pl.BlockSpec((ROW_TILE, D_MODEL), row),
        scratch_shapes=[pltpu.VMEM((ROW_TILE, D_MODEL), f32), pltpu.VMEM((ROW_TILE, D_MODEL), f32),
                        pltpu.SemaphoreType.DMA((2,))],
        compiler_params=_cparams(("arbitrary", "arbitrary")),
        name="moe_combine",
    )(slot0, slot1, h, y)


def _moe(h, gain, w_router, b_router, wg, wu, wd, batch, lp):
    seq = lp - PAD
    n_tok = batch * seq
    n_assign = n_tok * TOP_K
    w_r = jnp.pad(w_router, ((0, 0), (0, LANES - N_EXPERTS)))
    w_hi = w_r.astype(bf16)
    w_lo = (w_r - w_hi.astype(f32)).astype(bf16)
    b_r = jnp.pad(b_router, (0, LANES - N_EXPERTS)).reshape(1, LANES)
    routed = _router(h, gain, w_hi, w_lo, b_r).reshape(batch, lp, LANES)[:, PAD:, :4].reshape(n_tok, 4)
    gate = routed[:, :TOP_K]
    top_idx = routed[:, TOP_K:].astype(jnp.int32)

    e_flat = top_idx.reshape(n_assign)
    tok = jnp.repeat(jnp.arange(n_tok, dtype=jnp.int32), TOP_K)
    tok_row = (tok // seq) * lp + PAD + tok % seq
    order = jnp.argsort(e_flat)
    e_sorted = e_flat[order]
    counts = jnp.bincount(e_flat, length=N_EXPERTS).astype(jnp.int32)
    start = jnp.cumsum(counts) - counts
    padded = (counts + ROW_TILE - 1) // ROW_TILE * ROW_TILE
    pend = jnp.cumsum(padded)
    pstart = pend - padded
    dest = pstart[e_sorted] + jnp.arange(n_assign, dtype=jnp.int32) - start[e_sorted]
    n_blocks = -(-n_assign // ROW_TILE) + N_EXPERTS
    n_slots = n_blocks * ROW_TILE
    slot_row = jnp.zeros((n_slots,), jnp.int32).at[dest].set(tok_row[order])
    slot_w = jnp.zeros((n_slots,), f32).at[dest].set(gate.reshape(n_assign)[order])
    block_expert = jnp.minimum(
        jnp.searchsorted(pend, jnp.arange(n_blocks, dtype=jnp.int32) * ROW_TILE, side='right'),
        N_EXPERTS - 1).astype(jnp.int32)
    n_used = (pend[-1] // ROW_TILE).astype(jnp.int32).reshape(1)

    y = _experts(block_expert, n_used, slot_row.reshape(n_blocks, 1, ROW_TILE), slot_w.reshape(n_slots, 1),
                 gain, h, wg, wu, wd)

    slot_of = jnp.zeros((n_assign,), jnp.int32).at[order].set(dest).reshape(batch, seq, TOP_K)
    slot_of = jnp.pad(slot_of, ((0, 0), (PAD, 0), (0, 0))).reshape(batch * lp, TOP_K)
    nt_all = batch * lp // ROW_TILE
    return _combine(slot_of[:, 0].reshape(nt_all, 1, ROW_TILE), slot_of[:, 1].reshape(nt_all, 1, ROW_TILE), h, y,
                    batch, lp)


def _pad_cols(w, width):
    return jnp.pad(w, ((0, 0), (0, width - w.shape[1])))


def _in_proj_weight(w):
    offs = np.cumsum((MLA_Q_LORA, MLA_KV_LORA, MLA_ROPE, GLA_KW, GLA_KW, GLA_VW, GLA_GATE_RANK, GLA_VW,
                      HGRN_KW, HGRN_KW, HGRN_VW, HGRN_VW, N_BRANCH * D_MODEL))[:-1]
    (dq, dkv, kr, gq, gk, gv, ggd, gr, hq, hf, hi, hg, gates) = jnp.split(w, offs, axis=1)
    kr_p = jnp.pad(kr, ((0, 0), (MLA_NOPE, HEAD_LANES - MLA_NOPE - MLA_ROPE)))
    parts = [dq, dkv, kr_p, gq, gk, gv, gr, hq, hf, hi, hg, gates, _pad_cols(ggd, LANES)]
    return jnp.concatenate(parts, axis=1).astype(bf16)


def _mla_weights(w_uq, w_ukv):
    d_in = w_uq.shape[0]
    wq = w_uq.reshape(d_in, MLA_HEADS, MLA_QK)
    wq = jnp.pad(wq, ((0, 0), (0, 0), (0, HEAD_LANES - MLA_QK))).reshape(d_in, MLA_HEADS * HEAD_LANES)
    d_kv = w_ukv.shape[0]
    wkv = w_ukv.reshape(d_kv, MLA_HEADS, MLA_NOPE + MLA_V)
    wk = jnp.pad(wkv[:, :, :MLA_NOPE], ((0, 0), (0, 0), (0, HEAD_LANES - MLA_NOPE)))
    wv = wkv[:, :, MLA_NOPE:]
    zeros = jnp.zeros_like(wv)
    even = (jnp.arange(MLA_HEADS) % 2 == 0)[None, :, None]
    wv = jnp.concatenate([jnp.where(even, wv, zeros), jnp.where(even, zeros, wv)], axis=-1)
    return (wq.astype(bf16), wk.reshape(d_kv, -1).astype(bf16), wv.reshape(d_kv, -1).astype(bf16))


def _head_gain(g, scale):
    return (jnp.pad(g, (0, HEAD_LANES - MLA_QK)) * scale).reshape(1, HEAD_LANES).astype(f32)


def _rope_tables(lp):
    half = MLA_ROPE // 2
    pos = np.maximum(np.arange(lp) - PAD, 0).astype(np.float32)
    inv = (ROPE_THETA ** (-np.arange(half, dtype=np.float32) / half)).astype(np.float32)
    ang = jnp.asarray(pos)[:, None] * jnp.asarray(inv)[None, :]
    cos, sin = jnp.cos(ang), jnp.sin(ang)
    ones = jnp.ones((lp, MLA_NOPE), f32)
    z = lambda n: jnp.zeros((lp, n), f32)
    tail = HEAD_LANES - MLA_QK
    c = jnp.concatenate([ones, cos, cos, jnp.ones((lp, tail), f32)], axis=1)
    s_up = jnp.concatenate([z(MLA_NOPE), -sin, z(half), z(tail)], axis=1)
    s_dn = jnp.concatenate([z(MLA_NOPE), z(half), sin, z(tail)], axis=1)
    return c, s_up, s_dn


def _chunk_tril():
    r = np.arange(ROW_TILE)
    t = (r[:, None] >= r[None, :]) & (r[:, None] // CHUNK == r[None, :] // CHUNK)
    return jnp.asarray(t, dtype=bf16)


def kernel(x, meta_tokens, attn_norm, w_in, mla_cq_norm, w_mla_uq, mla_ckv_norm, w_mla_ukv, mla_q_norm, mla_k_norm, w_gla_g2, b_gla_g, gla_o_norm, hgrn_lower_bounds, hgrn_o_norm, w_mla_o, w_gla_o, w_hgrn_o, w_out, ffn_norm, w_ff_gate, w_ff_up, w_ff_down, w_router, b_router, w_exp_gate, w_exp_up, w_exp_down):
    batch, seq, _ = x.shape
    depth = w_in.shape[0]
    lp = PAD + N_META + seq
    assert lp % ROW_TILE == 0 and ROW_TILE % CHUNK == 0 and KV_TILE == ROW_TILE

    meta = jnp.broadcast_to(meta_tokens[None].astype(x.dtype), (batch, N_META, D_MODEL))
    h = jnp.concatenate([jnp.zeros((batch, PAD, D_MODEL), x.dtype), meta, x], axis=1).reshape(batch * lp, D_MODEL)

    lbs = jax.nn.softmax(hgrn_lower_bounds.astype(f32), axis=0)
    lbs = jnp.cumsum(lbs, axis=0) - lbs[0]
    rope_c, rope_su, rope_sd = _rope_tables(lp)
    tril = _chunk_tril()
    row1 = lambda v: v.reshape(1, -1).astype(f32)

    for layer in range(depth):
        proj = _in_proj(h, row1(attn_norm[layer]), _in_proj_weight(w_in[layer]))
        wq, wk, wv = _mla_weights(w_mla_uq[layer], w_mla_ukv[layer])
        q, k, v = _mla_proj(proj, row1(mla_cq_norm[layer]), row1(mla_ckv_norm[layer]), wq, wk, wv,
                            _head_gain(mla_q_norm[layer], MLA_QK ** -0.5), _head_gain(mla_k_norm[layer], 1.0),
                            rope_c, rope_su, rope_sd, batch, lp)
        o_mla = _attention(q, k, v)
        wg2 = jnp.pad(w_gla_g2[layer], ((0, LANES - GLA_GATE_RANK), (0, 0))).astype(bf16)
        o_gla = _gla(proj, wg2, row1(b_gla_g[layer]), row1(gla_o_norm[layer]), tril, batch, lp)
        o_hgrn = _hgrn(proj, row1(lbs[layer]), row1(hgrn_o_norm[layer]), tril, batch, lp)
        h = _mix(h, o_mla, o_gla, o_hgrn, proj, w_mla_o[layer].astype(bf16), w_gla_o[layer].astype(bf16),
                 w_hgrn_o[layer].astype(bf16), w_out[layer].astype(bf16), batch, lp)
        i = layer // 2
        if layer % 2 == 0:
            h = _ffn(h, row1(ffn_norm[layer]), w_ff_gate[i].astype(bf16), w_ff_up[i].astype(bf16),
                     w_ff_down[i].astype(bf16))
        else:
            h = _moe(h, row1(ffn_norm[layer]), w_router[i], b_router[i], w_exp_gate[i].astype(bf16),
                     w_exp_up[i].astype(bf16), w_exp_down[i].astype(bf16), batch, lp)
    return h.reshape(batch, lp, D_MODEL)[:, PAD + N_META:]
```

```python
import functools

import numpy as np
import jax
import jax.numpy as jnp
from jax import lax
from jax.experimental import pallas as pl
from jax.experimental.pallas import tpu as pltpu

f32 = jnp.float32
bf16 = jnp.bfloat16

D_MODEL = 1024
N_META = 16
PAD = 112
CHUNK = 64
EPS = 1e-6
NEG_INF = -1e30
LOG2_E = 1.4426950408889634
MLA_HEADS = 8
MLA_Q_LORA = 256
MLA_KV_LORA = 128
MLA_NOPE = 64
MLA_ROPE = 32
MLA_V = 64
MLA_QK = MLA_NOPE + MLA_ROPE
ROPE_THETA = 10000.0
GLA_HEADS = 4
GLA_DK = 64
GLA_DV = 128
GLA_KW = GLA_HEADS * GLA_DK
GLA_VW = GLA_HEADS * GLA_DV
GLA_GATE_RANK = 16
GLA_GATE_NORM = 16.0
HGRN_HEADS = 4
HGRN_DK = 128
HGRN_KW = 512
HGRN_VW = 512
N_BRANCH = 3
D_FF = 2816
N_EXPERTS = 8
TOP_K = 2
D_EXPERT = 3584

LANES = 128
HEAD_LANES = 128
VMEM_LIMIT = 56 * 1024 * 1024

ROW_TILE = 640
KV_TILE = 640
ATTN_GROUP = 2
FF_CHUNK = 1408
EXP_CHUNK = 1792

COL_MLA = 0
COL_GLA_Q = 512
COL_GLA_K = 768
COL_GLA_V = 1024
COL_GLA_R = 1536
COL_HG_Q = 2048
COL_HG_F = 2560
COL_HG_I = 3072
COL_HG_G = 3584
COL_GATES = 4096
COL_GLA_GD = 7168
IN_COLS_P = 7296
IN_COL_TILE = 2432


def _cparams(sem):
    return pltpu.CompilerParams(dimension_semantics=sem, vmem_limit_bytes=VMEM_LIMIT)


def _resident(shape):
    nd = len(shape)
    return pl.BlockSpec(shape, lambda *_: (0,) * nd, pipeline_mode=pl.Buffered(1))


def _silu(x):
    return x * (1.0 / (1.0 + jnp.exp(-x)))


def _sigmoid(x):
    return 1.0 / (1.0 + jnp.exp(-x))


def _rms(x, gain):
    ms = jnp.mean(x * x, axis=-1, keepdims=True)
    return x * lax.rsqrt(ms + EPS) * gain


def _in_proj_kernel(h_ref, g_ref, w_ref, o_ref, u_sc):
    @pl.when(pl.program_id(1) == 0)
    def _():
        u_sc[...] = _rms(h_ref[...], g_ref[...]).astype(bf16)

    o_ref[...] = jnp.dot(u_sc[...], w_ref[...], preferred_element_type=f32).astype(bf16)


def _in_proj(h, gain, w):
    tp = h.shape[0]
    return pl.pallas_call(
        _in_proj_kernel,
        out_shape=jax.ShapeDtypeStruct((tp, IN_COLS_P), bf16),
        grid=(tp // ROW_TILE, IN_COLS_P // IN_COL_TILE),
        in_specs=[
            pl.BlockSpec((ROW_TILE, D_MODEL), lambda i, j: (i, 0)),
            pl.BlockSpec((1, D_MODEL), lambda i, j: (0, 0)),
            pl.BlockSpec((D_MODEL, IN_COL_TILE), lambda i, j: (0, j)),
        ],
        out_specs=pl.BlockSpec((ROW_TILE, IN_COL_TILE), lambda i, j: (i, j)),
        scratch_shapes=[pltpu.VMEM((ROW_TILE, D_MODEL), bf16)],
        compiler_params=_cparams(("parallel", "arbitrary")),
        name="in_proj",
    )(h, gain, w)


def _rope(x, c, s_up, s_dn):
    return x * c + pltpu.roll(x, HEAD_LANES - MLA_ROPE // 2, 1) * s_up + pltpu.roll(x, MLA_ROPE // 2, 1) * s_dn


def _mla_proj_kernel(p_ref, cqn_ref, ckvn_ref, wuq_ref, wk_ref, wv_ref, qg_ref, kg_ref,
                     c_ref, su_ref, sd_ref, q_ref, k_ref, v_ref):
    p = p_ref[...].astype(f32)
    cq = _rms(p[:, :MLA_Q_LORA], cqn_ref[...]).astype(bf16)
    ckv = _rms(p[:, MLA_Q_LORA:MLA_Q_LORA + MLA_KV_LORA], ckvn_ref[...]).astype(bf16)
    k_rope = p[:, MLA_Q_LORA + MLA_KV_LORA:]
    q_all = jnp.dot(cq, wuq_ref[...], preferred_element_type=f32)
    k_all = jnp.dot(ckv, wk_ref[...], preferred_element_type=f32)
    v_all = jnp.dot(ckv, wv_ref[...], preferred_element_type=f32)
    c, su, sd = c_ref[...], su_ref[...], sd_ref[...]
    lane = lax.broadcasted_iota(jnp.int32, c.shape, 1)
    for h in range(MLA_HEADS):
        sl = slice(h * HEAD_LANES, (h + 1) * HEAD_LANES)
        qh = q_all[:, sl]
        kh = k_all[:, sl] + k_rope
        qh = qh * lax.rsqrt(jnp.sum(qh * qh, axis=-1, keepdims=True) * (1.0 / MLA_QK) + EPS) * qg_ref[...]
        kh = kh * lax.rsqrt(jnp.sum(kh * kh, axis=-1, keepdims=True) * (1.0 / MLA_QK) + EPS) * kg_ref[...]
        q_ref[0, h] = _rope(qh, c, su, sd).astype(bf16)
        k_ref[0, h] = _rope(kh, c, su, sd).astype(bf16)
        one_lane = LANES - 1 if h % 2 == 0 else 0
        v_ref[0, h] = jnp.where(lane == one_lane, 1.0, v_all[:, sl]).astype(bf16)


def _mla_proj(proj, cqn, ckvn, wuq, wk, wv, qg, kg, rope_c, rope_su, rope_sd, batch, lp):
    nt = lp // ROW_TILE
    hl = MLA_HEADS * HEAD_LANES
    head_spec = pl.BlockSpec((1, MLA_HEADS, ROW_TILE, HEAD_LANES), lambda i: (i // nt, 0, i % nt, 0))
    tab_spec = pl.BlockSpec((ROW_TILE, HEAD_LANES), lambda i: (i % nt, 0))
    shp = jax.ShapeDtypeStruct((batch, MLA_HEADS, lp, HEAD_LANES), bf16)
    return pl.pallas_call(
        _mla_proj_kernel,
        out_shape=(shp, shp, shp),
        grid=(batch * nt,),
        in_specs=[
            pl.BlockSpec((ROW_TILE, 512), lambda i: (i, COL_MLA // 512)),
            _resident((1, MLA_Q_LORA)), _resident((1, MLA_KV_LORA)),
            _resident((MLA_Q_LORA, hl)), _resident((MLA_KV_LORA, hl)), _resident((MLA_KV_LORA, hl)),
            _resident((1, HEAD_LANES)), _resident((1, HEAD_LANES)),
            tab_spec, tab_spec, tab_spec,
        ],
        out_specs=(head_spec, head_spec, head_spec),
        compiler_params=_cparams(("parallel",)),
        name="mla_proj",
    )(proj, cqn, ckvn, wuq, wk, wv, qg, kg, rope_c, rope_su, rope_sd)


MASK_PAD, MASK_PAD_CAUSAL, MASK_CAUSAL = 0, 1, 2


def _attn_bias():
    r = np.arange(ROW_TILE)[:, None]
    c = np.arange(KV_TILE)[None, :]
    keep = np.stack([np.broadcast_to(c >= PAD, (ROW_TILE, KV_TILE)), (c >= PAD) & (c <= r), c <= r])
    return jnp.asarray(np.where(keep, 0.0, NEG_INF), dtype=f32)


def _attn_kernel(q_ref, k_ref, v_ref, bias_ref, o_ref, m_sc, acc_sc, s_sc):
    qi = pl.program_id(2)
    rep = KV_TILE // LANES
    heads = range(ATTN_GROUP)

    def scores(t):
        start = pl.multiple_of(t * KV_TILE, KV_TILE)
        return [lax.dot_general(q_ref[0, hh], k_ref[0, hh, pl.ds(start, KV_TILE), :],
                                (((1,), (1,)), ((), ())), preferred_element_type=f32) for hh in heads]

    def absorb(t, s_all, mask):
        start = pl.multiple_of(t * KV_TILE, KV_TILE)
        for hh in heads:
            s = s_all[hh]
            if mask is not None:
                s = s + bias_ref[mask]
            m_prev = m_sc[hh]
            m_next = jnp.maximum(m_prev, jnp.max(s, axis=-1, keepdims=True))
            alpha = jnp.exp2(m_prev - m_next)
            p = jnp.exp2(s - jnp.tile(m_next, (1, rep))).astype(bf16)
            v = v_ref[0, hh, pl.ds(start, KV_TILE), :]
            acc_sc[hh] = alpha * acc_sc[hh] + jnp.dot(p, v, preferred_element_type=f32)
            m_sc[hh] = m_next

    def stash(s_all):
        for hh in heads:
            s_sc[hh] = s_all[hh]

    def step(t, mask):
        cur = [s_sc[hh] for hh in heads]
        stash(scores(t + 1))
        absorb(t, cur, mask)

    m_sc[...] = jnp.full(m_sc.shape, NEG_INF, f32)
    acc_sc[...] = jnp.zeros(acc_sc.shape, f32)
    first = scores(0)

    @pl.when(qi == 0)
    def _():
        absorb(0, first, MASK_PAD_CAUSAL)

    @pl.when(qi > 0)
    def _():
        stash(first)
        step(0, MASK_PAD)

        def body(t, c):
            step(t, None)
            return c
        lax.fori_loop(1, qi, body, 0)
        absorb(qi, [s_sc[hh] for hh in heads], MASK_CAUSAL)

    for pr in range(ATTN_GROUP // 2):
        a0, a1 = acc_sc[2 * pr], acc_sc[2 * pr + 1]
        lane = lax.broadcasted_iota(jnp.int32, a0.shape, 1)
        o_ref[0, pr] = jnp.where(lane < MLA_V, a0 * (1.0 / a0[:, LANES - 1:]), a1 * (1.0 / a1[:, :1])).astype(bf16)


def _attention(q, k, v):
    batch, _, lp, _ = q.shape
    nt = lp // ROW_TILE
    whole = lambda b, g, i: (b, g, 0, 0)
    tile = lambda b, g, i: (b, g, i, 0)
    return pl.pallas_call(
        _attn_kernel,
        out_shape=jax.ShapeDtypeStruct((batch, MLA_HEADS // 2, lp, HEAD_LANES), bf16),
        grid=(batch, MLA_HEADS // ATTN_GROUP, nt),
        in_specs=[
            pl.BlockSpec((1, ATTN_GROUP, ROW_TILE, HEAD_LANES), tile),
            pl.BlockSpec((1, ATTN_GROUP, lp, HEAD_LANES), whole, pipeline_mode=pl.Buffered(1)),
            pl.BlockSpec((1, ATTN_GROUP, lp, HEAD_LANES), whole, pipeline_mode=pl.Buffered(1)),
            _resident((3, ROW_TILE, KV_TILE)),
        ],
        out_specs=pl.BlockSpec((1, ATTN_GROUP // 2, ROW_TILE, HEAD_LANES), tile),
        scratch_shapes=[pltpu.VMEM((ATTN_GROUP, ROW_TILE, LANES), f32)] * 2
        + [pltpu.VMEM((ATTN_GROUP, ROW_TILE, KV_TILE), f32)],
        compiler_params=_cparams(("parallel", "parallel", "arbitrary")),
        name="mla_attention",
    )(q, k, v, _attn_bias())


def _chunk_cumsum(lg, tril_ref):
    hi = lg.astype(bf16)
    r1 = lg - hi.astype(f32)
    mid = r1.astype(bf16)
    lo = (r1 - mid.astype(f32)).astype(bf16)
    t = tril_ref[...]
    return (jnp.dot(t, hi, preferred_element_type=f32) + jnp.dot(t, mid, preferred_element_type=f32)
            + jnp.dot(t, lo, preferred_element_type=f32))


def _gated_scan(sub_heads, q_sc, k_sc, b_sc, v_ref, s_sc, o_sc):
    n_groups = q_sc.shape[1] // LANES
    gv = sub_heads * LANES
    dk = LANES // sub_heads
    ci = lax.broadcasted_iota(jnp.int32, (CHUNK, CHUNK), 0)
    cj = lax.broadcasted_iota(jnp.int32, (CHUNK, CHUNK), 1)
    causal = ci >= cj
    lane = lax.broadcasted_iota(jnp.int32, (CHUNK, LANES), 1)
    if sub_heads > 1:
        srow = lax.broadcasted_iota(jnp.int32, (gv, LANES), 0) // LANES
        scol = lax.broadcasted_iota(jnp.int32, (gv, LANES), 1) // dk
        own_block = srow == scol

    def chunk(c, carry):
        rows = pl.ds(pl.multiple_of(c * CHUNK, CHUNK), CHUNK)
        b = b_sc[rows, :]
        q = q_sc[rows, :]
        k = k_sc[rows, :]
        v = v_ref[rows, :]
        b_mid = b[CHUNK // 2 - 1:CHUNK // 2, :]
        b_last = b[CHUNK - 1:CHUNK, :]
        qe = (q * jnp.exp(b - b_mid)).astype(bf16)
        ke = (k * jnp.exp(b_mid - b)).astype(bf16)
        kd = (k * jnp.exp(b_last - b)).astype(bf16)
        qb = (q * jnp.exp(b)).astype(bf16)
        decay = jnp.exp(b_last)
        for g in range(n_groups):
            kl = slice(g * LANES, (g + 1) * LANES)
            vg = v[:, g * gv:(g + 1) * gv]
            st = s_sc[g]
            o_inter = lax.dot_general(qb[:, kl], st.astype(bf16), (((1,), (1,)), ((), ())),
                                      preferred_element_type=f32)
            for s in range(sub_heads):
                qs = qe[:, kl]
                if sub_heads > 1:
                    qs = jnp.where(lane // dk == s, qs, jnp.zeros_like(qs))
                a = lax.dot_general(qs, ke[:, kl], (((1,), (1,)), ((), ())), preferred_element_type=f32)
                a = jnp.where(causal, a, 0.0).astype(bf16)
                hs = slice(s * LANES, (s + 1) * LANES)
                o_h = jnp.dot(a, vg[:, hs], preferred_element_type=f32) + o_inter[:, hs]
                head = g * sub_heads + s
                o_sc[rows, head * LANES:(head + 1) * LANES] = o_h
            d_state = lax.dot_general(vg, kd[:, kl], (((0,), (0,)), ((), ())), preferred_element_type=f32)
            if sub_heads > 1:
                d_state = jnp.where(own_block, d_state, 0.0)
            s_sc[g] = st * decay[:, kl] + d_state
        return carry

    lax.fori_loop(0, q_sc.shape[0] // CHUNK, chunk, 0)


def _gate_and_norm(o_sc, gain_ref, r_ref, out_ref, n_heads):
    for h in range(n_heads):
        sl = slice(h * LANES, (h + 1) * LANES)
        out_ref[:, sl] = (_rms(o_sc[:, sl], gain_ref[...]) * _silu(r_ref[:, sl].astype(f32))).astype(bf16)


def _pad_row_mask(shape):
    row = pl.program_id(1) * ROW_TILE + lax.broadcasted_iota(jnp.int32, shape, 0)
    return row >= PAD


def _gla_kernel(q_ref, k_ref, v_ref, r_ref, gd_ref, wg2_ref, bg_ref, gain_ref, tril_ref, out_ref,
                q_sc, k_sc, b_sc, s_sc, o_sc):
    @pl.when(pl.program_id(1) == 0)
    def _():
        s_sc[...] = jnp.zeros(s_sc.shape, f32)

    x = jnp.dot(gd_ref[...], wg2_ref[...], preferred_element_type=f32) + bg_ref[...]
    log_g = (jnp.minimum(x, 0.0) - jnp.log(1.0 + jnp.exp(-jnp.abs(x)))) * (1.0 / GLA_GATE_NORM)
    log_g = jnp.where(_pad_row_mask(log_g.shape), log_g, 0.0)
    b_sc[...] = _chunk_cumsum(log_g, tril_ref)
    q_sc[...] = q_ref[...].astype(f32) * (GLA_DK ** -0.5)
    k_sc[...] = k_ref[...].astype(f32)
    _gated_scan(LANES // GLA_DK, q_sc, k_sc, b_sc, v_ref, s_sc, o_sc)
    _gate_and_norm(o_sc, gain_ref, r_ref, out_ref, GLA_HEADS)


def _hgrn_kernel(q_ref, f_ref, v_ref, r_ref, lb_ref, gain_ref, tril_ref, out_ref,
                 q_sc, k_sc, b_sc, s_sc, o_sc):
    @pl.when(pl.program_id(1) == 0)
    def _():
        s_sc[...] = jnp.zeros(s_sc.shape, f32)

    lb = lb_ref[...]
    f = lb + (1.0 - lb) * _sigmoid(f_ref[...].astype(f32))
    log_f = jnp.where(_pad_row_mask(f.shape), jnp.log(f), 0.0)
    b_sc[...] = _chunk_cumsum(log_f, tril_ref)
    q_sc[...] = _silu(q_ref[...].astype(f32))
    k_sc[...] = 1.0 - f
    _gated_scan(1, q_sc, k_sc, b_sc, v_ref, s_sc, o_sc)
    _gate_and_norm(o_sc, gain_ref, r_ref, out_ref, HGRN_HEADS)


def _seq_col_spec(width, col, nt):
    return pl.BlockSpec((ROW_TILE, width), lambda b, i: (b * nt + i, col // width))


def _gla(proj, wg2, bg, gain, tril, batch, lp):
    nt = lp // ROW_TILE
    return pl.pallas_call(
        _gla_kernel,
        out_shape=jax.ShapeDtypeStruct((batch * lp, GLA_VW), bf16),
        grid=(batch, nt),
        in_specs=[
            _seq_col_spec(GLA_KW, COL_GLA_Q, nt), _seq_col_spec(GLA_KW, COL_GLA_K, nt),
            _seq_col_spec(GLA_VW, COL_GLA_V, nt), _seq_col_spec(GLA_VW, COL_GLA_R, nt),
            _seq_col_spec(LANES, COL_GLA_GD, nt),
            _resident((LANES, GLA_KW)), _resident((1, GLA_KW)), _resident((1, GLA_DV)),
            _resident((ROW_TILE, ROW_TILE)),
        ],
        out_specs=pl.BlockSpec((ROW_TILE, GLA_VW), lambda b, i: (b * nt + i, 0)),
        scratch_shapes=[
            pltpu.VMEM((ROW_TILE, GLA_KW), f32), pltpu.VMEM((ROW_TILE, GLA_KW), f32),
            pltpu.VMEM((ROW_TILE, GLA_KW), f32),
            pltpu.VMEM((GLA_KW // LANES, 2 * LANES, LANES), f32),
            pltpu.VMEM((ROW_TILE, GLA_VW), f32),
        ],
        compiler_params=_cparams(("parallel", "arbitrary")),
        name="gla",
    )(proj, proj, proj, proj, proj, wg2, bg, gain, tril)


def _hgrn(proj, lb, gain, tril, batch, lp):
    nt = lp // ROW_TILE
    return pl.pallas_call(
        _hgrn_kernel,
        out_shape=jax.ShapeDtypeStruct((batch * lp, HGRN_VW), bf16),
        grid=(batch, nt),
        in_specs=[
            _seq_col_spec(HGRN_KW, COL_HG_Q, nt), _seq_col_spec(HGRN_KW, COL_HG_F, nt),
            _seq_col_spec(HGRN_VW, COL_HG_I, nt), _seq_col_spec(HGRN_VW, COL_HG_G, nt),
            _resident((1, HGRN_KW)), _resident((1, LANES)), _resident((ROW_TILE, ROW_TILE)),
        ],
        out_specs=pl.BlockSpec((ROW_TILE, HGRN_VW), lambda b, i: (b * nt + i, 0)),
        scratch_shapes=[
            pltpu.VMEM((ROW_TILE, HGRN_KW), f32), pltpu.VMEM((ROW_TILE, HGRN_KW), f32),
            pltpu.VMEM((ROW_TILE, HGRN_KW), f32),
            pltpu.VMEM((HGRN_KW // LANES, LANES, LANES), f32),
            pltpu.VMEM((ROW_TILE, HGRN_VW), f32),
        ],
        compiler_params=_cparams(("parallel", "arbitrary")),
        name="hgrn2",
    )(proj, proj, proj, proj, lb, gain, tril)


def _mix_kernel(h_ref, om_ref, og_ref, oh_ref, g0_ref, g1_ref, g2_ref, wm_ref, wg_ref, wh_ref, wo_ref, out_ref):
    om = jnp.concatenate([om_ref[0, j] for j in range(MLA_HEADS // 2)], axis=-1)
    mixed = (_sigmoid(g0_ref[...].astype(f32)) * jnp.dot(om, wm_ref[...], preferred_element_type=f32)
             + _sigmoid(g1_ref[...].astype(f32)) * jnp.dot(og_ref[...], wg_ref[...], preferred_element_type=f32)
             + _sigmoid(g2_ref[...].astype(f32)) * jnp.dot(oh_ref[...], wh_ref[...], preferred_element_type=f32))
    h2 = h_ref[...] + jnp.dot(mixed.astype(bf16), wo_ref[...], preferred_element_type=f32)
    out_ref[...] = jnp.where(_pad_row_mask(h2.shape), h2, 0.0)


def _mix(h, o_mla, o_gla, o_hgrn, proj, wm, wg, wh, wo, batch, lp):
    nt = lp // ROW_TILE
    row = lambda b, i: (b * nt + i, 0)
    return pl.pallas_call(
        _mix_kernel,
        out_shape=jax.ShapeDtypeStruct((batch * lp, D_MODEL), f32),
        grid=(batch, nt),
        in_specs=[
            pl.BlockSpec((ROW_TILE, D_MODEL), row),
            pl.BlockSpec((1, MLA_HEADS // 2, ROW_TILE, HEAD_LANES), lambda b, i: (b, 0, i, 0)),
            pl.BlockSpec((ROW_TILE, GLA_VW), row), pl.BlockSpec((ROW_TILE, HGRN_VW), row),
            _seq_col_spec(D_MODEL, COL_GATES, nt), _seq_col_spec(D_MODEL, COL_GATES + D_MODEL, nt),
            _seq_col_spec(D_MODEL, COL_GATES + 2 * D_MODEL, nt),
            _resident((MLA_HEADS * MLA_V, D_MODEL)), _resident((GLA_VW, D_MODEL)),
            _resident((HGRN_VW, D_MODEL)), _resident((D_MODEL, D_MODEL)),
        ],
        out_specs=pl.BlockSpec((ROW_TILE, D_MODEL), row),
        compiler_params=_cparams(("parallel", "arbitrary")),
        name="mix_out",
    )(h, o_mla, o_gla, o_hgrn, proj, proj, proj, wm, wg, wh, wo)


def _ffn_kernel(h_ref, g_ref, wg_ref, wu_ref, wd_ref, out_ref):
    h = h_ref[...]
    u = _rms(h, g_ref[...]).astype(bf16)
    acc = h
    for c in range(D_FF // FF_CHUNK):
        sl = slice(c * FF_CHUNK, (c + 1) * FF_CHUNK)
        gate = jnp.dot(u, wg_ref[:, sl], preferred_element_type=f32)
        up = jnp.dot(u, wu_ref[:, sl], preferred_element_type=f32)
        acc = acc + jnp.dot((_silu(gate) * up).astype(bf16), wd_ref[sl, :], preferred_element_type=f32)
    out_ref[...] = acc


def _ffn(h, gain, wg, wu, wd):
    tp = h.shape[0]
    row = lambda i: (i, 0)
    return pl.pallas_call(
        _ffn_kernel,
        out_shape=jax.ShapeDtypeStruct((tp, D_MODEL), f32),
        grid=(tp // ROW_TILE,),
        in_specs=[
            pl.BlockSpec((ROW_TILE, D_MODEL), row), _resident((1, D_MODEL)),
            _resident((D_MODEL, D_FF)), _resident((D_MODEL, D_FF)), _resident((D_FF, D_MODEL)),
        ],
        out_specs=pl.BlockSpec((ROW_TILE, D_MODEL), row),
        compiler_params=_cparams(("parallel",)),
        name="dense_ffn",
    )(h, gain, wg, wu, wd)


def _split3(x):
    hi = x.astype(bf16)
    lo = (x - hi.astype(f32)).astype(bf16)
    return hi, lo


def _router_kernel(h_ref, g_ref, whi_ref, wlo_ref, b_ref, out_ref):
    u = _rms(h_ref[...], g_ref[...])
    u_hi, u_lo = _split3(u)
    logits = (jnp.dot(u_hi, whi_ref[...], preferred_element_type=f32)
              + jnp.dot(u_lo, whi_ref[...], preferred_element_type=f32)
              + jnp.dot(u_hi, wlo_ref[...], preferred_element_type=f32)) + b_ref[...]
    lane = lax.broadcasted_iota(jnp.int32, logits.shape, 1)
    logits = jnp.where(lane < N_EXPERTS, logits, -jnp.inf)
    v1 = jnp.max(logits, axis=-1, keepdims=True)
    i1 = jnp.min(jnp.where(logits == v1, lane, LANES), axis=-1, keepdims=True)
    rest = jnp.where(lane == i1, -jnp.inf, logits)
    v2 = jnp.max(rest, axis=-1, keepdims=True)
    i2 = jnp.min(jnp.where(rest == v2, lane, LANES), axis=-1, keepdims=True)
    e = jnp.exp(v2 - v1)
    g1 = 1.0 / (1.0 + e)
    g2 = e * g1
    out = jnp.where(lane == 0, g1, jnp.where(lane == 1, g2, 0.0))
    out = jnp.where(lane == 2, i1.astype(f32), jnp.where(lane == 3, i2.astype(f32), out))
    out_ref[...] = out


def _router(h, gain, w_hi, w_lo, b):
    tp = h.shape[0]
    return pl.pallas_call(
        _router_kernel,
        out_shape=jax.ShapeDtypeStruct((tp, LANES), f32),
        grid=(tp // ROW_TILE,),
        in_specs=[
            pl.BlockSpec((ROW_TILE, D_MODEL), lambda i: (i, 0)), _resident((1, D_MODEL)),
            _resident((D_MODEL, LANES)), _resident((D_MODEL, LANES)), _resident((1, LANES)),
        ],
        out_specs=pl.BlockSpec((ROW_TILE, LANES), lambda i: (i, 0)),
        compiler_params=_cparams(("parallel",)),
        name="moe_router",
    )(h, gain, w_hi, w_lo, b)


def _gather_rows(idx_ref, src_hbm, dst, sem):
    def body(r, carry):
        pltpu.make_async_copy(src_hbm.at[pl.ds(idx_ref[0, 0, r], 1), :], dst.at[pl.ds(r, 1), :], sem).start()
        return carry
    lax.fori_loop(0, ROW_TILE, body, 0)


def _wait_rows(src_hbm, dst, sem):
    pltpu.make_async_copy(src_hbm.at[pl.ds(0, ROW_TILE), :], dst, sem).wait()


def _expert_kernel(be_ref, nu_ref, tok_ref, tok_next_ref, w_ref, g_ref, h_hbm, wg_ref, wu_ref, wd_ref, y_ref,
                   xbuf, sems, x_sc, acc_sc):
    i = pl.program_id(0)
    k = pl.program_id(1)
    nk = pl.num_programs(1)
    n_used = nu_ref[0]
    slot = i % 2
    rows_per_step = ROW_TILE // (D_EXPERT // EXP_CHUNK)

    @pl.when((k == 0) & (i <= n_used))
    def _():
        @pl.when(i == 0)
        def _():
            _gather_rows(tok_ref, h_hbm, xbuf.at[0], sems.at[0])

        _wait_rows(h_hbm, xbuf.at[slot], sems.at[slot])
        x_sc[...] = _rms(xbuf[slot], g_ref[...]).astype(bf16)
        acc_sc[...] = jnp.zeros(acc_sc.shape, f32)

    @pl.when(i < n_used)
    def _():
        base = k * rows_per_step
        nxt = xbuf.at[1 - slot]
        for r in range(rows_per_step):
            pltpu.make_async_copy(h_hbm.at[pl.ds(tok_next_ref[0, 0, base + r], 1), :],
                                  nxt.at[pl.ds(base + r, 1), :], sems.at[1 - slot]).start()
        x = x_sc[...]
        gate = jnp.dot(x, wg_ref[...], preferred_element_type=f32)
        up = jnp.dot(x, wu_ref[...], preferred_element_type=f32)
        acc_sc[...] += jnp.dot((_silu(gate) * up).astype(bf16), wd_ref[...], preferred_element_type=f32)

    @pl.when(k == nk - 1)
    def _():
        @pl.when(i < n_used)
        def _():
            y_ref[...] = acc_sc[...] * w_ref[...]

        @pl.when(i >= n_used)
        def _():
            y_ref[...] = jnp.zeros(y_ref.shape, f32)


def _experts(block_expert, n_used, slot_row, slot_w, gain, h, wg, wu, wd):
    n_blocks = slot_row.shape[0]
    nk = D_EXPERT // EXP_CHUNK

    def live(i, nu):
        return jnp.minimum(i, nu[0] - 1)

    def kk(i, k, nu):
        return jnp.where(i < nu[0], k, nk - 1)

    grid_spec = pltpu.PrefetchScalarGridSpec(
        num_scalar_prefetch=2,
        grid=(n_blocks, nk),
        in_specs=[
            pl.BlockSpec((1, 1, ROW_TILE), lambda i, k, be, nu: (i, 0, 0), memory_space=pltpu.SMEM),
            pl.BlockSpec((1, 1, ROW_TILE), lambda i, k, be, nu: (jnp.minimum(i + 1, n_blocks - 1), 0, 0),
                         memory_space=pltpu.SMEM),
            pl.BlockSpec((ROW_TILE, 1), lambda i, k, be, nu: (i, 0)),
            pl.BlockSpec((1, D_MODEL), lambda i, k, be, nu: (0, 0)),
            pl.BlockSpec(memory_space=pl.ANY),
            pl.BlockSpec((None, D_MODEL, EXP_CHUNK), lambda i, k, be, nu: (be[live(i, nu)], 0, kk(i, k, nu))),
            pl.BlockSpec((None, D_MODEL, EXP_CHUNK), lambda i, k, be, nu: (be[live(i, nu)], 0, kk(i, k, nu))),
            pl.BlockSpec((None, EXP_CHUNK, D_MODEL), lambda i, k, be, nu: (be[live(i, nu)], kk(i, k, nu), 0)),
        ],
        out_specs=pl.BlockSpec((ROW_TILE, D_MODEL), lambda i, k, be, nu: (i, 0)),
        scratch_shapes=[
            pltpu.VMEM((2, ROW_TILE, D_MODEL), f32),
            pltpu.SemaphoreType.DMA((2,)),
            pltpu.VMEM((ROW_TILE, D_MODEL), bf16),
            pltpu.VMEM((ROW_TILE, D_MODEL), f32),
        ],
    )
    return pl.pallas_call(
        _expert_kernel,
        out_shape=jax.ShapeDtypeStruct((n_blocks * ROW_TILE, D_MODEL), f32),
        grid_spec=grid_spec,
        compiler_params=_cparams(("arbitrary", "arbitrary")),
        name="moe_experts",
    )(block_expert, n_used, slot_row, slot_row, slot_w, gain, h, wg, wu, wd)


def _combine_kernel(s0_ref, s1_ref, h_ref, y_hbm, out_ref, buf0, buf1, sems):
    _gather_rows(s0_ref, y_hbm, buf0, sems.at[0])
    _gather_rows(s1_ref, y_hbm, buf1, sems.at[1])
    _wait_rows(y_hbm, buf0, sems.at[0])
    _wait_rows(y_hbm, buf1, sems.at[1])
    out = h_ref[...] + buf0[...] + buf1[...]
    out_ref[...] = jnp.where(_pad_row_mask(out.shape), out, 0.0)


def _combine(slot0, slot1, h, y, batch, lp):
    nt = lp // ROW_TILE
    idx_spec = pl.BlockSpec((1, 1, ROW_TILE), lambda b, i: (b * nt + i, 0, 0), memory_space=pltpu.SMEM)
    row = lambda b, i: (b * nt + i, 0)
    return pl.pallas_call(
        _combine_kernel,
        out_shape=jax.ShapeDtypeStruct((batch * lp, D_MODEL), f32),
        grid=(batch, nt),
        in_specs=[idx_spec, idx_spec, pl.BlockSpec((ROW_TILE, D_MODEL), row),
                  pl.BlockSpec(memory_space=pl.ANY)],
        out_specs=pl.BlockSpec((ROW_TILE, D_MODEL), row),
        scratch_shapes=[pltpu.VMEM((ROW_TILE, D_MODEL), f32), pltpu.VMEM((ROW_TILE, D_MODEL), f32),
                        pltpu.SemaphoreType.DMA((2,))],
        compiler_params=_cparams(("arbitrary", "arbitrary")),
        name="moe_combine",
    )(slot0, slot1, h, y)


def _moe(h, gain, w_router, b_router, wg, wu, wd, batch, lp):
    seq = lp - PAD
    n_tok = batch * seq
    n_assign = n_tok * TOP_K
    w_r = jnp.pad(w_router, ((0, 0), (0, LANES - N_EXPERTS)))
    w_hi = w_r.astype(bf16)
    w_lo = (w_r - w_hi.astype(f32)).astype(bf16)
    b_r = jnp.pad(b_router, (0, LANES - N_EXPERTS)).reshape(1, LANES)
    routed = _router(h, gain, w_hi, w_lo, b_r).reshape(batch, lp, LANES)[:, PAD:, :4].reshape(n_tok, 4)
    gate = routed[:, :TOP_K]
    top_idx = routed[:, TOP_K:].astype(jnp.int32)

    e_flat = top_idx.reshape(n_assign)
    experts = jnp.arange(N_EXPERTS, dtype=jnp.int32)
    order = jnp.argsort(e_flat)
    onehot = (e_flat[:, None] == experts[None, :]).astype(jnp.int32)
    running = jnp.cumsum(onehot, axis=0)
    counts = running[-1]
    start = jnp.cumsum(counts) - counts
    padded = (counts + ROW_TILE - 1) // ROW_TILE * ROW_TILE
    pend = jnp.cumsum(padded)
    pstart = pend - padded
    n_blocks = -(-n_assign // ROW_TILE) + N_EXPERTS
    n_slots = n_blocks * ROW_TILE

    slot = jnp.arange(n_slots, dtype=jnp.int32)
    slot_e = jnp.minimum(jnp.sum((slot[:, None] >= pend[None, :]).astype(jnp.int32), axis=1), N_EXPERTS - 1)
    within = slot - pstart[slot_e]
    live = (within < counts[slot_e]) & (slot < pend[-1])
    assign = order[jnp.clip(start[slot_e] + within, 0, n_assign - 1)]
    tok = assign // TOP_K
    slot_row = jnp.where(live, (tok // seq) * lp + PAD + tok % seq, 0)
    slot_w = jnp.where(live, gate.reshape(n_assign)[assign], 0.0)
    block_expert = slot_e[::ROW_TILE]
    n_used = (pend[-1] // ROW_TILE).astype(jnp.int32).reshape(1)

    y = _experts(block_expert, n_used, slot_row.reshape(n_blocks, 1, ROW_TILE), slot_w.reshape(n_slots, 1),
                 gain, h, wg, wu, wd)

    rank = jnp.sum(running * onehot, axis=1) - 1
    slot_of = (pstart[e_flat] + rank).reshape(batch, seq, TOP_K)
    slot_of = jnp.pad(slot_of, ((0, 0), (PAD, 0), (0, 0))).reshape(batch * lp, TOP_K)
    nt_all = batch * lp // ROW_TILE
    return _combine(slot_of[:, 0].reshape(nt_all, 1, ROW_TILE), slot_of[:, 1].reshape(nt_all, 1, ROW_TILE), h, y,
                    batch, lp)


def _pad_cols(w, width):
    return jnp.pad(w, ((0, 0), (0, width - w.shape[1])))


def _in_proj_weight(w):
    offs = np.cumsum((MLA_Q_LORA, MLA_KV_LORA, MLA_ROPE, GLA_KW, GLA_KW, GLA_VW, GLA_GATE_RANK, GLA_VW,
                      HGRN_KW, HGRN_KW, HGRN_VW, HGRN_VW, N_BRANCH * D_MODEL))[:-1]
    (dq, dkv, kr, gq, gk, gv, ggd, gr, hq, hf, hi, hg, gates) = jnp.split(w, offs, axis=1)
    kr_p = jnp.pad(kr, ((0, 0), (MLA_NOPE, HEAD_LANES - MLA_NOPE - MLA_ROPE)))
    parts = [dq, dkv, kr_p, gq, gk, gv, gr, hq, hf, hi, hg, gates, _pad_cols(ggd, LANES)]
    return jnp.concatenate(parts, axis=1).astype(bf16)


def _mla_weights(w_uq, w_ukv):
    d_in = w_uq.shape[0]
    wq = w_uq.reshape(d_in, MLA_HEADS, MLA_QK)
    wq = jnp.pad(wq, ((0, 0), (0, 0), (0, HEAD_LANES - MLA_QK))).reshape(d_in, MLA_HEADS * HEAD_LANES)
    d_kv = w_ukv.shape[0]
    wkv = w_ukv.reshape(d_kv, MLA_HEADS, MLA_NOPE + MLA_V)
    wk = jnp.pad(wkv[:, :, :MLA_NOPE], ((0, 0), (0, 0), (0, HEAD_LANES - MLA_NOPE)))
    wv = wkv[:, :, MLA_NOPE:]
    zeros = jnp.zeros_like(wv)
    even = (jnp.arange(MLA_HEADS) % 2 == 0)[None, :, None]
    wv = jnp.concatenate([jnp.where(even, wv, zeros), jnp.where(even, zeros, wv)], axis=-1)
    return (wq.astype(bf16), wk.reshape(d_kv, -1).astype(bf16), wv.reshape(d_kv, -1).astype(bf16))


def _head_gain(g, scale):
    return (jnp.pad(g, (0, HEAD_LANES - MLA_QK)) * scale).reshape(1, HEAD_LANES).astype(f32)


def _rope_tables(lp):
    half = MLA_ROPE // 2
    pos = np.maximum(np.arange(lp) - PAD, 0).astype(np.float32)
    inv = (ROPE_THETA ** (-np.arange(half, dtype=np.float32) / half)).astype(np.float32)
    ang = jnp.asarray(pos)[:, None] * jnp.asarray(inv)[None, :]
    cos, sin = jnp.cos(ang), jnp.sin(ang)
    ones = jnp.ones((lp, MLA_NOPE), f32)
    z = lambda n: jnp.zeros((lp, n), f32)
    tail = HEAD_LANES - MLA_QK
    c = jnp.concatenate([ones, cos, cos, jnp.ones((lp, tail), f32)], axis=1)
    s_up = jnp.concatenate([z(MLA_NOPE), -sin, z(half), z(tail)], axis=1)
    s_dn = jnp.concatenate([z(MLA_NOPE), z(half), sin, z(tail)], axis=1)
    return c, s_up, s_dn


def _chunk_tril():
    r = np.arange(ROW_TILE)
    t = (r[:, None] >= r[None, :]) & (r[:, None] // CHUNK == r[None, :] // CHUNK)
    return jnp.asarray(t, dtype=bf16)


def kernel(x, meta_tokens, attn_norm, w_in, mla_cq_norm, w_mla_uq, mla_ckv_norm, w_mla_ukv, mla_q_norm, mla_k_norm, w_gla_g2, b_gla_g, gla_o_norm, hgrn_lower_bounds, hgrn_o_norm, w_mla_o, w_gla_o, w_hgrn_o, w_out, ffn_norm, w_ff_gate, w_ff_up, w_ff_down, w_router, b_router, w_exp_gate, w_exp_up, w_exp_down):
    batch, seq, _ = x.shape
    depth = w_in.shape[0]
    lp = PAD + N_META + seq
    assert lp % ROW_TILE == 0 and ROW_TILE % CHUNK == 0 and KV_TILE == ROW_TILE

    meta = jnp.broadcast_to(meta_tokens[None].astype(x.dtype), (batch, N_META, D_MODEL))
    h = jnp.concatenate([jnp.zeros((batch, PAD, D_MODEL), x.dtype), meta, x], axis=1).reshape(batch * lp, D_MODEL)

    lbs = jax.nn.softmax(hgrn_lower_bounds.astype(f32), axis=0)
    lbs = jnp.cumsum(lbs, axis=0) - lbs[0]
    rope_c, rope_su, rope_sd = _rope_tables(lp)
    tril = _chunk_tril()
    row1 = lambda v: v.reshape(1, -1).astype(f32)

    for layer in range(depth):
        proj = _in_proj(h, row1(attn_norm[layer]), _in_proj_weight(w_in[layer]))
        wq, wk, wv = _mla_weights(w_mla_uq[layer], w_mla_ukv[layer])
        q, k, v = _mla_proj(proj, row1(mla_cq_norm[layer]), row1(mla_ckv_norm[layer]), wq, wk, wv,
                            _head_gain(mla_q_norm[layer], MLA_QK ** -0.5 * LOG2_E), _head_gain(mla_k_norm[layer], 1.0),
                            rope_c, rope_su, rope_sd, batch, lp)
        o_mla = _attention(q, k, v)
        wg2 = jnp.pad(w_gla_g2[layer], ((0, LANES - GLA_GATE_RANK), (0, 0))).astype(bf16)
        o_gla = _gla(proj, wg2, row1(b_gla_g[layer]), row1(gla_o_norm[layer]), tril, batch, lp)
        o_hgrn = _hgrn(proj, row1(lbs[layer]), row1(hgrn_o_norm[layer]), tril, batch, lp)
        h = _mix(h, o_mla, o_gla, o_hgrn, proj, w_mla_o[layer].astype(bf16), w_gla_o[layer].astype(bf16),
                 w_hgrn_o[layer].astype(bf16), w_out[layer].astype(bf16), batch, lp)
        i = layer // 2
        if layer % 2 == 0:
            h = _ffn(h, row1(ffn_norm[layer]), w_ff_gate[i].astype(bf16), w_ff_up[i].astype(bf16),
                     w_ff_down[i].astype(bf16))
        else:
            h = _moe(h, row1(ffn_norm[layer]), w_router[i], b_router[i], w_exp_gate[i].astype(bf16),
                     w_exp_up[i].astype(bf16), w_exp_down[i].astype(bf16), batch, lp)
    return h.reshape(batch, lp, D_MODEL)[:, PAD + N_META:]
```

```python
import functools

import numpy as np
import jax
import jax.numpy as jnp
from jax import lax
from jax.experimental import pallas as pl
from jax.experimental.pallas import tpu as pltpu

f32 = jnp.float32
bf16 = jnp.bfloat16

D_MODEL = 1024
N_META = 16
PAD = 112
CHUNK = 64
EPS = 1e-6
NEG_INF = -1e30
LOG2_E = 1.4426950408889634
MLA_HEADS = 8
MLA_Q_LORA = 256
MLA_KV_LORA = 128
MLA_NOPE = 64
MLA_ROPE = 32
MLA_V = 64
MLA_QK = MLA_NOPE + MLA_ROPE
ROPE_THETA = 10000.0
GLA_HEADS = 4
GLA_DK = 64
GLA_DV = 128
GLA_KW = GLA_HEADS * GLA_DK
GLA_VW = GLA_HEADS * GLA_DV
GLA_GATE_RANK = 16
GLA_GATE_NORM = 16.0
HGRN_HEADS = 4
HGRN_DK = 128
HGRN_KW = 512
HGRN_VW = 512
N_BRANCH = 3
D_FF = 2816
N_EXPERTS = 8
TOP_K = 2
D_EXPERT = 3584

LANES = 128
SUBLANES = 8
HEAD_LANES = 128
VMEM_LIMIT = 56 * 1024 * 1024

ROW_TILE = 640
KV_TILE = 640
ATTN_GROUP = 2
FF_CHUNK = 1408
EXP_CHUNK = 1792
GATHER_UNROLL = 8

COL_MLA = 0
COL_GLA_Q = 512
COL_GLA_K = 768
COL_GLA_V = 1024
COL_GLA_R = 1536
COL_HG_Q = 2048
COL_HG_F = 2560
COL_HG_I = 3072
COL_HG_G = 3584
COL_GATES = 4096
COL_GLA_GD = 7168
COL_KR_ROT = 7296
IN_COLS_P = 7424
IN_COL_TILE = 3712


def _cparams(sem):
    return pltpu.CompilerParams(dimension_semantics=sem, vmem_limit_bytes=VMEM_LIMIT)


def _resident(shape):
    nd = len(shape)
    return pl.BlockSpec(shape, lambda *_: (0,) * nd, pipeline_mode=pl.Buffered(1))


def _silu(x):
    return x * (1.0 / (1.0 + jnp.exp(-x)))


def _sigmoid(x):
    return 1.0 / (1.0 + jnp.exp(-x))


def _rms(x, gain):
    ms = jnp.mean(x * x, axis=-1, keepdims=True)
    return x * lax.rsqrt(ms + EPS) * gain


def _in_proj_kernel(h_ref, g_ref, w_ref, o_ref, u_sc):
    @pl.when(pl.program_id(1) == 0)
    def _():
        u_sc[...] = _rms(h_ref[...], g_ref[...]).astype(bf16)

    o_ref[...] = jnp.dot(u_sc[...], w_ref[...], preferred_element_type=f32).astype(bf16)


def _in_proj(h, gain, w):
    tp = h.shape[0]
    return pl.pallas_call(
        _in_proj_kernel,
        out_shape=jax.ShapeDtypeStruct((tp, IN_COLS_P), bf16),
        grid=(tp // ROW_TILE, IN_COLS_P // IN_COL_TILE),
        in_specs=[
            pl.BlockSpec((ROW_TILE, D_MODEL), lambda i, j: (i, 0)),
            pl.BlockSpec((1, D_MODEL), lambda i, j: (0, 0)),
            pl.BlockSpec((D_MODEL, IN_COL_TILE), lambda i, j: (0, j)),
        ],
        out_specs=pl.BlockSpec((ROW_TILE, IN_COL_TILE), lambda i, j: (i, j)),
        scratch_shapes=[pltpu.VMEM((ROW_TILE, D_MODEL), bf16)],
        compiler_params=_cparams(("parallel", "arbitrary")),
        name="in_proj",
    )(h, gain, w)


def _head_mean_square(x, ind_ref):
    sq = (x * x).astype(bf16)
    w = ind_ref.shape[0]
    sums = [jnp.dot(sq[:, g * w:(g + 1) * w], ind_ref[...], preferred_element_type=f32)
            for g in range(x.shape[1] // w)]
    return jnp.concatenate(sums, axis=1) * (1.0 / MLA_QK)


def _mla_proj_kernel(p_ref, krr_ref, cqn_ref, ckvn_ref, wuq_ref, wuqr_ref, wk_ref, wv_ref, ind_ref,
                     qg_ref, qgr_ref, kg_ref, kgr_ref, c_ref, s_ref, q_ref, k_ref, v_ref):
    p = p_ref[...].astype(f32)
    cq = _rms(p[:, :MLA_Q_LORA], cqn_ref[...]).astype(bf16)
    ckv = _rms(p[:, MLA_Q_LORA:MLA_Q_LORA + MLA_KV_LORA], ckvn_ref[...]).astype(bf16)
    k_rope = p[:, MLA_Q_LORA + MLA_KV_LORA:]
    q_all = jnp.dot(cq, wuq_ref[...], preferred_element_type=f32)
    q_rot = jnp.dot(cq, wuqr_ref[...], preferred_element_type=f32)
    k_all = jnp.dot(ckv, wk_ref[...], preferred_element_type=f32) + jnp.tile(k_rope, (1, MLA_HEADS))
    v_all = jnp.dot(ckv, wv_ref[...], preferred_element_type=f32)
    q_scale = lax.rsqrt(_head_mean_square(q_all, ind_ref) + EPS)
    k_scale = lax.rsqrt(_head_mean_square(k_all, ind_ref) + EPS)
    c, s = c_ref[...], s_ref[...]
    q_c, q_s = qg_ref[...] * c, qgr_ref[...] * s
    k_c, k_s = kg_ref[...] * c, kgr_ref[...] * s
    k_rot = krr_ref[...].astype(f32) * k_s
    lane = lax.broadcasted_iota(jnp.int32, c.shape, 1)
    for h in range(MLA_HEADS):
        sl = slice(h * HEAD_LANES, (h + 1) * HEAD_LANES)
        q_ref[0, h] = (q_scale[:, sl] * (q_all[:, sl] * q_c + q_rot[:, sl] * q_s)).astype(bf16)
        k_ref[0, h] = (k_scale[:, sl] * (k_all[:, sl] * k_c + k_rot)).astype(bf16)
        one_lane = LANES - 1 if h % 2 == 0 else 0
        v_ref[0, h] = jnp.where(lane == one_lane, 1.0, v_all[:, sl]).astype(bf16)


def _mla_proj(proj, cqn, ckvn, wuq, wuqr, wk, wv, ind, qg, qgr, kg, kgr, rope_c, rope_s, batch, lp):
    nt = lp // ROW_TILE
    hl = MLA_HEADS * HEAD_LANES
    head_spec = pl.BlockSpec((1, MLA_HEADS, ROW_TILE, HEAD_LANES), lambda i: (i // nt, 0, i % nt, 0))
    tab_spec = pl.BlockSpec((ROW_TILE, HEAD_LANES), lambda i: (i % nt, 0))
    gain_spec = _resident((1, HEAD_LANES))
    shp = jax.ShapeDtypeStruct((batch, MLA_HEADS, lp, HEAD_LANES), bf16)
    return pl.pallas_call(
        _mla_proj_kernel,
        out_shape=(shp, shp, shp),
        grid=(batch * nt,),
        in_specs=[
            pl.BlockSpec((ROW_TILE, 512), lambda i: (i, COL_MLA // 512)),
            pl.BlockSpec((ROW_TILE, LANES), lambda i: (i, COL_KR_ROT // LANES)),
            _resident((1, MLA_Q_LORA)), _resident((1, MLA_KV_LORA)),
            _resident((MLA_Q_LORA, hl)), _resident((MLA_Q_LORA, hl)),
            _resident((MLA_KV_LORA, hl)), _resident((MLA_KV_LORA, hl)),
            _resident((2 * HEAD_LANES, 2 * HEAD_LANES)),
            gain_spec, gain_spec, gain_spec, gain_spec,
            tab_spec, tab_spec,
        ],
        out_specs=(head_spec, head_spec, head_spec),
        compiler_params=_cparams(("parallel",)),
        name="mla_proj",
    )(proj, proj, cqn, ckvn, wuq, wuqr, wk, wv, ind, qg, qgr, kg, kgr, rope_c, rope_s)


MASK_PAD, MASK_PAD_CAUSAL, MASK_CAUSAL = 0, 1, 2


def _attn_bias():
    r = np.arange(ROW_TILE)[:, None]
    c = np.arange(KV_TILE)[None, :]
    keep = np.stack([np.broadcast_to(c >= PAD, (ROW_TILE, KV_TILE)), (c >= PAD) & (c <= r), c <= r])
    return jnp.asarray(np.where(keep, 0.0, NEG_INF), dtype=f32)


def _attn_kernel(q_ref, k_ref, v_ref, bias_ref, o_ref, m_sc, acc_sc, s_sc):
    qi = pl.program_id(2)
    rep = KV_TILE // LANES
    heads = range(ATTN_GROUP)

    def scores(t):
        start = pl.multiple_of(t * KV_TILE, KV_TILE)
        return [lax.dot_general(q_ref[0, hh], k_ref[0, hh, pl.ds(start, KV_TILE), :],
                                (((1,), (1,)), ((), ())), preferred_element_type=f32) for hh in heads]

    def absorb(t, s_all, mask):
        start = pl.multiple_of(t * KV_TILE, KV_TILE)
        for hh in heads:
            s = s_all[hh]
            if mask is not None:
                s = s + bias_ref[mask]
            m_prev = m_sc[hh]
            m_next = jnp.maximum(m_prev, jnp.max(s, axis=-1, keepdims=True))
            alpha = jnp.exp2(m_prev - m_next)
            p = jnp.exp2(s - jnp.tile(m_next, (1, rep))).astype(bf16)
            v = v_ref[0, hh, pl.ds(start, KV_TILE), :]
            acc_sc[hh] = alpha * acc_sc[hh] + jnp.dot(p, v, preferred_element_type=f32)
            m_sc[hh] = m_next

    def stash(s_all):
        for hh in heads:
            s_sc[hh] = s_all[hh]

    def step(t, mask):
        cur = [s_sc[hh] for hh in heads]
        stash(scores(t + 1))
        absorb(t, cur, mask)

    m_sc[...] = jnp.full(m_sc.shape, NEG_INF, f32)
    acc_sc[...] = jnp.zeros(acc_sc.shape, f32)
    first = scores(0)

    @pl.when(qi == 0)
    def _():
        absorb(0, first, MASK_PAD_CAUSAL)

    @pl.when(qi > 0)
    def _():
        stash(first)
        step(0, MASK_PAD)

        n_plain = qi - 1

        def body(u, c):
            step(2 * u + 1, None)
            step(2 * u + 2, None)
            return c
        lax.fori_loop(0, n_plain // 2, body, 0)

        @pl.when(n_plain % 2 == 1)
        def _():
            step(qi - 1, None)
        absorb(qi, [s_sc[hh] for hh in heads], MASK_CAUSAL)

    for pr in range(ATTN_GROUP // 2):
        a0, a1 = acc_sc[2 * pr], acc_sc[2 * pr + 1]
        lane = lax.broadcasted_iota(jnp.int32, a0.shape, 1)
        o_ref[0, pr] = jnp.where(lane < MLA_V, a0 * (1.0 / a0[:, LANES - 1:]), a1 * (1.0 / a1[:, :1])).astype(bf16)


def _attention(q, k, v):
    batch, _, lp, _ = q.shape
    nt = lp // ROW_TILE
    whole = lambda b, g, i: (b, g, 0, 0)
    tile = lambda b, g, i: (b, g, i, 0)
    return pl.pallas_call(
        _attn_kernel,
        out_shape=jax.ShapeDtypeStruct((batch, MLA_HEADS // 2, lp, HEAD_LANES), bf16),
        grid=(batch, MLA_HEADS // ATTN_GROUP, nt),
        in_specs=[
            pl.BlockSpec((1, ATTN_GROUP, ROW_TILE, HEAD_LANES), tile),
            pl.BlockSpec((1, ATTN_GROUP, lp, HEAD_LANES), whole, pipeline_mode=pl.Buffered(1)),
            pl.BlockSpec((1, ATTN_GROUP, lp, HEAD_LANES), whole, pipeline_mode=pl.Buffered(1)),
            _resident((3, ROW_TILE, KV_TILE)),
        ],
        out_specs=pl.BlockSpec((1, ATTN_GROUP // 2, ROW_TILE, HEAD_LANES), tile),
        scratch_shapes=[pltpu.VMEM((ATTN_GROUP, ROW_TILE, LANES), f32)] * 2
        + [pltpu.VMEM((ATTN_GROUP, ROW_TILE, KV_TILE), f32)],
        compiler_params=_cparams(("parallel", "parallel", "arbitrary")),
        name="mla_attention",
    )(q, k, v, _attn_bias())


def _chunk_cumsum(lg, tril):
    hi = lg.astype(bf16)
    r1 = lg - hi.astype(f32)
    mid = r1.astype(bf16)
    lo = (r1 - mid.astype(f32)).astype(bf16)
    return (jnp.dot(tril, hi, preferred_element_type=f32) + jnp.dot(tril, mid, preferred_element_type=f32)
            + jnp.dot(tril, lo, preferred_element_type=f32))


def _gated_scan(sub_heads, q_sc, k_sc, lg_sc, v_ref, tril_ref, s_sc, o_sc):
    n_groups = q_sc.shape[1] // LANES
    n_chunks = q_sc.shape[0] // CHUNK
    gv = sub_heads * LANES
    dk = LANES // sub_heads
    ci = lax.broadcasted_iota(jnp.int32, (CHUNK, CHUNK), 0)
    cj = lax.broadcasted_iota(jnp.int32, (CHUNK, CHUNK), 1)
    causal = ci >= cj
    lane = lax.broadcasted_iota(jnp.int32, (CHUNK, LANES), 1)
    if sub_heads > 1:
        srow = lax.broadcasted_iota(jnp.int32, (gv, LANES), 0) // LANES
        scol = lax.broadcasted_iota(jnp.int32, (gv, LANES), 1) // dk
        own_block = srow == scol
    tril = tril_ref[...]
    rows = [slice(c * CHUNK, (c + 1) * CHUNK) for c in range(n_chunks)]
    nt_dims = (((1,), (1,)), ((), ()))
    tn_dims = (((0,), (0,)), ((), ()))

    bs = [_chunk_cumsum(lg_sc[r, :], tril) for r in rows]

    qe, ke, kd, qb, decay = [], [], [], [], []
    for r, b in zip(rows, bs):
        q = q_sc[r, :]
        k = k_sc[r, :]
        b_mid = b[CHUNK // 2 - 1:CHUNK // 2, :]
        b_last = b[CHUNK - 1:CHUNK, :]
        qe.append((q * jnp.exp(b - b_mid)).astype(bf16))
        ke.append((k * jnp.exp(b_mid - b)).astype(bf16))
        kd.append((k * jnp.exp(b_last - b)).astype(bf16))
        qb.append((q * jnp.exp(b)).astype(bf16))
        decay.append(jnp.exp(b_last))

    o_intra = [[None] * (n_groups * sub_heads) for _ in rows]
    d_state = [[None] * n_groups for _ in rows]
    for c, r in enumerate(rows):
        v = v_ref[r, :]
        for g in range(n_groups):
            kl = slice(g * LANES, (g + 1) * LANES)
            vg = v[:, g * gv:(g + 1) * gv]
            for s in range(sub_heads):
                qs = qe[c][:, kl]
                if sub_heads > 1:
                    qs = jnp.where(lane // dk == s, qs, jnp.zeros_like(qs))
                a = lax.dot_general(qs, ke[c][:, kl], nt_dims, preferred_element_type=f32)
                a = jnp.where(causal, a, 0.0).astype(bf16)
                o_intra[c][g * sub_heads + s] = jnp.dot(a, vg[:, s * LANES:(s + 1) * LANES],
                                                        preferred_element_type=f32)
            d = lax.dot_general(vg, kd[c][:, kl], tn_dims, preferred_element_type=f32)
            d_state[c][g] = jnp.where(own_block, d, 0.0) if sub_heads > 1 else d

    for g in range(n_groups):
        kl = slice(g * LANES, (g + 1) * LANES)
        st = s_sc[g]
        for c, r in enumerate(rows):
            o_inter = lax.dot_general(qb[c][:, kl], st.astype(bf16), nt_dims, preferred_element_type=f32)
            for s in range(sub_heads):
                head = g * sub_heads + s
                o_sc[r, head * LANES:(head + 1) * LANES] = o_intra[c][head] + o_inter[:, s * LANES:(s + 1) * LANES]
            st = st * decay[c][:, kl] + d_state[c][g]
        s_sc[g] = st


def _gate_and_norm(o_sc, gain_ref, r_ref, out_ref, n_heads):
    for h in range(n_heads):
        sl = slice(h * LANES, (h + 1) * LANES)
        out_ref[:, sl] = (_rms(o_sc[:, sl], gain_ref[...]) * _silu(r_ref[:, sl].astype(f32))).astype(bf16)


def _pad_row_mask(shape):
    row = pl.program_id(1) * ROW_TILE + lax.broadcasted_iota(jnp.int32, shape, 0)
    return row >= PAD


def _gla_kernel(q_ref, k_ref, v_ref, r_ref, gd_ref, wg2_ref, bg_ref, gain_ref, tril_ref, out_ref,
                q_sc, k_sc, b_sc, s_sc, o_sc):
    @pl.when(pl.program_id(1) == 0)
    def _():
        s_sc[...] = jnp.zeros(s_sc.shape, f32)

    x = jnp.dot(gd_ref[...], wg2_ref[...], preferred_element_type=f32) + bg_ref[...]
    log_g = (jnp.minimum(x, 0.0) - jnp.log(1.0 + jnp.exp(-jnp.abs(x)))) * (1.0 / GLA_GATE_NORM)
    b_sc[...] = jnp.where(_pad_row_mask(log_g.shape), log_g, 0.0)
    q_sc[...] = q_ref[...].astype(f32) * (GLA_DK ** -0.5)
    k_sc[...] = k_ref[...].astype(f32)
    _gated_scan(LANES // GLA_DK, q_sc, k_sc, b_sc, v_ref, tril_ref, s_sc, o_sc)
    _gate_and_norm(o_sc, gain_ref, r_ref, out_ref, GLA_HEADS)


def _hgrn_kernel(q_ref, f_ref, v_ref, r_ref, lb_ref, gain_ref, tril_ref, out_ref,
                 q_sc, k_sc, b_sc, s_sc, o_sc):
    @pl.when(pl.program_id(1) == 0)
    def _():
        s_sc[...] = jnp.zeros(s_sc.shape, f32)

    lb = lb_ref[...]
    f = lb + (1.0 - lb) * _sigmoid(f_ref[...].astype(f32))
    b_sc[...] = jnp.where(_pad_row_mask(f.shape), jnp.log(f), 0.0)
    q_sc[...] = _silu(q_ref[...].astype(f32))
    k_sc[...] = 1.0 - f
    _gated_scan(1, q_sc, k_sc, b_sc, v_ref, tril_ref, s_sc, o_sc)
    _gate_and_norm(o_sc, gain_ref, r_ref, out_ref, HGRN_HEADS)


def _seq_col_spec(width, col, nt):
    return pl.BlockSpec((ROW_TILE, width), lambda b, i: (b * nt + i, col // width))


def _gla(proj, wg2, bg, gain, tril, batch, lp):
    nt = lp // ROW_TILE
    return pl.pallas_call(
        _gla_kernel,
        out_shape=jax.ShapeDtypeStruct((batch * lp, GLA_VW), bf16),
        grid=(batch, nt),
        in_specs=[
            _seq_col_spec(GLA_KW, COL_GLA_Q, nt), _seq_col_spec(GLA_KW, COL_GLA_K, nt),
            _seq_col_spec(GLA_VW, COL_GLA_V, nt), _seq_col_spec(GLA_VW, COL_GLA_R, nt),
            _seq_col_spec(LANES, COL_GLA_GD, nt),
            _resident((LANES, GLA_KW)), _resident((1, GLA_KW)), _resident((1, GLA_DV)),
            _resident((CHUNK, CHUNK)),
        ],
        out_specs=pl.BlockSpec((ROW_TILE, GLA_VW), lambda b, i: (b * nt + i, 0)),
        scratch_shapes=[
            pltpu.VMEM((ROW_TILE, GLA_KW), f32), pltpu.VMEM((ROW_TILE, GLA_KW), f32),
            pltpu.VMEM((ROW_TILE, GLA_KW), f32),
            pltpu.VMEM((GLA_KW // LANES, 2 * LANES, LANES), f32),
            pltpu.VMEM((ROW_TILE, GLA_VW), f32),
        ],
        compiler_params=_cparams(("parallel", "arbitrary")),
        name="gla",
    )(proj, proj, proj, proj, proj, wg2, bg, gain, tril)


def _hgrn(proj, lb, gain, tril, batch, lp):
    nt = lp // ROW_TILE
    return pl.pallas_call(
        _hgrn_kernel,
        out_shape=jax.ShapeDtypeStruct((batch * lp, HGRN_VW), bf16),
        grid=(batch, nt),
        in_specs=[
            _seq_col_spec(HGRN_KW, COL_HG_Q, nt), _seq_col_spec(HGRN_KW, COL_HG_F, nt),
            _seq_col_spec(HGRN_VW, COL_HG_I, nt), _seq_col_spec(HGRN_VW, COL_HG_G, nt),
            _resident((1, HGRN_KW)), _resident((1, LANES)), _resident((CHUNK, CHUNK)),
        ],
        out_specs=pl.BlockSpec((ROW_TILE, HGRN_VW), lambda b, i: (b * nt + i, 0)),
        scratch_shapes=[
            pltpu.VMEM((ROW_TILE, HGRN_KW), f32), pltpu.VMEM((ROW_TILE, HGRN_KW), f32),
            pltpu.VMEM((ROW_TILE, HGRN_KW), f32),
            pltpu.VMEM((HGRN_KW // LANES, LANES, LANES), f32),
            pltpu.VMEM((ROW_TILE, HGRN_VW), f32),
        ],
        compiler_params=_cparams(("parallel", "arbitrary")),
        name="hgrn2",
    )(proj, proj, proj, proj, lb, gain, tril)


def _mix_kernel(h_ref, om_ref, og_ref, oh_ref, g0_ref, g1_ref, g2_ref, wm_ref, wg_ref, wh_ref, wo_ref, out_ref):
    om = jnp.concatenate([om_ref[0, j] for j in range(MLA_HEADS // 2)], axis=-1)
    mixed = (_sigmoid(g0_ref[...].astype(f32)) * jnp.dot(om, wm_ref[...], preferred_element_type=f32)
             + _sigmoid(g1_ref[...].astype(f32)) * jnp.dot(og_ref[...], wg_ref[...], preferred_element_type=f32)
             + _sigmoid(g2_ref[...].astype(f32)) * jnp.dot(oh_ref[...], wh_ref[...], preferred_element_type=f32))
    h2 = h_ref[...] + jnp.dot(mixed.astype(bf16), wo_ref[...], preferred_element_type=f32)
    out_ref[...] = jnp.where(_pad_row_mask(h2.shape), h2, 0.0)


def _mix(h, o_mla, o_gla, o_hgrn, proj, wm, wg, wh, wo, batch, lp):
    nt = lp // ROW_TILE
    row = lambda b, i: (b * nt + i, 0)
    return pl.pallas_call(
        _mix_kernel,
        out_shape=jax.ShapeDtypeStruct((batch * lp, D_MODEL), f32),
        grid=(batch, nt),
        in_specs=[
            pl.BlockSpec((ROW_TILE, D_MODEL), row),
            pl.BlockSpec((1, MLA_HEADS // 2, ROW_TILE, HEAD_LANES), lambda b, i: (b, 0, i, 0)),
            pl.BlockSpec((ROW_TILE, GLA_VW), row), pl.BlockSpec((ROW_TILE, HGRN_VW), row),
            _seq_col_spec(D_MODEL, COL_GATES, nt), _seq_col_spec(D_MODEL, COL_GATES + D_MODEL, nt),
            _seq_col_spec(D_MODEL, COL_GATES + 2 * D_MODEL, nt),
            _resident((MLA_HEADS * MLA_V, D_MODEL)), _resident((GLA_VW, D_MODEL)),
            _resident((HGRN_VW, D_MODEL)), _resident((D_MODEL, D_MODEL)),
        ],
        out_specs=pl.BlockSpec((ROW_TILE, D_MODEL), row),
        compiler_params=_cparams(("parallel", "arbitrary")),
        name="mix_out",
    )(h, o_mla, o_gla, o_hgrn, proj, proj, proj, wm, wg, wh, wo)


def _ffn_kernel(h_ref, g_ref, wg_ref, wu_ref, wd_ref, out_ref):
    h = h_ref[...]
    u = _rms(h, g_ref[...]).astype(bf16)
    acc = h
    for c in range(D_FF // FF_CHUNK):
        sl = slice(c * FF_CHUNK, (c + 1) * FF_CHUNK)
        gate = jnp.dot(u, wg_ref[:, sl], preferred_element_type=f32)
        up = jnp.dot(u, wu_ref[:, sl], preferred_element_type=f32)
        acc = acc + jnp.dot((_silu(gate) * up).astype(bf16), wd_ref[sl, :], preferred_element_type=f32)
    out_ref[...] = acc


def _ffn(h, gain, wg, wu, wd):
    tp = h.shape[0]
    row = lambda i: (i, 0)
    return pl.pallas_call(
        _ffn_kernel,
        out_shape=jax.ShapeDtypeStruct((tp, D_MODEL), f32),
        grid=(tp // ROW_TILE,),
        in_specs=[
            pl.BlockSpec((ROW_TILE, D_MODEL), row), _resident((1, D_MODEL)),
            _resident((D_MODEL, D_FF)), _resident((D_MODEL, D_FF)), _resident((D_FF, D_MODEL)),
        ],
        out_specs=pl.BlockSpec((ROW_TILE, D_MODEL), row),
        compiler_params=_cparams(("parallel",)),
        name="dense_ffn",
    )(h, gain, wg, wu, wd)


def _split3(x):
    hi = x.astype(bf16)
    lo = (x - hi.astype(f32)).astype(bf16)
    return hi, lo


def _router_kernel(h_ref, g_ref, whi_ref, wlo_ref, b_ref, out_ref):
    u = _rms(h_ref[...], g_ref[...])
    u_hi, u_lo = _split3(u)
    logits = (jnp.dot(u_hi, whi_ref[...], preferred_element_type=f32)
              + jnp.dot(u_lo, whi_ref[...], preferred_element_type=f32)
              + jnp.dot(u_hi, wlo_ref[...], preferred_element_type=f32)) + b_ref[...]
    lane = lax.broadcasted_iota(jnp.int32, logits.shape, 1)
    logits = jnp.where(lane < N_EXPERTS, logits, -jnp.inf)
    v1 = jnp.max(logits, axis=-1, keepdims=True)
    i1 = jnp.min(jnp.where(logits == v1, lane, LANES), axis=-1, keepdims=True)
    rest = jnp.where(lane == i1, -jnp.inf, logits)
    v2 = jnp.max(rest, axis=-1, keepdims=True)
    i2 = jnp.min(jnp.where(rest == v2, lane, LANES), axis=-1, keepdims=True)
    e = jnp.exp(v2 - v1)
    g1 = 1.0 / (1.0 + e)
    g2 = e * g1
    out = jnp.where(lane == 0, g1, jnp.where(lane == 1, g2, 0.0))
    out = jnp.where(lane == 2, i1.astype(f32), jnp.where(lane == 3, i2.astype(f32), out))
    out_ref[...] = out


def _router(h, gain, w_hi, w_lo, b):
    tp = h.shape[0]
    return pl.pallas_call(
        _router_kernel,
        out_shape=jax.ShapeDtypeStruct((tp, LANES), f32),
        grid=(tp // ROW_TILE,),
        in_specs=[
            pl.BlockSpec((ROW_TILE, D_MODEL), lambda i: (i, 0)), _resident((1, D_MODEL)),
            _resident((D_MODEL, LANES)), _resident((D_MODEL, LANES)), _resident((1, LANES)),
        ],
        out_specs=pl.BlockSpec((ROW_TILE, LANES), lambda i: (i, 0)),
        compiler_params=_cparams(("parallel",)),
        name="moe_router",
    )(h, gain, w_hi, w_lo, b)


def _token_copy(src_tok, token, dst_tok, r, sem):
    return pltpu.make_async_copy(src_tok.at[token], dst_tok.at[pl.ds(r * SUBLANES, SUBLANES), :], sem)


def _gather_tokens(idx_ref, src_tok, dst_tok, sem):
    def body(r, carry):
        _token_copy(src_tok, idx_ref[0, 0, r], dst_tok, r, sem).start()
        return carry
    lax.fori_loop(0, ROW_TILE, body, 0, unroll=GATHER_UNROLL)


def _wait_tokens(dst_tok, sem):
    pltpu.make_async_copy(dst_tok, dst_tok, sem).wait()


def _tokens_to_rows(buf_tok):
    return jnp.concatenate([buf_tok[pl.ds(c, ROW_TILE, stride=SUBLANES), :] for c in range(D_MODEL // LANES)],
                           axis=1)


def _rows_to_tokens(rows, dst_tok):
    for c in range(D_MODEL // LANES):
        dst_tok[pl.ds(c, ROW_TILE, stride=SUBLANES), :] = rows[:, c * LANES:(c + 1) * LANES]


def _expert_kernel(be_ref, nu_ref, tok_ref, tok_next_ref, w_ref, g_ref, h_tok, wg_ref, wu_ref, wd_ref, y_ref,
                   xbuf, sems, x_sc, acc_sc):
    i = pl.program_id(0)
    k = pl.program_id(1)
    nk = pl.num_programs(1)
    n_used = nu_ref[0]
    slot = i % 2
    rows_per_step = ROW_TILE // (D_EXPERT // EXP_CHUNK)

    @pl.when((k == 0) & (i <= n_used))
    def _():
        @pl.when(i == 0)
        def _():
            _gather_tokens(tok_ref, h_tok, xbuf.at[0], sems.at[0])

        _wait_tokens(xbuf.at[slot], sems.at[slot])
        x_sc[...] = _rms(_tokens_to_rows(xbuf.at[slot]), g_ref[...]).astype(bf16)
        acc_sc[...] = jnp.zeros(acc_sc.shape, f32)

    @pl.when(i < n_used)
    def _():
        base = k * rows_per_step
        for r in range(rows_per_step):
            _token_copy(h_tok, tok_next_ref[0, 0, base + r], xbuf.at[1 - slot], base + r, sems.at[1 - slot]).start()
        x = x_sc[...]
        gate = jnp.dot(x, wg_ref[...], preferred_element_type=f32)
        up = jnp.dot(x, wu_ref[...], preferred_element_type=f32)
        acc_sc[...] += jnp.dot((_silu(gate) * up).astype(bf16), wd_ref[...], preferred_element_type=f32)

    @pl.when(k == nk - 1)
    def _():
        @pl.when(i < n_used)
        def _():
            _rows_to_tokens(acc_sc[...] * w_ref[...], y_ref)

        @pl.when(i >= n_used)
        def _():
            y_ref[...] = jnp.zeros(y_ref.shape, f32)


def _experts(block_expert, n_used, slot_row, slot_w, gain, h, wg, wu, wd):
    n_blocks = slot_row.shape[0]
    nk = D_EXPERT // EXP_CHUNK

    def live(i, nu):
        return jnp.minimum(i, nu[0] - 1)

    def kk(i, k, nu):
        return jnp.where(i < nu[0], k, nk - 1)

    grid_spec = pltpu.PrefetchScalarGridSpec(
        num_scalar_prefetch=2,
        grid=(n_blocks, nk),
        in_specs=[
            pl.BlockSpec((1, 1, ROW_TILE), lambda i, k, be, nu: (i, 0, 0), memory_space=pltpu.SMEM),
            pl.BlockSpec((1, 1, ROW_TILE), lambda i, k, be, nu: (jnp.minimum(i + 1, n_blocks - 1), 0, 0),
                         memory_space=pltpu.SMEM),
            pl.BlockSpec((ROW_TILE, 1), lambda i, k, be, nu: (i, 0)),
            pl.BlockSpec((1, D_MODEL), lambda i, k, be, nu: (0, 0)),
            pl.BlockSpec(memory_space=pl.ANY),
            pl.BlockSpec((None, D_MODEL, EXP_CHUNK), lambda i, k, be, nu: (be[live(i, nu)], 0, kk(i, k, nu))),
            pl.BlockSpec((None, D_MODEL, EXP_CHUNK), lambda i, k, be, nu: (be[live(i, nu)], 0, kk(i, k, nu))),
            pl.BlockSpec((None, EXP_CHUNK, D_MODEL), lambda i, k, be, nu: (be[live(i, nu)], kk(i, k, nu), 0)),
        ],
        out_specs=pl.BlockSpec((ROW_TILE * SUBLANES, LANES), lambda i, k, be, nu: (i, 0)),
        scratch_shapes=[
            pltpu.VMEM((2, ROW_TILE * SUBLANES, LANES), f32),
            pltpu.SemaphoreType.DMA((2,)),
            pltpu.VMEM((ROW_TILE, D_MODEL), bf16),
            pltpu.VMEM((ROW_TILE, D_MODEL), f32),
        ],
    )
    h_tok = h.reshape(h.shape[0], SUBLANES, LANES)
    y_tok = pl.pallas_call(
        _expert_kernel,
        out_shape=jax.ShapeDtypeStruct((n_blocks * ROW_TILE * SUBLANES, LANES), f32),
        grid_spec=grid_spec,
        compiler_params=_cparams(("arbitrary", "arbitrary")),
        name="moe_experts",
    )(block_expert, n_used, slot_row, slot_row, slot_w, gain, h_tok, wg, wu, wd)
    return y_tok.reshape(n_blocks * ROW_TILE, SUBLANES, LANES)


def _combine_kernel(tiles_per_batch, s0_ref, s1_ref, s0_next_ref, s1_next_ref, h_ref, y_tok, out_ref, buf, sems):
    i = pl.program_id(0)
    slot = i % 2

    @pl.when(i == 0)
    def _():
        _gather_tokens(s0_ref, y_tok, buf.at[0, 0], sems.at[0, 0])
        _gather_tokens(s1_ref, y_tok, buf.at[0, 1], sems.at[0, 1])

    @pl.when(i + 1 < pl.num_programs(0))
    def _():
        _gather_tokens(s0_next_ref, y_tok, buf.at[1 - slot, 0], sems.at[1 - slot, 0])
        _gather_tokens(s1_next_ref, y_tok, buf.at[1 - slot, 1], sems.at[1 - slot, 1])

    _wait_tokens(buf.at[slot, 0], sems.at[slot, 0])
    _wait_tokens(buf.at[slot, 1], sems.at[slot, 1])
    out = h_ref[...] + _tokens_to_rows(buf.at[slot, 0]) + _tokens_to_rows(buf.at[slot, 1])
    row = (i % tiles_per_batch) * ROW_TILE + lax.broadcasted_iota(jnp.int32, out.shape, 0)
    out_ref[...] = jnp.where(row >= PAD, out, 0.0)


def _combine(slot0, slot1, h, y_tok, batch, lp):
    nt = lp // ROW_TILE
    n_tiles = batch * nt
    idx_spec = pl.BlockSpec((1, 1, ROW_TILE), lambda i: (i, 0, 0), memory_space=pltpu.SMEM)
    nxt_spec = pl.BlockSpec((1, 1, ROW_TILE), lambda i: (jnp.minimum(i + 1, n_tiles - 1), 0, 0),
                            memory_space=pltpu.SMEM)
    row = lambda i: (i, 0)
    return pl.pallas_call(
        functools.partial(_combine_kernel, nt),
        out_shape=jax.ShapeDtypeStruct((batch * lp, D_MODEL), f32),
        grid=(n_tiles,),
        in_specs=[idx_spec, idx_spec, nxt_spec, nxt_spec, pl.BlockSpec((ROW_TILE, D_MODEL), row),
                  pl.BlockSpec(memory_space=pl.ANY)],
        out_specs=pl.BlockSpec((ROW_TILE, D_MODEL), row),
        scratch_shapes=[pltpu.VMEM((2, 2, ROW_TILE * SUBLANES, LANES), f32), pltpu.SemaphoreType.DMA((2, 2))],
        compiler_params=_cparams(("arbitrary",)),
        name="moe_combine",
    )(slot0, slot1, slot0, slot1, h, y_tok)


def _moe(h, gain, w_router, b_router, wg, wu, wd, batch, lp):
    seq = lp - PAD
    n_tok = batch * seq
    n_assign = n_tok * TOP_K
    w_r = jnp.pad(w_router, ((0, 0), (0, LANES - N_EXPERTS)))
    w_hi = w_r.astype(bf16)
    w_lo = (w_r - w_hi.astype(f32)).astype(bf16)
    b_r = jnp.pad(b_router, (0, LANES - N_EXPERTS)).reshape(1, LANES)
    routed = _router(h, gain, w_hi, w_lo, b_r).reshape(batch, lp, LANES)[:, PAD:, :4].reshape(n_tok, 4)
    gate = routed[:, :TOP_K]
    top_idx = routed[:, TOP_K:].astype(jnp.int32)

    e_flat = top_idx.reshape(n_assign)
    experts = jnp.arange(N_EXPERTS, dtype=jnp.int32)
    order = jnp.argsort(e_flat)
    onehot = (e_flat[:, None] == experts[None, :]).astype(jnp.int32)
    running = jnp.cumsum(onehot, axis=0)
    counts = running[-1]
    start = jnp.cumsum(counts) - counts
    padded = (counts + ROW_TILE - 1) // ROW_TILE * ROW_TILE
    pend = jnp.cumsum(padded)
    pstart = pend - padded
    n_blocks = -(-n_assign // ROW_TILE) + N_EXPERTS
    n_slots = n_blocks * ROW_TILE

    slot = jnp.arange(n_slots, dtype=jnp.int32)
    slot_e = jnp.minimum(jnp.sum((slot[:, None] >= pend[None, :]).astype(jnp.int32), axis=1), N_EXPERTS - 1)
    within = slot - pstart[slot_e]
    live = (within < counts[slot_e]) & (slot < pend[-1])
    assign = order[jnp.clip(start[slot_e] + within, 0, n_assign - 1)]
    tok = assign // TOP_K
    slot_row = jnp.where(live, (tok // seq) * lp + PAD + tok % seq, 0)
    slot_w = jnp.where(live, gate.reshape(n_assign)[assign], 0.0)
    block_expert = slot_e[::ROW_TILE]
    n_used = (pend[-1] // ROW_TILE).astype(jnp.int32).reshape(1)

    y = _experts(block_expert, n_used, slot_row.reshape(n_blocks, 1, ROW_TILE), slot_w.reshape(n_slots, 1),
                 gain, h, wg, wu, wd)

    rank = jnp.sum(running * onehot, axis=1) - 1
    slot_of = (pstart[e_flat] + rank).reshape(batch, seq, TOP_K)
    slot_of = jnp.pad(slot_of, ((0, 0), (PAD, 0), (0, 0))).reshape(batch * lp, TOP_K)
    nt_all = batch * lp // ROW_TILE
    return _combine(slot_of[:, 0].reshape(nt_all, 1, ROW_TILE), slot_of[:, 1].reshape(nt_all, 1, ROW_TILE), h, y,
                    batch, lp)


def _pad_cols(w, width):
    return jnp.pad(w, ((0, 0), (0, width - w.shape[1])))


def _in_proj_weight(w):
    offs = np.cumsum((MLA_Q_LORA, MLA_KV_LORA, MLA_ROPE, GLA_KW, GLA_KW, GLA_VW, GLA_GATE_RANK, GLA_VW,
                      HGRN_KW, HGRN_KW, HGRN_VW, HGRN_VW, N_BRANCH * D_MODEL))[:-1]
    (dq, dkv, kr, gq, gk, gv, ggd, gr, hq, hf, hi, hg, gates) = jnp.split(w, offs, axis=1)
    kr_p = jnp.pad(kr, ((0, 0), (MLA_NOPE, HEAD_LANES - MLA_NOPE - MLA_ROPE)))
    parts = [dq, dkv, kr_p, gq, gk, gv, gr, hq, hf, hi, hg, gates, _pad_cols(ggd, LANES), _rotate_half_cols(kr_p)]
    return jnp.concatenate(parts, axis=1).astype(bf16)


def _rotate_half_cols(w, sign=-1.0):
    lead = w.shape[0]
    w3 = w.reshape(lead, -1, HEAD_LANES)
    half = MLA_ROPE // 2
    first, second = w3[:, :, MLA_NOPE:MLA_NOPE + half], w3[:, :, MLA_NOPE + half:MLA_QK]
    zero = lambda n: jnp.zeros(w3.shape[:2] + (n,), w.dtype)
    return jnp.concatenate([zero(MLA_NOPE), sign * second, first, zero(HEAD_LANES - MLA_QK)],
                           axis=-1).reshape(lead, -1)


def _mla_weights(w_uq, w_ukv):
    d_in = w_uq.shape[0]
    wq = w_uq.reshape(d_in, MLA_HEADS, MLA_QK)
    wq = jnp.pad(wq, ((0, 0), (0, 0), (0, HEAD_LANES - MLA_QK))).reshape(d_in, MLA_HEADS * HEAD_LANES)
    d_kv = w_ukv.shape[0]
    wkv = w_ukv.reshape(d_kv, MLA_HEADS, MLA_NOPE + MLA_V)
    wk = jnp.pad(wkv[:, :, :MLA_NOPE], ((0, 0), (0, 0), (0, HEAD_LANES - MLA_NOPE)))
    wv = wkv[:, :, MLA_NOPE:]
    zeros = jnp.zeros_like(wv)
    even = (jnp.arange(MLA_HEADS) % 2 == 0)[None, :, None]
    wv = jnp.concatenate([jnp.where(even, wv, zeros), jnp.where(even, zeros, wv)], axis=-1)
    return (wq.astype(bf16), wk.reshape(d_kv, -1).astype(bf16), wv.reshape(d_kv, -1).astype(bf16))


def _head_gain(g, scale):
    return (jnp.pad(g, (0, HEAD_LANES - MLA_QK)) * scale).reshape(1, HEAD_LANES).astype(f32)


def _rope_tables(lp):
    half = MLA_ROPE // 2
    pos = np.maximum(np.arange(lp) - PAD, 0).astype(np.float32)
    inv = (ROPE_THETA ** (-np.arange(half, dtype=np.float32) / half)).astype(np.float32)
    ang = jnp.asarray(pos)[:, None] * jnp.asarray(inv)[None, :]
    cos, sin = jnp.cos(ang), jnp.sin(ang)
    ones = jnp.ones((lp, MLA_NOPE), f32)
    z = lambda n: jnp.zeros((lp, n), f32)
    tail = HEAD_LANES - MLA_QK
    c = jnp.concatenate([ones, cos, cos, jnp.ones((lp, tail), f32)], axis=1)
    s = jnp.concatenate([z(MLA_NOPE), sin, sin, z(tail)], axis=1)
    return c, s


def _head_indicator():
    lane_head = np.arange(2 * HEAD_LANES) // HEAD_LANES
    return jnp.asarray(lane_head[:, None] == lane_head[None, :], dtype=bf16)


def _chunk_tril():
    r = np.arange(CHUNK)
    return jnp.asarray(r[:, None] >= r[None, :], dtype=bf16)


def kernel(x, meta_tokens, attn_norm, w_in, mla_cq_norm, w_mla_uq, mla_ckv_norm, w_mla_ukv, mla_q_norm, mla_k_norm, w_gla_g2, b_gla_g, gla_o_norm, hgrn_lower_bounds, hgrn_o_norm, w_mla_o, w_gla_o, w_hgrn_o, w_out, ffn_norm, w_ff_gate, w_ff_up, w_ff_down, w_router, b_router, w_exp_gate, w_exp_up, w_exp_down):
    batch, seq, _ = x.shape
    depth = w_in.shape[0]
    lp = PAD + N_META + seq
    assert lp % ROW_TILE == 0 and ROW_TILE % CHUNK == 0 and KV_TILE == ROW_TILE

    meta = jnp.broadcast_to(meta_tokens[None].astype(x.dtype), (batch, N_META, D_MODEL))
    h = jnp.concatenate([jnp.zeros((batch, PAD, D_MODEL), x.dtype), meta, x], axis=1).reshape(batch * lp, D_MODEL)

    lbs = jax.nn.softmax(hgrn_lower_bounds.astype(f32), axis=0)
    lbs = jnp.cumsum(lbs, axis=0) - lbs[0]
    rope_c, rope_s = _rope_tables(lp)
    head_ind = _head_indicator()
    tril = _chunk_tril()
    row1 = lambda v: v.reshape(1, -1).astype(f32)

    for layer in range(depth):
        proj = _in_proj(h, row1(attn_norm[layer]), _in_proj_weight(w_in[layer]))
        wq, wk, wv = _mla_weights(w_mla_uq[layer], w_mla_ukv[layer])
        q_gain = _head_gain(mla_q_norm[layer], MLA_QK ** -0.5 * LOG2_E)
        k_gain = _head_gain(mla_k_norm[layer], 1.0)
        q, k, v = _mla_proj(proj, row1(mla_cq_norm[layer]), row1(mla_ckv_norm[layer]),
                            wq, _rotate_half_cols(wq), wk, wv, head_ind,
                            q_gain, _rotate_half_cols(q_gain, 1.0), k_gain, _rotate_half_cols(k_gain, 1.0),
                            rope_c, rope_s, batch, lp)
        o_mla = _attention(q, k, v)
        wg2 = jnp.pad(w_gla_g2[layer], ((0, LANES - GLA_GATE_RANK), (0, 0))).astype(bf16)
        o_gla = _gla(proj, wg2, row1(b_gla_g[layer]), row1(gla_o_norm[layer]), tril, batch, lp)
        o_hgrn = _hgrn(proj, row1(lbs[layer]), row1(hgrn_o_norm[layer]), tril, batch, lp)
        h = _mix(h, o_mla, o_gla, o_hgrn, proj, w_mla_o[layer].astype(bf16), w_gla_o[layer].astype(bf16),
                 w_hgrn_o[layer].astype(bf16), w_out[layer].astype(bf16), batch, lp)
        i = layer // 2
        if layer % 2 == 0:
            h = _ffn(h, row1(ffn_norm[layer]), w_ff_gate[i].astype(bf16), w_ff_up[i].astype(bf16),
                     w_ff_down[i].astype(bf16))
        else:
            h = _moe(h, row1(ffn_norm[layer]), w_router[i], b_router[i], w_exp_gate[i].astype(bf16),
                     w_exp_up[i].astype(bf16), w_exp_down[i].astype(bf16), batch, lp)
    return h.reshape(batch, lp, D_MODEL)[:, PAD + N_META:]
```

```python
import functools

import numpy as np
import jax
import jax.numpy as jnp
from jax import lax
from jax.experimental import pallas as pl
from jax.experimental.pallas import tpu as pltpu

f32 = jnp.float32
bf16 = jnp.bfloat16

D_MODEL = 1024
N_META = 16
PAD = 112
CHUNK = 64
EPS = 1e-6
NEG_INF = -1e30
LOG2_E = 1.4426950408889634
MLA_HEADS = 8
MLA_Q_LORA = 256
MLA_KV_LORA = 128
MLA_NOPE = 64
MLA_ROPE = 32
MLA_V = 64
MLA_QK = MLA_NOPE + MLA_ROPE
ROPE_THETA = 10000.0
GLA_HEADS = 4
GLA_DK = 64
GLA_DV = 128
GLA_KW = GLA_HEADS * GLA_DK
GLA_VW = GLA_HEADS * GLA_DV
GLA_GATE_RANK = 16
GLA_GATE_NORM = 16.0
HGRN_HEADS = 4
HGRN_DK = 128
HGRN_KW = 512
HGRN_VW = 512
N_BRANCH = 3
D_FF = 2816
N_EXPERTS = 8
TOP_K = 2
D_EXPERT = 3584

LANES = 128
SUBLANES = 8
HEAD_LANES = 128
VMEM_LIMIT = 56 * 1024 * 1024

ROW_TILE = 640
KV_TILE = 640
ATTN_GROUP = 2
FF_CHUNKS = (1280, 1536)
EXP_CHUNK = 1792
GATHER_UNROLL = 8
OUT_TILE = 512

COL_MLA = 0
COL_GLA_Q = 512
COL_GLA_K = 768
COL_GLA_V = 1024
COL_GLA_R = 1536
COL_HG_Q = 2048
COL_HG_F = 2560
COL_HG_I = 3072
COL_HG_G = 3584
COL_GATES = 4096
COL_GLA_GD = 7168
COL_KR_ROT = 7296
IN_COLS_P = 7424
IN_COL_TILE = 3712


def _cparams(sem):
    return pltpu.CompilerParams(dimension_semantics=sem, vmem_limit_bytes=VMEM_LIMIT)


def _resident(shape):
    nd = len(shape)
    return pl.BlockSpec(shape, lambda *_: (0,) * nd, pipeline_mode=pl.Buffered(1))


def _silu(x):
    return x * (1.0 / (1.0 + jnp.exp(-x)))


def _sigmoid(x):
    return 1.0 / (1.0 + jnp.exp(-x))


def _rms(x, gain):
    ms = jnp.mean(x * x, axis=-1, keepdims=True)
    return x * lax.rsqrt(ms + EPS) * gain


def _in_proj_kernel(h_ref, g_ref, w_ref, o_ref, u_sc):
    @pl.when(pl.program_id(1) == 0)
    def _():
        u_sc[...] = _rms(h_ref[...], g_ref[...]).astype(bf16)

    o_ref[...] = jnp.dot(u_sc[...], w_ref[...], preferred_element_type=f32).astype(bf16)


def _in_proj(h, gain, w):
    tp = h.shape[0]
    return pl.pallas_call(
        _in_proj_kernel,
        out_shape=jax.ShapeDtypeStruct((tp, IN_COLS_P), bf16),
        grid=(tp // ROW_TILE, IN_COLS_P // IN_COL_TILE),
        in_specs=[
            pl.BlockSpec((ROW_TILE, D_MODEL), lambda i, j: (i, 0)),
            pl.BlockSpec((1, D_MODEL), lambda i, j: (0, 0)),
            pl.BlockSpec((D_MODEL, IN_COL_TILE), lambda i, j: (0, j)),
        ],
        out_specs=pl.BlockSpec((ROW_TILE, IN_COL_TILE), lambda i, j: (i, j)),
        scratch_shapes=[pltpu.VMEM((ROW_TILE, D_MODEL), bf16)],
        compiler_params=_cparams(("parallel", "arbitrary")),
        name="in_proj",
    )(h, gain, w)


def _head_mean_square(x, ind_ref):
    sq = (x * x).astype(bf16)
    w = ind_ref.shape[0]
    sums = [jnp.dot(sq[:, g * w:(g + 1) * w], ind_ref[...], preferred_element_type=f32)
            for g in range(x.shape[1] // w)]
    return jnp.concatenate(sums, axis=1) * (1.0 / MLA_QK)


def _mla_proj_kernel(p_ref, krr_ref, cqn_ref, ckvn_ref, wuq_ref, wuqr_ref, wk_ref, wv_ref, ind_ref,
                     qg_ref, qgr_ref, kg_ref, kgr_ref, c_ref, s_ref, q_ref, k_ref, v_ref):
    p = p_ref[...].astype(f32)
    cq = _rms(p[:, :MLA_Q_LORA], cqn_ref[...]).astype(bf16)
    ckv = _rms(p[:, MLA_Q_LORA:MLA_Q_LORA + MLA_KV_LORA], ckvn_ref[...]).astype(bf16)
    k_rope = p[:, MLA_Q_LORA + MLA_KV_LORA:]
    q_all = jnp.dot(cq, wuq_ref[...], preferred_element_type=f32)
    q_rot = jnp.dot(cq, wuqr_ref[...], preferred_element_type=f32)
    k_all = jnp.dot(ckv, wk_ref[...], preferred_element_type=f32) + jnp.tile(k_rope, (1, MLA_HEADS))
    v_all = jnp.dot(ckv, wv_ref[...], preferred_element_type=f32)
    q_scale = lax.rsqrt(_head_mean_square(q_all, ind_ref) + EPS)
    k_scale = lax.rsqrt(_head_mean_square(k_all, ind_ref) + EPS)
    c, s = c_ref[...], s_ref[...]
    q_c, q_s = qg_ref[...] * c, qgr_ref[...] * s
    k_c, k_s = kg_ref[...] * c, kgr_ref[...] * s
    k_rot = krr_ref[...].astype(f32) * k_s
    lane = lax.broadcasted_iota(jnp.int32, c.shape, 1)
    for h in range(MLA_HEADS):
        sl = slice(h * HEAD_LANES, (h + 1) * HEAD_LANES)
        q_ref[0, h] = (q_scale[:, sl] * (q_all[:, sl] * q_c + q_rot[:, sl] * q_s)).astype(bf16)
        k_ref[0, h] = (k_scale[:, sl] * (k_all[:, sl] * k_c + k_rot)).astype(bf16)
        one_lane = LANES - 1 if h % 2 == 0 else 0
        v_ref[0, h] = jnp.where(lane == one_lane, 1.0, v_all[:, sl]).astype(bf16)


def _mla_proj(proj, cqn, ckvn, wuq, wuqr, wk, wv, ind, qg, qgr, kg, kgr, rope_c, rope_s, batch, lp):
    nt = lp // ROW_TILE
    hl = MLA_HEADS * HEAD_LANES
    head_spec = pl.BlockSpec((1, MLA_HEADS, ROW_TILE, HEAD_LANES), lambda i: (i // nt, 0, i % nt, 0))
    tab_spec = pl.BlockSpec((ROW_TILE, HEAD_LANES), lambda i: (i % nt, 0))
    gain_spec = _resident((1, HEAD_LANES))
    shp = jax.ShapeDtypeStruct((batch, MLA_HEADS, lp, HEAD_LANES), bf16)
    return pl.pallas_call(
        _mla_proj_kernel,
        out_shape=(shp, shp, shp),
        grid=(batch * nt,),
        in_specs=[
            pl.BlockSpec((ROW_TILE, 512), lambda i: (i, COL_MLA // 512)),
            pl.BlockSpec((ROW_TILE, LANES), lambda i: (i, COL_KR_ROT // LANES)),
            _resident((1, MLA_Q_LORA)), _resident((1, MLA_KV_LORA)),
            _resident((MLA_Q_LORA, hl)), _resident((MLA_Q_LORA, hl)),
            _resident((MLA_KV_LORA, hl)), _resident((MLA_KV_LORA, hl)),
            _resident((2 * HEAD_LANES, 2 * HEAD_LANES)),
            gain_spec, gain_spec, gain_spec, gain_spec,
            tab_spec, tab_spec,
        ],
        out_specs=(head_spec, head_spec, head_spec),
        compiler_params=_cparams(("parallel",)),
        name="mla_proj",
    )(proj, proj, cqn, ckvn, wuq, wuqr, wk, wv, ind, qg, qgr, kg, kgr, rope_c, rope_s)


MASK_PAD, MASK_PAD_CAUSAL, MASK_CAUSAL = 0, 1, 2


def _attn_bias():
    r = np.arange(ROW_TILE)[:, None]
    c = np.arange(KV_TILE)[None, :]
    keep = np.stack([np.broadcast_to(c >= PAD, (ROW_TILE, KV_TILE)), (c >= PAD) & (c <= r), c <= r])
    return jnp.asarray(np.where(keep, 0.0, NEG_INF), dtype=f32)


def _attn_kernel(q_ref, k_ref, v_ref, bias_ref, o_ref, m_sc, acc_sc, s_sc):
    qi = pl.program_id(2)
    rep = KV_TILE // LANES
    heads = range(ATTN_GROUP)

    def scores(t):
        start = pl.multiple_of(t * KV_TILE, KV_TILE)
        return [lax.dot_general(q_ref[0, hh], k_ref[0, hh, pl.ds(start, KV_TILE), :],
                                (((1,), (1,)), ((), ())), preferred_element_type=f32) for hh in heads]

    def absorb(t, s_all, mask):
        start = pl.multiple_of(t * KV_TILE, KV_TILE)
        for hh in heads:
            s = s_all[hh]
            if mask is not None:
                s = s + bias_ref[mask]
            m_prev = m_sc[hh]
            m_next = jnp.maximum(m_prev, jnp.max(s, axis=-1, keepdims=True))
            alpha = jnp.exp2(m_prev - m_next)
            p = jnp.exp2(s - jnp.tile(m_next, (1, rep))).astype(bf16)
            v = v_ref[0, hh, pl.ds(start, KV_TILE), :]
            acc_sc[hh] = alpha * acc_sc[hh] + jnp.dot(p, v, preferred_element_type=f32)
            m_sc[hh] = m_next

    def stash(s_all):
        for hh in heads:
            s_sc[hh] = s_all[hh]

    def step(t, mask):
        cur = [s_sc[hh] for hh in heads]
        stash(scores(t + 1))
        absorb(t, cur, mask)

    m_sc[...] = jnp.full(m_sc.shape, NEG_INF, f32)
    acc_sc[...] = jnp.zeros(acc_sc.shape, f32)
    first = scores(0)

    @pl.when(qi == 0)
    def _():
        absorb(0, first, MASK_PAD_CAUSAL)

    @pl.when(qi > 0)
    def _():
        stash(first)
        step(0, MASK_PAD)

        n_plain = qi - 1

        def body(u, c):
            step(2 * u + 1, None)
            step(2 * u + 2, None)
            return c
        lax.fori_loop(0, n_plain // 2, body, 0)

        @pl.when(n_plain % 2 == 1)
        def _():
            step(qi - 1, None)
        absorb(qi, [s_sc[hh] for hh in heads], MASK_CAUSAL)

    for pr in range(ATTN_GROUP // 2):
        a0, a1 = acc_sc[2 * pr], acc_sc[2 * pr + 1]
        lane = lax.broadcasted_iota(jnp.int32, a0.shape, 1)
        o_ref[0, pr] = jnp.where(lane < MLA_V, a0 * (1.0 / a0[:, LANES - 1:]), a1 * (1.0 / a1[:, :1])).astype(bf16)


def _attention(q, k, v):
    batch, _, lp, _ = q.shape
    nt = lp // ROW_TILE
    whole = lambda b, g, i: (b, g, 0, 0)
    tile = lambda b, g, i: (b, g, i, 0)
    return pl.pallas_call(
        _attn_kernel,
        out_shape=jax.ShapeDtypeStruct((batch, MLA_HEADS // 2, lp, HEAD_LANES), bf16),
        grid=(batch, MLA_HEADS // ATTN_GROUP, nt),
        in_specs=[
            pl.BlockSpec((1, ATTN_GROUP, ROW_TILE, HEAD_LANES), tile),
            pl.BlockSpec((1, ATTN_GROUP, lp, HEAD_LANES), whole),
            pl.BlockSpec((1, ATTN_GROUP, lp, HEAD_LANES), whole),
            _resident((3, ROW_TILE, KV_TILE)),
        ],
        out_specs=pl.BlockSpec((1, ATTN_GROUP // 2, ROW_TILE, HEAD_LANES), tile),
        scratch_shapes=[pltpu.VMEM((ATTN_GROUP, ROW_TILE, LANES), f32)] * 2
        + [pltpu.VMEM((ATTN_GROUP, ROW_TILE, KV_TILE), f32)],
        compiler_params=_cparams(("parallel", "parallel", "arbitrary")),
        name="mla_attention",
    )(q, k, v, _attn_bias())


def _chunk_cumsum(lg, tril):
    hi = lg.astype(bf16)
    r1 = lg - hi.astype(f32)
    mid = r1.astype(bf16)
    lo = (r1 - mid.astype(f32)).astype(bf16)
    return (jnp.dot(tril, hi, preferred_element_type=f32) + jnp.dot(tril, mid, preferred_element_type=f32)
            + jnp.dot(tril, lo, preferred_element_type=f32))


def _gated_scan(sub_heads, q_sc, k_sc, lg_sc, v_ref, tril_ref, s_sc, o_sc):
    n_groups = q_sc.shape[1] // LANES
    n_chunks = q_sc.shape[0] // CHUNK
    gv = sub_heads * LANES
    dk = LANES // sub_heads
    ci = lax.broadcasted_iota(jnp.int32, (CHUNK, CHUNK), 0)
    cj = lax.broadcasted_iota(jnp.int32, (CHUNK, CHUNK), 1)
    causal = ci >= cj
    lane = lax.broadcasted_iota(jnp.int32, (CHUNK, LANES), 1)
    if sub_heads > 1:
        srow = lax.broadcasted_iota(jnp.int32, (gv, LANES), 0) // LANES
        scol = lax.broadcasted_iota(jnp.int32, (gv, LANES), 1) // dk
        own_block = srow == scol
    tril = tril_ref[...]
    rows = [slice(c * CHUNK, (c + 1) * CHUNK) for c in range(n_chunks)]
    nt_dims = (((1,), (1,)), ((), ()))
    tn_dims = (((0,), (0,)), ((), ()))

    bs = [_chunk_cumsum(lg_sc[r, :], tril) for r in rows]

    qe, ke, kd, qb, decay = [], [], [], [], []
    for r, b in zip(rows, bs):
        q = q_sc[r, :]
        k = k_sc[r, :]
        b_mid = b[CHUNK // 2 - 1:CHUNK // 2, :]
        b_last = b[CHUNK - 1:CHUNK, :]
        qe.append((q * jnp.exp(b - b_mid)).astype(bf16))
        ke.append((k * jnp.exp(b_mid - b)).astype(bf16))
        kd.append((k * jnp.exp(b_last - b)).astype(bf16))
        qb.append((q * jnp.exp(b)).astype(bf16))
        decay.append(jnp.exp(b_last))

    o_intra = [[None] * (n_groups * sub_heads) for _ in rows]
    d_state = [[None] * n_groups for _ in rows]
    for c, r in enumerate(rows):
        v = v_ref[r, :]
        for g in range(n_groups):
            kl = slice(g * LANES, (g + 1) * LANES)
            vg = v[:, g * gv:(g + 1) * gv]
            for s in range(sub_heads):
                qs = qe[c][:, kl]
                if sub_heads > 1:
                    qs = jnp.where(lane // dk == s, qs, jnp.zeros_like(qs))
                a = lax.dot_general(qs, ke[c][:, kl], nt_dims, preferred_element_type=f32)
                a = jnp.where(causal, a, 0.0).astype(bf16)
                o_intra[c][g * sub_heads + s] = jnp.dot(a, vg[:, s * LANES:(s + 1) * LANES],
                                                        preferred_element_type=f32)
            d = lax.dot_general(vg, kd[c][:, kl], tn_dims, preferred_element_type=f32)
            d_state[c][g] = jnp.where(own_block, d, 0.0) if sub_heads > 1 else d

    for g in range(n_groups):
        kl = slice(g * LANES, (g + 1) * LANES)
        st = s_sc[g]
        for c, r in enumerate(rows):
            o_inter = lax.dot_general(qb[c][:, kl], st.astype(bf16), nt_dims, preferred_element_type=f32)
            for s in range(sub_heads):
                head = g * sub_heads + s
                o_sc[r, head * LANES:(head + 1) * LANES] = o_intra[c][head] + o_inter[:, s * LANES:(s + 1) * LANES]
            st = st * decay[c][:, kl] + d_state[c][g]
        s_sc[g] = st


def _gate_and_norm(o_sc, gain_ref, r_ref, out_ref, n_heads):
    for h in range(n_heads):
        sl = slice(h * LANES, (h + 1) * LANES)
        out_ref[:, sl] = (_rms(o_sc[:, sl], gain_ref[...]) * _silu(r_ref[:, sl].astype(f32))).astype(bf16)


def _pad_row_mask(shape):
    row = pl.program_id(1) * ROW_TILE + lax.broadcasted_iota(jnp.int32, shape, 0)
    return row >= PAD


def _gla_kernel(q_ref, k_ref, v_ref, r_ref, gd_ref, wg2_ref, bg_ref, gain_ref, tril_ref, out_ref,
                q_sc, k_sc, b_sc, s_sc, o_sc):
    @pl.when(pl.program_id(1) == 0)
    def _():
        s_sc[...] = jnp.zeros(s_sc.shape, f32)

    x = jnp.dot(gd_ref[...], wg2_ref[...], preferred_element_type=f32) + bg_ref[...]
    log_g = (jnp.minimum(x, 0.0) - jnp.log(1.0 + jnp.exp(-jnp.abs(x)))) * (1.0 / GLA_GATE_NORM)
    b_sc[...] = jnp.where(_pad_row_mask(log_g.shape), log_g, 0.0)
    q_sc[...] = q_ref[...].astype(f32) * (GLA_DK ** -0.5)
    k_sc[...] = k_ref[...].astype(f32)
    _gated_scan(LANES // GLA_DK, q_sc, k_sc, b_sc, v_ref, tril_ref, s_sc, o_sc)
    _gate_and_norm(o_sc, gain_ref, r_ref, out_ref, GLA_HEADS)


def _hgrn_kernel(q_ref, f_ref, v_ref, r_ref, lb_ref, gain_ref, tril_ref, out_ref,
                 q_sc, k_sc, b_sc, s_sc, o_sc):
    @pl.when(pl.program_id(1) == 0)
    def _():
        s_sc[...] = jnp.zeros(s_sc.shape, f32)

    lb = lb_ref[...]
    f = lb + (1.0 - lb) * _sigmoid(f_ref[...].astype(f32))
    b_sc[...] = jnp.where(_pad_row_mask(f.shape), jnp.log(f), 0.0)
    q_sc[...] = _silu(q_ref[...].astype(f32))
    k_sc[...] = 1.0 - f
    _gated_scan(1, q_sc, k_sc, b_sc, v_ref, tril_ref, s_sc, o_sc)
    _gate_and_norm(o_sc, gain_ref, r_ref, out_ref, HGRN_HEADS)


def _seq_col_spec(width, col, nt):
    return pl.BlockSpec((ROW_TILE, width), lambda b, i: (b * nt + i, col // width))


def _gla(proj, wg2, bg, gain, tril, batch, lp):
    nt = lp // ROW_TILE
    return pl.pallas_call(
        _gla_kernel,
        out_shape=jax.ShapeDtypeStruct((batch * lp, GLA_VW), bf16),
        grid=(batch, nt),
        in_specs=[
            _seq_col_spec(GLA_KW, COL_GLA_Q, nt), _seq_col_spec(GLA_KW, COL_GLA_K, nt),
            _seq_col_spec(GLA_VW, COL_GLA_V, nt), _seq_col_spec(GLA_VW, COL_GLA_R, nt),
            _seq_col_spec(LANES, COL_GLA_GD, nt),
            _resident((LANES, GLA_KW)), _resident((1, GLA_KW)), _resident((1, GLA_DV)),
            _resident((CHUNK, CHUNK)),
        ],
        out_specs=pl.BlockSpec((ROW_TILE, GLA_VW), lambda b, i: (b * nt + i, 0)),
        scratch_shapes=[
            pltpu.VMEM((ROW_TILE, GLA_KW), f32), pltpu.VMEM((ROW_TILE, GLA_KW), f32),
            pltpu.VMEM((ROW_TILE, GLA_KW), f32),
            pltpu.VMEM((GLA_KW // LANES, 2 * LANES, LANES), f32),
            pltpu.VMEM((ROW_TILE, GLA_VW), f32),
        ],
        compiler_params=_cparams(("parallel", "arbitrary")),
        name="gla",
    )(proj, proj, proj, proj, proj, wg2, bg, gain, tril)


def _hgrn(proj, lb, gain, tril, batch, lp):
    nt = lp // ROW_TILE
    return pl.pallas_call(
        _hgrn_kernel,
        out_shape=jax.ShapeDtypeStruct((batch * lp, HGRN_VW), bf16),
        grid=(batch, nt),
        in_specs=[
            _seq_col_spec(HGRN_KW, COL_HG_Q, nt), _seq_col_spec(HGRN_KW, COL_HG_F, nt),
            _seq_col_spec(HGRN_VW, COL_HG_I, nt), _seq_col_spec(HGRN_VW, COL_HG_G, nt),
            _resident((1, HGRN_KW)), _resident((1, LANES)), _resident((CHUNK, CHUNK)),
        ],
        out_specs=pl.BlockSpec((ROW_TILE, HGRN_VW), lambda b, i: (b * nt + i, 0)),
        scratch_shapes=[
            pltpu.VMEM((ROW_TILE, HGRN_KW), f32), pltpu.VMEM((ROW_TILE, HGRN_KW), f32),
            pltpu.VMEM((ROW_TILE, HGRN_KW), f32),
            pltpu.VMEM((HGRN_KW // LANES, LANES, LANES), f32),
            pltpu.VMEM((ROW_TILE, HGRN_VW), f32),
        ],
        compiler_params=_cparams(("parallel", "arbitrary")),
        name="hgrn2",
    )(proj, proj, proj, proj, lb, gain, tril)


def _mix_kernel(h_ref, om_ref, og_ref, oh_ref, g0_ref, g1_ref, g2_ref, wm_ref, wg_ref, wh_ref, wo_ref, out_ref):
    om = jnp.concatenate([om_ref[0, j] for j in range(MLA_HEADS // 2)], axis=-1)
    mixed = (_sigmoid(g0_ref[...].astype(f32)) * jnp.dot(om, wm_ref[...], preferred_element_type=f32)
             + _sigmoid(g1_ref[...].astype(f32)) * jnp.dot(og_ref[...], wg_ref[...], preferred_element_type=f32)
             + _sigmoid(g2_ref[...].astype(f32)) * jnp.dot(oh_ref[...], wh_ref[...], preferred_element_type=f32))
    h2 = h_ref[...] + jnp.dot(mixed.astype(bf16), wo_ref[...], preferred_element_type=f32)
    out_ref[...] = jnp.where(_pad_row_mask(h2.shape), h2, 0.0)


def _mix(h, o_mla, o_gla, o_hgrn, proj, wm, wg, wh, wo, batch, lp):
    nt = lp // ROW_TILE
    row = lambda b, i: (b * nt + i, 0)
    return pl.pallas_call(
        _mix_kernel,
        out_shape=jax.ShapeDtypeStruct((batch * lp, D_MODEL), f32),
        grid=(batch, nt),
        in_specs=[
            pl.BlockSpec((ROW_TILE, D_MODEL), row),
            pl.BlockSpec((1, MLA_HEADS // 2, ROW_TILE, HEAD_LANES), lambda b, i: (b, 0, i, 0)),
            pl.BlockSpec((ROW_TILE, GLA_VW), row), pl.BlockSpec((ROW_TILE, HGRN_VW), row),
            _seq_col_spec(D_MODEL, COL_GATES, nt), _seq_col_spec(D_MODEL, COL_GATES + D_MODEL, nt),
            _seq_col_spec(D_MODEL, COL_GATES + 2 * D_MODEL, nt),
            _resident((MLA_HEADS * MLA_V, D_MODEL)), _resident((GLA_VW, D_MODEL)),
            _resident((HGRN_VW, D_MODEL)), _resident((D_MODEL, D_MODEL)),
        ],
        out_specs=pl.BlockSpec((ROW_TILE, D_MODEL), row),
        compiler_params=_cparams(("parallel", "arbitrary")),
        name="mix_out",
    )(h, o_mla, o_gla, o_hgrn, proj, proj, proj, wm, wg, wh, wo)


def _ffn_kernel(h_ref, g_ref, wg_ref, wu_ref, wd_ref, out_ref):
    h = h_ref[...]
    u = _rms(h, g_ref[...]).astype(bf16)
    acc = h
    start = 0
    for width in FF_CHUNKS:
        sl = slice(start, start + width)
        start += width
        gate = jnp.dot(u, wg_ref[:, sl], preferred_element_type=f32)
        up = jnp.dot(u, wu_ref[:, sl], preferred_element_type=f32)
        acc = acc + jnp.dot((_silu(gate) * up).astype(bf16), wd_ref[sl, :], preferred_element_type=f32)
    out_ref[...] = acc


def _ffn(h, gain, wg, wu, wd):
    tp = h.shape[0]
    row = lambda i: (i, 0)
    return pl.pallas_call(
        _ffn_kernel,
        out_shape=jax.ShapeDtypeStruct((tp, D_MODEL), f32),
        grid=(tp // ROW_TILE,),
        in_specs=[
            pl.BlockSpec((ROW_TILE, D_MODEL), row), _resident((1, D_MODEL)),
            _resident((D_MODEL, D_FF)), _resident((D_MODEL, D_FF)), _resident((D_FF, D_MODEL)),
        ],
        out_specs=pl.BlockSpec((ROW_TILE, D_MODEL), row),
        compiler_params=_cparams(("parallel",)),
        name="dense_ffn",
    )(h, gain, wg, wu, wd)


def _split3(x):
    hi = x.astype(bf16)
    lo = (x - hi.astype(f32)).astype(bf16)
    return hi, lo


def _router_kernel(tiles_per_batch, h_ref, g_ref, whi_ref, wlo_ref, b_ref, lower_ref, out_ref, count_ref, run_sc):
    i = pl.program_id(0)

    @pl.when(i == 0)
    def _():
        run_sc[...] = jnp.zeros(run_sc.shape, f32)

    u = _rms(h_ref[...], g_ref[...])
    u_hi, u_lo = _split3(u)
    logits = (jnp.dot(u_hi, whi_ref[...], preferred_element_type=f32)
              + jnp.dot(u_lo, whi_ref[...], preferred_element_type=f32)
              + jnp.dot(u_hi, wlo_ref[...], preferred_element_type=f32)) + b_ref[...]
    lane = lax.broadcasted_iota(jnp.int32, logits.shape, 1)
    logits = jnp.where(lane < N_EXPERTS, logits, -jnp.inf)
    v1 = jnp.max(logits, axis=-1, keepdims=True)
    i1 = jnp.min(jnp.where(logits == v1, lane, LANES), axis=-1, keepdims=True)
    rest = jnp.where(lane == i1, -jnp.inf, logits)
    v2 = jnp.max(rest, axis=-1, keepdims=True)
    i2 = jnp.min(jnp.where(rest == v2, lane, LANES), axis=-1, keepdims=True)
    e = jnp.exp(v2 - v1)
    g1 = 1.0 / (1.0 + e)
    g2 = e * g1

    row = (i % tiles_per_batch) * ROW_TILE + lax.broadcasted_iota(jnp.int32, logits.shape, 0)
    chosen = jnp.where(((lane == i1) | (lane == i2)) & (row >= PAD), 1.0, 0.0)
    before = jnp.dot(lower_ref[...], chosen.astype(bf16), preferred_element_type=f32) + run_sc[...]
    r1 = jnp.sum(jnp.where(lane == i1, before, 0.0), axis=-1, keepdims=True)
    r2 = jnp.sum(jnp.where(lane == i2, before, 0.0), axis=-1, keepdims=True)
    total = run_sc[...] + jnp.sum(chosen, axis=0, keepdims=True)
    run_sc[...] = total
    count_ref[...] = total

    out = jnp.where(lane == 0, g1, jnp.where(lane == 1, g2, 0.0))
    out = jnp.where(lane == 2, i1.astype(f32), jnp.where(lane == 3, i2.astype(f32), out))
    out = jnp.where(lane == 4, r1, jnp.where(lane == 5, r2, out))
    out_ref[...] = out


def _router(h, gain, w_hi, w_lo, b, tiles_per_batch):
    tp = h.shape[0]
    r = np.arange(ROW_TILE)
    lower = jnp.asarray(r[:, None] > r[None, :], dtype=bf16)
    return pl.pallas_call(
        functools.partial(_router_kernel, tiles_per_batch),
        out_shape=(jax.ShapeDtypeStruct((tp, LANES), f32), jax.ShapeDtypeStruct((1, LANES), f32)),
        grid=(tp // ROW_TILE,),
        in_specs=[
            pl.BlockSpec((ROW_TILE, D_MODEL), lambda i: (i, 0)), _resident((1, D_MODEL)),
            _resident((D_MODEL, LANES)), _resident((D_MODEL, LANES)), _resident((1, LANES)),
            _resident((ROW_TILE, ROW_TILE)),
        ],
        out_specs=(pl.BlockSpec((ROW_TILE, LANES), lambda i: (i, 0)), pl.BlockSpec((1, LANES), lambda i: (0, 0))),
        scratch_shapes=[pltpu.VMEM((1, LANES), f32)],
        compiler_params=_cparams(("arbitrary",)),
        name="moe_router",
    )(h, gain, w_hi, w_lo, b, lower)


def _token_copy(src_tok, token, dst_tok, r, sem):
    return pltpu.make_async_copy(src_tok.at[token], dst_tok.at[pl.ds(r * SUBLANES, SUBLANES), :], sem)


def _gather_tokens(idx_ref, src_tok, dst_tok, sem):
    def body(r, carry):
        _token_copy(src_tok, idx_ref[0, 0, r], dst_tok, r, sem).start()
        return carry
    lax.fori_loop(0, idx_ref.shape[-1], body, 0, unroll=GATHER_UNROLL)


def _wait_tokens(dst_tok, sem):
    pltpu.make_async_copy(dst_tok, dst_tok, sem).wait()


def _tokens_to_rows(buf_tok):
    n = buf_tok.shape[0] // SUBLANES
    return jnp.concatenate([buf_tok[pl.ds(c, n, stride=SUBLANES), :] for c in range(D_MODEL // LANES)], axis=1)


def _rows_to_tokens(rows, dst_tok):
    for c in range(D_MODEL // LANES):
        dst_tok[pl.ds(c, rows.shape[0], stride=SUBLANES), :] = rows[:, c * LANES:(c + 1) * LANES]


def _expert_kernel(be_ref, nu_ref, tok_ref, tok_next_ref, w_ref, g_ref, h_tok, wg_ref, wu_ref, wd_ref, y_ref,
                   xbuf, sem, x_sc, acc_sc):
    i = pl.program_id(0)
    k = pl.program_id(1)
    n_used = nu_ref[0]
    slot = i % 2

    def normalised_rows():
        return _rms(_tokens_to_rows(xbuf), g_ref[...]).astype(bf16)

    def partial_out():
        x = x_sc[slot]
        gate = jnp.dot(x, wg_ref[...], preferred_element_type=f32)
        up = jnp.dot(x, wu_ref[...], preferred_element_type=f32)
        return jnp.dot((_silu(gate) * up).astype(bf16), wd_ref[...], preferred_element_type=f32)

    @pl.when((k == 0) & (i == 0))
    def _():
        _gather_tokens(tok_ref, h_tok, xbuf, sem)
        _wait_tokens(xbuf, sem)
        x_sc[0] = normalised_rows()

    @pl.when((k == 0) & (i < n_used))
    def _():
        for r in range(ROW_TILE):
            _token_copy(h_tok, tok_next_ref[0, 0, r], xbuf, r, sem).start()
        acc_sc[...] = partial_out()

    @pl.when((k == 1) & (i < n_used))
    def _():
        _wait_tokens(xbuf, sem)
        x_sc[1 - slot] = normalised_rows()
        _rows_to_tokens((acc_sc[...] + partial_out()) * w_ref[...], y_ref)

    @pl.when((k == 1) & (i >= n_used))
    def _():
        y_ref[...] = jnp.zeros(y_ref.shape, f32)


def _experts(block_expert, n_used, slot_row, slot_w, gain, h, wg, wu, wd):
    n_blocks = slot_row.shape[0]
    nk = D_EXPERT // EXP_CHUNK

    def live(i, nu):
        return jnp.minimum(i, nu[0] - 1)

    def kk(i, k, nu):
        return jnp.where(i < nu[0], k, nk - 1)

    grid_spec = pltpu.PrefetchScalarGridSpec(
        num_scalar_prefetch=2,
        grid=(n_blocks, nk),
        in_specs=[
            pl.BlockSpec((1, 1, ROW_TILE), lambda i, k, be, nu: (i, 0, 0), memory_space=pltpu.SMEM),
            pl.BlockSpec((1, 1, ROW_TILE), lambda i, k, be, nu: (jnp.minimum(i + 1, n_blocks - 1), 0, 0),
                         memory_space=pltpu.SMEM),
            pl.BlockSpec((ROW_TILE, 1), lambda i, k, be, nu: (i, 0)),
            pl.BlockSpec((1, D_MODEL), lambda i, k, be, nu: (0, 0)),
            pl.BlockSpec(memory_space=pl.ANY),
            pl.BlockSpec((None, D_MODEL, EXP_CHUNK), lambda i, k, be, nu: (be[live(i, nu)], 0, kk(i, k, nu))),
            pl.BlockSpec((None, D_MODEL, EXP_CHUNK), lambda i, k, be, nu: (be[live(i, nu)], 0, kk(i, k, nu))),
            pl.BlockSpec((None, EXP_CHUNK, D_MODEL), lambda i, k, be, nu: (be[live(i, nu)], kk(i, k, nu), 0)),
        ],
        out_specs=pl.BlockSpec((ROW_TILE * SUBLANES, LANES), lambda i, k, be, nu: (i, 0)),
        scratch_shapes=[
            pltpu.VMEM((ROW_TILE * SUBLANES, LANES), f32),
            pltpu.SemaphoreType.DMA(()),
            pltpu.VMEM((2, ROW_TILE, D_MODEL), bf16),
            pltpu.VMEM((ROW_TILE, D_MODEL), f32),
        ],
    )
    assert nk == 2
    h_tok = h.reshape(h.shape[0], SUBLANES, LANES)
    y_tok = pl.pallas_call(
        _expert_kernel,
        out_shape=jax.ShapeDtypeStruct((n_blocks * ROW_TILE * SUBLANES, LANES), f32),
        grid_spec=grid_spec,
        compiler_params=_cparams(("arbitrary", "arbitrary")),
        name="moe_experts",
    )(block_expert, n_used, slot_row, slot_row, slot_w, gain, h_tok, wg, wu, wd)
    return y_tok.reshape(n_blocks * ROW_TILE, SUBLANES, LANES)


def _combine_kernel(rows, tiles_per_batch, batch_stride, first_row, zero_pad_rows,
                    s0_ref, s1_ref, s0_next_ref, s1_next_ref, h_hbm, y_tok, out_ref, ybuf, hbuf, ysems, hsems):
    i = pl.program_id(0)
    slot = i % 2

    def h_copy(tile, sl):
        start = (tile // tiles_per_batch) * batch_stride + first_row + (tile % tiles_per_batch) * rows
        return pltpu.make_async_copy(h_hbm.at[pl.ds(pl.multiple_of(start, LANES), rows), :], hbuf.at[sl],
                                     hsems.at[sl])

    @pl.when(i == 0)
    def _():
        h_copy(0, 0).start()
        _gather_tokens(s0_ref, y_tok, ybuf.at[0, 0], ysems.at[0, 0])
        _gather_tokens(s1_ref, y_tok, ybuf.at[0, 1], ysems.at[0, 1])

    @pl.when(i + 1 < pl.num_programs(0))
    def _():
        h_copy(i + 1, 1 - slot).start()
        _gather_tokens(s0_next_ref, y_tok, ybuf.at[1 - slot, 0], ysems.at[1 - slot, 0])
        _gather_tokens(s1_next_ref, y_tok, ybuf.at[1 - slot, 1], ysems.at[1 - slot, 1])

    h_copy(i, slot).wait()
    _wait_tokens(ybuf.at[slot, 0], ysems.at[slot, 0])
    _wait_tokens(ybuf.at[slot, 1], ysems.at[slot, 1])
    out = hbuf[slot] + _tokens_to_rows(ybuf.at[slot, 0]) + _tokens_to_rows(ybuf.at[slot, 1])
    if zero_pad_rows:
        row = (i % tiles_per_batch) * rows + lax.broadcasted_iota(jnp.int32, out.shape, 0)
        out = jnp.where(row >= PAD, out, 0.0)
    out_ref[...] = out


def _combine(slot_of, h, y_tok, batch, lp, frames_only):
    if frames_only:
        rows, first_row = OUT_TILE, PAD + N_META
    else:
        rows, first_row = ROW_TILE, 0
    assert (lp - first_row) % rows == 0
    per_batch = (lp - first_row) // rows
    n_tiles = batch * per_batch
    slots = slot_of[:, first_row:, :].reshape(n_tiles, 1, rows, TOP_K)
    s0, s1 = slots[..., 0], slots[..., 1]
    idx_spec = pl.BlockSpec((1, 1, rows), lambda i: (i, 0, 0), memory_space=pltpu.SMEM)
    nxt_spec = pl.BlockSpec((1, 1, rows), lambda i: (jnp.minimum(i + 1, n_tiles - 1), 0, 0), memory_space=pltpu.SMEM)
    return pl.pallas_call(
        functools.partial(_combine_kernel, rows, per_batch, lp, first_row, not frames_only),
        out_shape=jax.ShapeDtypeStruct((n_tiles * rows, D_MODEL), f32),
        grid=(n_tiles,),
        in_specs=[idx_spec, idx_spec, nxt_spec, nxt_spec, pl.BlockSpec(memory_space=pl.ANY),
                  pl.BlockSpec(memory_space=pl.ANY)],
        out_specs=pl.BlockSpec((rows, D_MODEL), lambda i: (i, 0)),
        scratch_shapes=[pltpu.VMEM((2, 2, rows * SUBLANES, LANES), f32), pltpu.VMEM((2, rows, D_MODEL), f32),
                        pltpu.SemaphoreType.DMA((2, 2)), pltpu.SemaphoreType.DMA((2,))],
        compiler_params=_cparams(("arbitrary",)),
        name="moe_combine",
    )(s0, s1, s0, s1, h, y_tok)


def _moe(h, gain, w_router, b_router, wg, wu, wd, batch, lp, frames_only):
    seq = lp - PAD
    n_tok = batch * seq
    n_assign = n_tok * TOP_K
    w_r = jnp.pad(w_router, ((0, 0), (0, LANES - N_EXPERTS)))
    w_hi = w_r.astype(bf16)
    w_lo = (w_r - w_hi.astype(f32)).astype(bf16)
    b_r = jnp.pad(b_router, (0, LANES - N_EXPERTS)).reshape(1, LANES)
    routed, count_row = _router(h, gain, w_hi, w_lo, b_r, lp // ROW_TILE)
    routed = routed.reshape(batch, lp, LANES)
    gate = routed[:, PAD:, 0:TOP_K].reshape(n_assign)
    expert_p = routed[:, :, TOP_K:2 * TOP_K].astype(jnp.int32)
    rank_p = routed[:, :, 2 * TOP_K:3 * TOP_K].astype(jnp.int32)
    counts = count_row[0, :N_EXPERTS].astype(jnp.int32)

    e_flat = expert_p[:, PAD:].reshape(n_assign)
    order = jnp.argsort(e_flat)
    start = jnp.cumsum(counts) - counts
    padded = (counts + ROW_TILE - 1) // ROW_TILE * ROW_TILE
    pend = jnp.cumsum(padded)
    pstart = pend - padded
    n_blocks = -(-n_assign // ROW_TILE) + N_EXPERTS
    n_slots = n_blocks * ROW_TILE

    slot = jnp.arange(n_slots, dtype=jnp.int32)
    slot_e = jnp.minimum(jnp.sum((slot[:, None] >= pend[None, :]).astype(jnp.int32), axis=1), N_EXPERTS - 1)
    within = slot - pstart[slot_e]
    live = (within < counts[slot_e]) & (slot < pend[-1])
    assign = order[jnp.clip(start[slot_e] + within, 0, n_assign - 1)]
    tok = assign // TOP_K
    slot_row = jnp.where(live, (tok // seq) * lp + PAD + tok % seq, 0)
    slot_w = jnp.where(live, gate[assign], 0.0)
    block_expert = slot_e[::ROW_TILE]
    n_used = (pend[-1] // ROW_TILE).astype(jnp.int32).reshape(1)

    y = _experts(block_expert, n_used, slot_row.reshape(n_blocks, 1, ROW_TILE), slot_w.reshape(n_slots, 1),
                 gain, h, wg, wu, wd)

    is_pad = (jnp.arange(lp) < PAD)[None, :, None]
    slot_of = jnp.where(is_pad, 0, pstart[expert_p] + rank_p)
    return _combine(slot_of, h, y, batch, lp, frames_only)


def _pad_cols(w, width):
    return jnp.pad(w, ((0, 0), (0, width - w.shape[1])))


def _in_proj_weight(w):
    offs = np.cumsum((MLA_Q_LORA, MLA_KV_LORA, MLA_ROPE, GLA_KW, GLA_KW, GLA_VW, GLA_GATE_RANK, GLA_VW,
                      HGRN_KW, HGRN_KW, HGRN_VW, HGRN_VW, N_BRANCH * D_MODEL))[:-1]
    (dq, dkv, kr, gq, gk, gv, ggd, gr, hq, hf, hi, hg, gates) = jnp.split(w, offs, axis=1)
    kr_p = jnp.pad(kr, ((0, 0), (MLA_NOPE, HEAD_LANES - MLA_NOPE - MLA_ROPE)))
    parts = [dq, dkv, kr_p, gq, gk, gv, gr, hq, hf, hi, hg, gates, _pad_cols(ggd, LANES), _rotate_half_cols(kr_p)]
    return jnp.concatenate(parts, axis=1).astype(bf16)


def _rotate_half_cols(w, sign=-1.0):
    lead = w.shape[0]
    w3 = w.reshape(lead, -1, HEAD_LANES)
    half = MLA_ROPE // 2
    first, second = w3[:, :, MLA_NOPE:MLA_NOPE + half], w3[:, :, MLA_NOPE + half:MLA_QK]
    zero = lambda n: jnp.zeros(w3.shape[:2] + (n,), w.dtype)
    return jnp.concatenate([zero(MLA_NOPE), sign * second, first, zero(HEAD_LANES - MLA_QK)],
                           axis=-1).reshape(lead, -1)


def _mla_weights(w_uq, w_ukv):
    d_in = w_uq.shape[0]
    wq = w_uq.reshape(d_in, MLA_HEADS, MLA_QK)
    wq = jnp.pad(wq, ((0, 0), (0, 0), (0, HEAD_LANES - MLA_QK))).reshape(d_in, MLA_HEADS * HEAD_LANES)
    d_kv = w_ukv.shape[0]
    wkv = w_ukv.reshape(d_kv, MLA_HEADS, MLA_NOPE + MLA_V)
    wk = jnp.pad(wkv[:, :, :MLA_NOPE], ((0, 0), (0, 0), (0, HEAD_LANES - MLA_NOPE)))
    wv = wkv[:, :, MLA_NOPE:]
    zeros = jnp.zeros_like(wv)
    even = (jnp.arange(MLA_HEADS) % 2 == 0)[None, :, None]
    wv = jnp.concatenate([jnp.where(even, wv, zeros), jnp.where(even, zeros, wv)], axis=-1)
    return (wq.astype(bf16), wk.reshape(d_kv, -1).astype(bf16), wv.reshape(d_kv, -1).astype(bf16))


def _head_gain(g, scale):
    return (jnp.pad(g, (0, HEAD_LANES - MLA_QK)) * scale).reshape(1, HEAD_LANES).astype(f32)


def _rope_tables(lp):
    half = MLA_ROPE // 2
    pos = np.maximum(np.arange(lp) - PAD, 0).astype(np.float32)
    inv = (ROPE_THETA ** (-np.arange(half, dtype=np.float32) / half)).astype(np.float32)
    ang = jnp.asarray(pos)[:, None] * jnp.asarray(inv)[None, :]
    cos, sin = jnp.cos(ang), jnp.sin(ang)
    ones = jnp.ones((lp, MLA_NOPE), f32)
    z = lambda n: jnp.zeros((lp, n), f32)
    tail = HEAD_LANES - MLA_QK
    c = jnp.concatenate([ones, cos, cos, jnp.ones((lp, tail), f32)], axis=1)
    s = jnp.concatenate([z(MLA_NOPE), sin, sin, z(tail)], axis=1)
    return c, s


def _head_indicator():
    lane_head = np.arange(2 * HEAD_LANES) // HEAD_LANES
    return jnp.asarray(lane_head[:, None] == lane_head[None, :], dtype=bf16)


def _chunk_tril():
    r = np.arange(CHUNK)
    return jnp.asarray(r[:, None] >= r[None, :], dtype=bf16)


def kernel(x, meta_tokens, attn_norm, w_in, mla_cq_norm, w_mla_uq, mla_ckv_norm, w_mla_ukv, mla_q_norm, mla_k_norm, w_gla_g2, b_gla_g, gla_o_norm, hgrn_lower_bounds, hgrn_o_norm, w_mla_o, w_gla_o, w_hgrn_o, w_out, ffn_norm, w_ff_gate, w_ff_up, w_ff_down, w_router, b_router, w_exp_gate, w_exp_up, w_exp_down):
    batch, seq, _ = x.shape
    depth = w_in.shape[0]
    lp = PAD + N_META + seq
    assert lp % ROW_TILE == 0 and ROW_TILE % CHUNK == 0 and KV_TILE == ROW_TILE

    meta = jnp.broadcast_to(meta_tokens[None].astype(x.dtype), (batch, N_META, D_MODEL))
    h = jnp.concatenate([jnp.zeros((batch, PAD, D_MODEL), x.dtype), meta, x], axis=1).reshape(batch * lp, D_MODEL)

    lbs = jax.nn.softmax(hgrn_lower_bounds.astype(f32), axis=0)
    lbs = jnp.cumsum(lbs, axis=0) - lbs[0]
    rope_c, rope_s = _rope_tables(lp)
    head_ind = _head_indicator()
    tril = _chunk_tril()
    row1 = lambda v: v.reshape(1, -1).astype(f32)

    for layer in range(depth):
        proj = _in_proj(h, row1(attn_norm[layer]), _in_proj_weight(w_in[layer]))
        wq, wk, wv = _mla_weights(w_mla_uq[layer], w_mla_ukv[layer])
        q_gain = _head_gain(mla_q_norm[layer], MLA_QK ** -0.5 * LOG2_E)
        k_gain = _head_gain(mla_k_norm[layer], 1.0)
        q, k, v = _mla_proj(proj, row1(mla_cq_norm[layer]), row1(mla_ckv_norm[layer]),
                            wq, _rotate_half_cols(wq), wk, wv, head_ind,
                            q_gain, _rotate_half_cols(q_gain, 1.0), k_gain, _rotate_half_cols(k_gain, 1.0),
                            rope_c, rope_s, batch, lp)
        o_mla = _attention(q, k, v)
        wg2 = jnp.pad(w_gla_g2[layer], ((0, LANES - GLA_GATE_RANK), (0, 0))).astype(bf16)
        o_gla = _gla(proj, wg2, row1(b_gla_g[layer]), row1(gla_o_norm[layer]), tril, batch, lp)
        o_hgrn = _hgrn(proj, row1(lbs[layer]), row1(hgrn_o_norm[layer]), tril, batch, lp)
        h = _mix(h, o_mla, o_gla, o_hgrn, proj, w_mla_o[layer].astype(bf16), w_gla_o[layer].astype(bf16),
                 w_hgrn_o[layer].astype(bf16), w_out[layer].astype(bf16), batch, lp)
        i = layer // 2
        if layer % 2 == 0:
            h = _ffn(h, row1(ffn_norm[layer]), w_ff_gate[i].astype(bf16), w_ff_up[i].astype(bf16),
                     w_ff_down[i].astype(bf16))
        else:
            h = _moe(h, row1(ffn_norm[layer]), w_router[i], b_router[i], w_exp_gate[i].astype(bf16),
                     w_exp_up[i].astype(bf16), w_exp_down[i].astype(bf16), batch, lp, layer == depth - 1)
    if depth % 2 == 0:
        return h.reshape(batch, seq, D_MODEL)
    return h.reshape(batch, lp, D_MODEL)[:, PAD + N_META:]
```

```python
import functools

import numpy as np
import jax
import jax.numpy as jnp
from jax import lax
from jax.experimental import pallas as pl
from jax.experimental.pallas import tpu as pltpu

f32 = jnp.float32
bf16 = jnp.bfloat16

D_MODEL = 1024
N_META = 16
PAD = 112
CHUNK = 64
EPS = 1e-6
NEG_INF = -1e30
LOG2_E = 1.4426950408889634
MLA_HEADS = 8
MLA_Q_LORA = 256
MLA_KV_LORA = 128
MLA_NOPE = 64
MLA_ROPE = 32
MLA_V = 64
MLA_QK = MLA_NOPE + MLA_ROPE
ROPE_THETA = 10000.0
GLA_HEADS = 4
GLA_DK = 64
GLA_DV = 128
GLA_KW = GLA_HEADS * GLA_DK
GLA_VW = GLA_HEADS * GLA_DV
GLA_GATE_RANK = 16
GLA_GATE_NORM = 16.0
HGRN_HEADS = 4
HGRN_DK = 128
HGRN_KW = 512
HGRN_VW = 512
N_BRANCH = 3
D_FF = 2816
N_EXPERTS = 8
TOP_K = 2
D_EXPERT = 3584

LANES = 128
SUBLANES = 8
HEAD_LANES = 128
VMEM_LIMIT = 56 * 1024 * 1024

ROW_TILE = 640
KV_TILE = 640
ATTN_GROUP = 2
FF_CHUNKS = (1280, 1536)
EXP_CHUNK = 1792
GATHER_UNROLL = 8
OUT_TILE = 512

COL_MLA = 0
COL_GLA_Q = 512
COL_GLA_K = 768
COL_GLA_V = 1024
COL_GLA_R = 1536
COL_HG_Q = 2048
COL_HG_F = 2560
COL_HG_I = 3072
COL_HG_G = 3584
COL_GATES = 4096
COL_GLA_GD = 7168
COL_KR_ROT = 7296
IN_COLS_P = 7424
IN_COL_TILE = 3712


def _cparams(sem):
    return pltpu.CompilerParams(dimension_semantics=sem, vmem_limit_bytes=VMEM_LIMIT)


def _resident(shape):
    nd = len(shape)
    return pl.BlockSpec(shape, lambda *_: (0,) * nd, pipeline_mode=pl.Buffered(1))


def _silu(x):
    return x * (1.0 / (1.0 + jnp.exp(-x)))


def _sigmoid(x):
    return 1.0 / (1.0 + jnp.exp(-x))


def _rms(x, gain):
    ms = jnp.mean(x * x, axis=-1, keepdims=True)
    return x * lax.rsqrt(ms + EPS) * gain


def _in_proj_kernel(h_ref, g_ref, w_ref, o_ref, u_sc):
    @pl.when(pl.program_id(1) == 0)
    def _():
        u_sc[...] = _rms(h_ref[...], g_ref[...]).astype(bf16)

    o_ref[...] = jnp.dot(u_sc[...], w_ref[...], preferred_element_type=f32).astype(bf16)


def _in_proj(h, gain, w):
    tp = h.shape[0]
    return pl.pallas_call(
        _in_proj_kernel,
        out_shape=jax.ShapeDtypeStruct((tp, IN_COLS_P), bf16),
        grid=(tp // ROW_TILE, IN_COLS_P // IN_COL_TILE),
        in_specs=[
            pl.BlockSpec((ROW_TILE, D_MODEL), lambda i, j: (i, 0)),
            pl.BlockSpec((1, D_MODEL), lambda i, j: (0, 0)),
            pl.BlockSpec((D_MODEL, IN_COL_TILE), lambda i, j: (0, j)),
        ],
        out_specs=pl.BlockSpec((ROW_TILE, IN_COL_TILE), lambda i, j: (i, j)),
        scratch_shapes=[pltpu.VMEM((ROW_TILE, D_MODEL), bf16)],
        compiler_params=_cparams(("parallel", "arbitrary")),
        name="in_proj",
    )(h, gain, w)


def _head_mean_square(x, ind_ref):
    sq = (x * x).astype(bf16)
    w = ind_ref.shape[0]
    sums = [jnp.dot(sq[:, g * w:(g + 1) * w], ind_ref[...], preferred_element_type=f32)
            for g in range(x.shape[1] // w)]
    return jnp.concatenate(sums, axis=1) * (1.0 / MLA_QK)


def _mla_proj_kernel(p_ref, krr_ref, cqn_ref, ckvn_ref, wuq_ref, wuqr_ref, wk_ref, wv_ref, ind_ref,
                     qg_ref, qgr_ref, kg_ref, kgr_ref, c_ref, s_ref, q_ref, k_ref, v_ref):
    p = p_ref[...].astype(f32)
    cq = _rms(p[:, :MLA_Q_LORA], cqn_ref[...]).astype(bf16)
    ckv = _rms(p[:, MLA_Q_LORA:MLA_Q_LORA + MLA_KV_LORA], ckvn_ref[...]).astype(bf16)
    k_rope = p[:, MLA_Q_LORA + MLA_KV_LORA:]
    q_all = jnp.dot(cq, wuq_ref[...], preferred_element_type=f32)
    q_rot = jnp.dot(cq, wuqr_ref[...], preferred_element_type=f32)
    k_all = jnp.dot(ckv, wk_ref[...], preferred_element_type=f32) + jnp.tile(k_rope, (1, MLA_HEADS))
    v_all = jnp.dot(ckv, wv_ref[...], preferred_element_type=f32)
    q_scale = lax.rsqrt(_head_mean_square(q_all, ind_ref) + EPS)
    k_scale = lax.rsqrt(_head_mean_square(k_all, ind_ref) + EPS)
    c, s = c_ref[...], s_ref[...]
    q_c, q_s = qg_ref[...] * c, qgr_ref[...] * s
    k_c, k_s = kg_ref[...] * c, kgr_ref[...] * s
    k_rot = krr_ref[...].astype(f32) * k_s
    lane = lax.broadcasted_iota(jnp.int32, c.shape, 1)
    for h in range(MLA_HEADS):
        sl = slice(h * HEAD_LANES, (h + 1) * HEAD_LANES)
        q_ref[0, h] = (q_scale[:, sl] * (q_all[:, sl] * q_c + q_rot[:, sl] * q_s)).astype(bf16)
        k_ref[0, h] = (k_scale[:, sl] * (k_all[:, sl] * k_c + k_rot)).astype(bf16)
        one_lane = LANES - 1 if h % 2 == 0 else 0
        v_ref[0, h] = jnp.where(lane == one_lane, 1.0, v_all[:, sl]).astype(bf16)


def _mla_proj(proj, cqn, ckvn, wuq, wuqr, wk, wv, ind, qg, qgr, kg, kgr, rope_c, rope_s, batch, lp):
    nt = lp // ROW_TILE
    hl = MLA_HEADS * HEAD_LANES
    head_spec = pl.BlockSpec((1, MLA_HEADS, ROW_TILE, HEAD_LANES), lambda i: (i // nt, 0, i % nt, 0))
    tab_spec = pl.BlockSpec((ROW_TILE, HEAD_LANES), lambda i: (i % nt, 0))
    gain_spec = _resident((1, HEAD_LANES))
    shp = jax.ShapeDtypeStruct((batch, MLA_HEADS, lp, HEAD_LANES), bf16)
    return pl.pallas_call(
        _mla_proj_kernel,
        out_shape=(shp, shp, shp),
        grid=(batch * nt,),
        in_specs=[
            pl.BlockSpec((ROW_TILE, 512), lambda i: (i, COL_MLA // 512)),
            pl.BlockSpec((ROW_TILE, LANES), lambda i: (i, COL_KR_ROT // LANES)),
            _resident((1, MLA_Q_LORA)), _resident((1, MLA_KV_LORA)),
            _resident((MLA_Q_LORA, hl)), _resident((MLA_Q_LORA, hl)),
            _resident((MLA_KV_LORA, hl)), _resident((MLA_KV_LORA, hl)),
            _resident((2 * HEAD_LANES, 2 * HEAD_LANES)),
            gain_spec, gain_spec, gain_spec, gain_spec,
            tab_spec, tab_spec,
        ],
        out_specs=(head_spec, head_spec, head_spec),
        compiler_params=_cparams(("parallel",)),
        name="mla_proj",
    )(proj, proj, cqn, ckvn, wuq, wuqr, wk, wv, ind, qg, qgr, kg, kgr, rope_c, rope_s)


MASK_PAD, MASK_PAD_CAUSAL, MASK_CAUSAL = 0, 1, 2


def _attn_bias():
    r = np.arange(ROW_TILE)[:, None]
    c = np.arange(KV_TILE)[None, :]
    keep = np.stack([np.broadcast_to(c >= PAD, (ROW_TILE, KV_TILE)), (c >= PAD) & (c <= r), c <= r])
    return jnp.asarray(np.where(keep, 0.0, NEG_INF), dtype=f32)


def _attn_kernel(q_ref, k_ref, v_ref, bias_ref, o_ref, m_sc, acc_sc, s_sc):
    qi = pl.program_id(2)
    rep = KV_TILE // LANES
    heads = range(ATTN_GROUP)

    def scores(t):
        start = pl.multiple_of(t * KV_TILE, KV_TILE)
        return [lax.dot_general(q_ref[0, hh], k_ref[0, hh, pl.ds(start, KV_TILE), :],
                                (((1,), (1,)), ((), ())), preferred_element_type=f32) for hh in heads]

    def absorb(t, s_all, mask):
        start = pl.multiple_of(t * KV_TILE, KV_TILE)
        for hh in heads:
            s = s_all[hh]
            if mask is not None:
                s = s + bias_ref[mask]
            m_prev = m_sc[hh]
            m_next = jnp.maximum(m_prev, jnp.max(s, axis=-1, keepdims=True))
            alpha = jnp.exp2(m_prev - m_next)
            p = jnp.exp2(s - jnp.tile(m_next, (1, rep))).astype(bf16)
            v = v_ref[0, hh, pl.ds(start, KV_TILE), :]
            acc_sc[hh] = alpha * acc_sc[hh] + jnp.dot(p, v, preferred_element_type=f32)
            m_sc[hh] = m_next

    def stash(s_all):
        for hh in heads:
            s_sc[hh] = s_all[hh]

    def step(t, mask):
        cur = [s_sc[hh] for hh in heads]
        stash(scores(t + 1))
        absorb(t, cur, mask)

    m_sc[...] = jnp.full(m_sc.shape, NEG_INF, f32)
    acc_sc[...] = jnp.zeros(acc_sc.shape, f32)
    first = scores(0)

    @pl.when(qi == 0)
    def _():
        absorb(0, first, MASK_PAD_CAUSAL)

    @pl.when(qi > 0)
    def _():
        stash(first)
        step(0, MASK_PAD)

        n_plain = qi - 1

        def body(u, c):
            step(2 * u + 1, None)
            step(2 * u + 2, None)
            return c
        lax.fori_loop(0, n_plain // 2, body, 0)

        @pl.when(n_plain % 2 == 1)
        def _():
            step(qi - 1, None)
        absorb(qi, [s_sc[hh] for hh in heads], MASK_CAUSAL)

    for pr in range(ATTN_GROUP // 2):
        a0, a1 = acc_sc[2 * pr], acc_sc[2 * pr + 1]
        lane = lax.broadcasted_iota(jnp.int32, a0.shape, 1)
        o_ref[0, pr] = jnp.where(lane < MLA_V, a0 * (1.0 / a0[:, LANES - 1:]), a1 * (1.0 / a1[:, :1])).astype(bf16)


def _attention(q, k, v):
    batch, _, lp, _ = q.shape
    nt = lp // ROW_TILE
    whole = lambda b, g, i: (b, g, 0, 0)
    tile = lambda b, g, i: (b, g, i, 0)
    return pl.pallas_call(
        _attn_kernel,
        out_shape=jax.ShapeDtypeStruct((batch, MLA_HEADS // 2, lp, HEAD_LANES), bf16),
        grid=(batch, MLA_HEADS // ATTN_GROUP, nt),
        in_specs=[
            pl.BlockSpec((1, ATTN_GROUP, ROW_TILE, HEAD_LANES), tile),
            pl.BlockSpec((1, ATTN_GROUP, lp, HEAD_LANES), whole),
            pl.BlockSpec((1, ATTN_GROUP, lp, HEAD_LANES), whole),
            _resident((3, ROW_TILE, KV_TILE)),
        ],
        out_specs=pl.BlockSpec((1, ATTN_GROUP // 2, ROW_TILE, HEAD_LANES), tile),
        scratch_shapes=[pltpu.VMEM((ATTN_GROUP, ROW_TILE, LANES), f32)] * 2
        + [pltpu.VMEM((ATTN_GROUP, ROW_TILE, KV_TILE), f32)],
        compiler_params=_cparams(("parallel", "parallel", "arbitrary")),
        name="mla_attention",
    )(q, k, v, _attn_bias())


def _chunk_cumsum(lg, tril):
    hi = lg.astype(bf16)
    r1 = lg - hi.astype(f32)
    mid = r1.astype(bf16)
    lo = (r1 - mid.astype(f32)).astype(bf16)
    return (jnp.dot(tril, hi, preferred_element_type=f32) + jnp.dot(tril, mid, preferred_element_type=f32)
            + jnp.dot(tril, lo, preferred_element_type=f32))


def _gated_scan(sub_heads, q_sc, k_sc, lg_sc, v_ref, tril_ref, s_sc, o_sc):
    n_groups = q_sc.shape[1] // LANES
    n_chunks = q_sc.shape[0] // CHUNK
    gv = sub_heads * LANES
    dk = LANES // sub_heads
    ci = lax.broadcasted_iota(jnp.int32, (CHUNK, CHUNK), 0)
    cj = lax.broadcasted_iota(jnp.int32, (CHUNK, CHUNK), 1)
    causal = ci >= cj
    lane = lax.broadcasted_iota(jnp.int32, (CHUNK, LANES), 1)
    if sub_heads > 1:
        srow = lax.broadcasted_iota(jnp.int32, (gv, LANES), 0) // LANES
        scol = lax.broadcasted_iota(jnp.int32, (gv, LANES), 1) // dk
        own_block = srow == scol
    tril = tril_ref[...]
    rows = [slice(c * CHUNK, (c + 1) * CHUNK) for c in range(n_chunks)]
    nt_dims = (((1,), (1,)), ((), ()))
    tn_dims = (((0,), (0,)), ((), ()))

    bs = [_chunk_cumsum(lg_sc[r, :], tril) for r in rows]

    qe, ke, kd, qb, decay = [], [], [], [], []
    for r, b in zip(rows, bs):
        q = q_sc[r, :]
        k = k_sc[r, :]
        b_mid = b[CHUNK // 2 - 1:CHUNK // 2, :]
        b_last = b[CHUNK - 1:CHUNK, :]
        qe.append((q * jnp.exp(b - b_mid)).astype(bf16))
        ke.append((k * jnp.exp(b_mid - b)).astype(bf16))
        kd.append((k * jnp.exp(b_last - b)).astype(bf16))
        qb.append((q * jnp.exp(b)).astype(bf16))
        decay.append(jnp.exp(b_last))

    o_intra = [[None] * (n_groups * sub_heads) for _ in rows]
    d_state = [[None] * n_groups for _ in rows]
    for c, r in enumerate(rows):
        v = v_ref[r, :]
        for g in range(n_groups):
            kl = slice(g * LANES, (g + 1) * LANES)
            vg = v[:, g * gv:(g + 1) * gv]
            for s in range(sub_heads):
                qs = qe[c][:, kl]
                if sub_heads > 1:
                    qs = jnp.where(lane // dk == s, qs, jnp.zeros_like(qs))
                a = lax.dot_general(qs, ke[c][:, kl], nt_dims, preferred_element_type=f32)
                a = jnp.where(causal, a, 0.0).astype(bf16)
                o_intra[c][g * sub_heads + s] = jnp.dot(a, vg[:, s * LANES:(s + 1) * LANES],
                                                        preferred_element_type=f32)
            d = lax.dot_general(vg, kd[c][:, kl], tn_dims, preferred_element_type=f32)
            d_state[c][g] = jnp.where(own_block, d, 0.0) if sub_heads > 1 else d

    for g in range(n_groups):
        kl = slice(g * LANES, (g + 1) * LANES)
        st = s_sc[g]
        for c, r in enumerate(rows):
            o_inter = lax.dot_general(qb[c][:, kl], st.astype(bf16), nt_dims, preferred_element_type=f32)
            for s in range(sub_heads):
                head = g * sub_heads + s
                o_sc[r, head * LANES:(head + 1) * LANES] = o_intra[c][head] + o_inter[:, s * LANES:(s + 1) * LANES]
            st = st * decay[c][:, kl] + d_state[c][g]
        s_sc[g] = st


def _gate_and_norm(o_sc, gain_ref, r_ref, out_ref, n_heads):
    for h in range(n_heads):
        sl = slice(h * LANES, (h + 1) * LANES)
        out_ref[:, sl] = (_rms(o_sc[:, sl], gain_ref[...]) * _silu(r_ref[:, sl].astype(f32))).astype(bf16)


def _pad_row_mask(shape):
    row = pl.program_id(1) * ROW_TILE + lax.broadcasted_iota(jnp.int32, shape, 0)
    return row >= PAD


def _gla_kernel(q_ref, k_ref, v_ref, r_ref, gd_ref, wg2_ref, bg_ref, gain_ref, tril_ref, out_ref,
                q_sc, k_sc, b_sc, s_sc, o_sc):
    @pl.when(pl.program_id(1) == 0)
    def _():
        s_sc[...] = jnp.zeros(s_sc.shape, f32)

    x = jnp.dot(gd_ref[...], wg2_ref[...], preferred_element_type=f32) + bg_ref[...]
    log_g = (jnp.minimum(x, 0.0) - jnp.log(1.0 + jnp.exp(-jnp.abs(x)))) * (1.0 / GLA_GATE_NORM)
    b_sc[...] = jnp.where(_pad_row_mask(log_g.shape), log_g, 0.0)
    q_sc[...] = q_ref[...].astype(f32) * (GLA_DK ** -0.5)
    k_sc[...] = k_ref[...].astype(f32)
    _gated_scan(LANES // GLA_DK, q_sc, k_sc, b_sc, v_ref, tril_ref, s_sc, o_sc)
    _gate_and_norm(o_sc, gain_ref, r_ref, out_ref, GLA_HEADS)


def _hgrn_kernel(q_ref, f_ref, v_ref, r_ref, lb_ref, gain_ref, tril_ref, out_ref,
                 q_sc, k_sc, b_sc, s_sc, o_sc):
    @pl.when(pl.program_id(1) == 0)
    def _():
        s_sc[...] = jnp.zeros(s_sc.shape, f32)

    lb = lb_ref[...]
    f = lb + (1.0 - lb) * _sigmoid(f_ref[...].astype(f32))
    b_sc[...] = jnp.where(_pad_row_mask(f.shape), jnp.log(f), 0.0)
    q_sc[...] = _silu(q_ref[...].astype(f32))
    k_sc[...] = 1.0 - f
    _gated_scan(1, q_sc, k_sc, b_sc, v_ref, tril_ref, s_sc, o_sc)
    _gate_and_norm(o_sc, gain_ref, r_ref, out_ref, HGRN_HEADS)


def _seq_col_spec(width, col, nt):
    return pl.BlockSpec((ROW_TILE, width), lambda b, i: (b * nt + i, col // width))


def _gla(proj, wg2, bg, gain, tril, batch, lp):
    nt = lp // ROW_TILE
    return pl.pallas_call(
        _gla_kernel,
        out_shape=jax.ShapeDtypeStruct((batch * lp, GLA_VW), bf16),
        grid=(batch, nt),
        in_specs=[
            _seq_col_spec(GLA_KW, COL_GLA_Q, nt), _seq_col_spec(GLA_KW, COL_GLA_K, nt),
            _seq_col_spec(GLA_VW, COL_GLA_V, nt), _seq_col_spec(GLA_VW, COL_GLA_R, nt),
            _seq_col_spec(LANES, COL_GLA_GD, nt),
            _resident((LANES, GLA_KW)), _resident((1, GLA_KW)), _resident((1, GLA_DV)),
            _resident((CHUNK, CHUNK)),
        ],
        out_specs=pl.BlockSpec((ROW_TILE, GLA_VW), lambda b, i: (b * nt + i, 0)),
        scratch_shapes=[
            pltpu.VMEM((ROW_TILE, GLA_KW), f32), pltpu.VMEM((ROW_TILE, GLA_KW), f32),
            pltpu.VMEM((ROW_TILE, GLA_KW), f32),
            pltpu.VMEM((GLA_KW // LANES, 2 * LANES, LANES), f32),
            pltpu.VMEM((ROW_TILE, GLA_VW), f32),
        ],
        compiler_params=_cparams(("parallel", "arbitrary")),
        name="gla",
    )(proj, proj, proj, proj, proj, wg2, bg, gain, tril)


def _hgrn(proj, lb, gain, tril, batch, lp):
    nt = lp // ROW_TILE
    return pl.pallas_call(
        _hgrn_kernel,
        out_shape=jax.ShapeDtypeStruct((batch * lp, HGRN_VW), bf16),
        grid=(batch, nt),
        in_specs=[
            _seq_col_spec(HGRN_KW, COL_HG_Q, nt), _seq_col_spec(HGRN_KW, COL_HG_F, nt),
            _seq_col_spec(HGRN_VW, COL_HG_I, nt), _seq_col_spec(HGRN_VW, COL_HG_G, nt),
            _resident((1, HGRN_KW)), _resident((1, LANES)), _resident((CHUNK, CHUNK)),
        ],
        out_specs=pl.BlockSpec((ROW_TILE, HGRN_VW), lambda b, i: (b * nt + i, 0)),
        scratch_shapes=[
            pltpu.VMEM((ROW_TILE, HGRN_KW), f32), pltpu.VMEM((ROW_TILE, HGRN_KW), f32),
            pltpu.VMEM((ROW_TILE, HGRN_KW), f32),
            pltpu.VMEM((HGRN_KW // LANES, LANES, LANES), f32),
            pltpu.VMEM((ROW_TILE, HGRN_VW), f32),
        ],
        compiler_params=_cparams(("parallel", "arbitrary")),
        name="hgrn2",
    )(proj, proj, proj, proj, lb, gain, tril)


def _mix_kernel(h_ref, om_ref, og_ref, oh_ref, g0_ref, g1_ref, g2_ref, wm_ref, wg_ref, wh_ref, wo_ref, out_ref):
    om = jnp.concatenate([om_ref[0, j] for j in range(MLA_HEADS // 2)], axis=-1)
    mixed = (_sigmoid(g0_ref[...].astype(f32)) * jnp.dot(om, wm_ref[...], preferred_element_type=f32)
             + _sigmoid(g1_ref[...].astype(f32)) * jnp.dot(og_ref[...], wg_ref[...], preferred_element_type=f32)
             + _sigmoid(g2_ref[...].astype(f32)) * jnp.dot(oh_ref[...], wh_ref[...], preferred_element_type=f32))
    h2 = h_ref[...] + jnp.dot(mixed.astype(bf16), wo_ref[...], preferred_element_type=f32)
    out_ref[...] = jnp.where(_pad_row_mask(h2.shape), h2, 0.0)


def _mix(h, o_mla, o_gla, o_hgrn, proj, wm, wg, wh, wo, batch, lp):
    nt = lp // ROW_TILE
    row = lambda b, i: (b * nt + i, 0)
    return pl.pallas_call(
        _mix_kernel,
        out_shape=jax.ShapeDtypeStruct((batch * lp, D_MODEL), f32),
        grid=(batch, nt),
        in_specs=[
            pl.BlockSpec((ROW_TILE, D_MODEL), row),
            pl.BlockSpec((1, MLA_HEADS // 2, ROW_TILE, HEAD_LANES), lambda b, i: (b, 0, i, 0)),
            pl.BlockSpec((ROW_TILE, GLA_VW), row), pl.BlockSpec((ROW_TILE, HGRN_VW), row),
            _seq_col_spec(D_MODEL, COL_GATES, nt), _seq_col_spec(D_MODEL, COL_GATES + D_MODEL, nt),
            _seq_col_spec(D_MODEL, COL_GATES + 2 * D_MODEL, nt),
            _resident((MLA_HEADS * MLA_V, D_MODEL)), _resident((GLA_VW, D_MODEL)),
            _resident((HGRN_VW, D_MODEL)), _resident((D_MODEL, D_MODEL)),
        ],
        out_specs=pl.BlockSpec((ROW_TILE, D_MODEL), row),
        compiler_params=_cparams(("parallel", "arbitrary")),
        name="mix_out",
    )(h, o_mla, o_gla, o_hgrn, proj, proj, proj, wm, wg, wh, wo)


def _ffn_kernel(h_ref, g_ref, wg_ref, wu_ref, wd_ref, out_ref):
    h = h_ref[...]
    u = _rms(h, g_ref[...]).astype(bf16)
    acc = h
    start = 0
    for width in FF_CHUNKS:
        sl = slice(start, start + width)
        start += width
        gate = jnp.dot(u, wg_ref[:, sl], preferred_element_type=f32)
        up = jnp.dot(u, wu_ref[:, sl], preferred_element_type=f32)
        acc = acc + jnp.dot((_silu(gate) * up).astype(bf16), wd_ref[sl, :], preferred_element_type=f32)
    out_ref[...] = acc


def _ffn(h, gain, wg, wu, wd):
    tp = h.shape[0]
    row = lambda i: (i, 0)
    return pl.pallas_call(
        _ffn_kernel,
        out_shape=jax.ShapeDtypeStruct((tp, D_MODEL), f32),
        grid=(tp // ROW_TILE,),
        in_specs=[
            pl.BlockSpec((ROW_TILE, D_MODEL), row), _resident((1, D_MODEL)),
            _resident((D_MODEL, D_FF)), _resident((D_MODEL, D_FF)), _resident((D_FF, D_MODEL)),
        ],
        out_specs=pl.BlockSpec((ROW_TILE, D_MODEL), row),
        compiler_params=_cparams(("parallel",)),
        name="dense_ffn",
    )(h, gain, wg, wu, wd)


def _split3(x):
    hi = x.astype(bf16)
    lo = (x - hi.astype(f32)).astype(bf16)
    return hi, lo


def _router_kernel(tiles_per_batch, h_ref, g_ref, whi_ref, wlo_ref, b_ref, lower_ref, out_ref, count_ref, run_sc):
    i = pl.program_id(0)

    @pl.when(i == 0)
    def _():
        run_sc[...] = jnp.zeros(run_sc.shape, f32)

    u = _rms(h_ref[...], g_ref[...])
    u_hi, u_lo = _split3(u)
    logits = (jnp.dot(u_hi, whi_ref[...], preferred_element_type=f32)
              + jnp.dot(u_lo, whi_ref[...], preferred_element_type=f32)
              + jnp.dot(u_hi, wlo_ref[...], preferred_element_type=f32)) + b_ref[...]
    lane = lax.broadcasted_iota(jnp.int32, logits.shape, 1)
    logits = jnp.where(lane < N_EXPERTS, logits, -jnp.inf)
    v1 = jnp.max(logits, axis=-1, keepdims=True)
    i1 = jnp.min(jnp.where(logits == v1, lane, LANES), axis=-1, keepdims=True)
    rest = jnp.where(lane == i1, -jnp.inf, logits)
    v2 = jnp.max(rest, axis=-1, keepdims=True)
    i2 = jnp.min(jnp.where(rest == v2, lane, LANES), axis=-1, keepdims=True)
    e = jnp.exp(v2 - v1)
    g1 = 1.0 / (1.0 + e)
    g2 = e * g1

    row = (i % tiles_per_batch) * ROW_TILE + lax.broadcasted_iota(jnp.int32, logits.shape, 0)
    chosen = jnp.where(((lane == i1) | (lane == i2)) & (row >= PAD), 1.0, 0.0)
    before = jnp.dot(lower_ref[...], chosen.astype(bf16), preferred_element_type=f32) + run_sc[...]
    r1 = jnp.sum(jnp.where(lane == i1, before, 0.0), axis=-1, keepdims=True)
    r2 = jnp.sum(jnp.where(lane == i2, before, 0.0), axis=-1, keepdims=True)
    total = run_sc[...] + jnp.sum(chosen, axis=0, keepdims=True)
    run_sc[...] = total
    count_ref[...] = total

    out = jnp.where(lane == 0, g1, jnp.where(lane == 1, g2, 0.0))
    out = jnp.where(lane == 2, i1.astype(f32), jnp.where(lane == 3, i2.astype(f32), out))
    out = jnp.where(lane == 4, r1, jnp.where(lane == 5, r2, out))
    out_ref[...] = out


def _router(h, gain, w_hi, w_lo, b, tiles_per_batch):
    tp = h.shape[0]
    r = np.arange(ROW_TILE)
    lower = jnp.asarray(r[:, None] > r[None, :], dtype=bf16)
    return pl.pallas_call(
        functools.partial(_router_kernel, tiles_per_batch),
        out_shape=(jax.ShapeDtypeStruct((tp, LANES), f32), jax.ShapeDtypeStruct((1, LANES), f32)),
        grid=(tp // ROW_TILE,),
        in_specs=[
            pl.BlockSpec((ROW_TILE, D_MODEL), lambda i: (i, 0)), _resident((1, D_MODEL)),
            _resident((D_MODEL, LANES)), _resident((D_MODEL, LANES)), _resident((1, LANES)),
            _resident((ROW_TILE, ROW_TILE)),
        ],
        out_specs=(pl.BlockSpec((ROW_TILE, LANES), lambda i: (i, 0)), pl.BlockSpec((1, LANES), lambda i: (0, 0))),
        scratch_shapes=[pltpu.VMEM((1, LANES), f32)],
        compiler_params=_cparams(("arbitrary",)),
        name="moe_router",
    )(h, gain, w_hi, w_lo, b, lower)


def _token_copy(src_tok, token, dst_tok, r, sem):
    return pltpu.make_async_copy(src_tok.at[token], dst_tok.at[pl.ds(r * SUBLANES, SUBLANES), :], sem)


def _gather_tokens(idx_ref, src_tok, dst_tok, sem):
    def body(r, carry):
        _token_copy(src_tok, idx_ref[0, 0, r], dst_tok, r, sem).start()
        return carry
    lax.fori_loop(0, idx_ref.shape[-1], body, 0, unroll=GATHER_UNROLL)


def _wait_tokens(dst_tok, sem):
    pltpu.make_async_copy(dst_tok, dst_tok, sem).wait()


def _tokens_to_rows(buf_tok):
    n = buf_tok.shape[0] // SUBLANES
    return jnp.concatenate([buf_tok[pl.ds(c, n, stride=SUBLANES), :] for c in range(D_MODEL // LANES)], axis=1)


def _rows_to_tokens(rows, dst_tok):
    for c in range(D_MODEL // LANES):
        dst_tok[pl.ds(c, rows.shape[0], stride=SUBLANES), :] = rows[:, c * LANES:(c + 1) * LANES]


def _expert_kernel(be_ref, nu_ref, tok_ref, tok_next_ref, w_ref, g_ref, h_tok, wg_ref, wu_ref, wd_ref, y_ref,
                   xbuf, sem, x_sc, acc_sc):
    i = pl.program_id(0)
    k = pl.program_id(1)
    n_used = nu_ref[0]
    slot = i % 2

    def normalised_rows():
        return _rms(_tokens_to_rows(xbuf), g_ref[...]).astype(bf16)

    def partial_out():
        x = x_sc[slot]
        gate = jnp.dot(x, wg_ref[...], preferred_element_type=f32)
        up = jnp.dot(x, wu_ref[...], preferred_element_type=f32)
        return jnp.dot((_silu(gate) * up).astype(bf16), wd_ref[...], preferred_element_type=f32)

    @pl.when((k == 0) & (i == 0))
    def _():
        _gather_tokens(tok_ref, h_tok, xbuf, sem)
        _wait_tokens(xbuf, sem)
        x_sc[0] = normalised_rows()

    @pl.when((k == 0) & (i < n_used))
    def _():
        for r in range(ROW_TILE):
            _token_copy(h_tok, tok_next_ref[0, 0, r], xbuf, r, sem).start()
        acc_sc[...] = partial_out()

    @pl.when((k == 1) & (i < n_used))
    def _():
        _wait_tokens(xbuf, sem)
        x_sc[1 - slot] = normalised_rows()
        _rows_to_tokens((acc_sc[...] + partial_out()) * w_ref[...], y_ref)

    @pl.when((k == 1) & (i >= n_used))
    def _():
        y_ref[...] = jnp.zeros(y_ref.shape, f32)


def _experts(block_expert, n_used, slot_row, slot_w, gain, h, wg, wu, wd):
    n_blocks = slot_row.shape[0]
    nk = D_EXPERT // EXP_CHUNK

    def live(i, nu):
        return jnp.minimum(i, nu[0] - 1)

    def kk(i, k, nu):
        return jnp.where(i < nu[0], k, nk - 1)

    grid_spec = pltpu.PrefetchScalarGridSpec(
        num_scalar_prefetch=2,
        grid=(n_blocks, nk),
        in_specs=[
            pl.BlockSpec((1, 1, ROW_TILE), lambda i, k, be, nu: (i, 0, 0), memory_space=pltpu.SMEM),
            pl.BlockSpec((1, 1, ROW_TILE), lambda i, k, be, nu: (jnp.minimum(i + 1, n_blocks - 1), 0, 0),
                         memory_space=pltpu.SMEM),
            pl.BlockSpec((ROW_TILE, 1), lambda i, k, be, nu: (i, 0)),
            pl.BlockSpec((1, D_MODEL), lambda i, k, be, nu: (0, 0)),
            pl.BlockSpec(memory_space=pl.ANY),
            pl.BlockSpec((None, D_MODEL, EXP_CHUNK), lambda i, k, be, nu: (be[live(i, nu)], 0, kk(i, k, nu))),
            pl.BlockSpec((None, D_MODEL, EXP_CHUNK), lambda i, k, be, nu: (be[live(i, nu)], 0, kk(i, k, nu))),
            pl.BlockSpec((None, EXP_CHUNK, D_MODEL), lambda i, k, be, nu: (be[live(i, nu)], kk(i, k, nu), 0)),
        ],
        out_specs=pl.BlockSpec((ROW_TILE * SUBLANES, LANES), lambda i, k, be, nu: (i, 0)),
        scratch_shapes=[
            pltpu.VMEM((ROW_TILE * SUBLANES, LANES), f32),
            pltpu.SemaphoreType.DMA(()),
            pltpu.VMEM((2, ROW_TILE, D_MODEL), bf16),
            pltpu.VMEM((ROW_TILE, D_MODEL), f32),
        ],
    )
    assert nk == 2
    h_tok = h.reshape(h.shape[0], SUBLANES, LANES)
    y_tok = pl.pallas_call(
        _expert_kernel,
        out_shape=jax.ShapeDtypeStruct((n_blocks * ROW_TILE * SUBLANES, LANES), f32),
        grid_spec=grid_spec,
        compiler_params=_cparams(("arbitrary", "arbitrary")),
        name="moe_experts",
    )(block_expert, n_used, slot_row, slot_row, slot_w, gain, h_tok, wg, wu, wd)
    return y_tok.reshape(n_blocks * ROW_TILE, SUBLANES, LANES)


def _combine_kernel(rows, tiles_per_batch, batch_stride, first_row, zero_pad_rows,
                    s0_ref, s1_ref, s0_next_ref, s1_next_ref, h_hbm, y_tok, out_ref, ybuf, hbuf, ysems, hsems):
    i = pl.program_id(0)
    slot = i % 2

    def h_copy(tile, sl):
        start = (tile // tiles_per_batch) * batch_stride + first_row + (tile % tiles_per_batch) * rows
        return pltpu.make_async_copy(h_hbm.at[pl.ds(pl.multiple_of(start, LANES), rows), :], hbuf.at[sl],
                                     hsems.at[sl])

    @pl.when(i == 0)
    def _():
        h_copy(0, 0).start()
        _gather_tokens(s0_ref, y_tok, ybuf.at[0, 0], ysems.at[0, 0])
        _gather_tokens(s1_ref, y_tok, ybuf.at[0, 1], ysems.at[0, 1])

    @pl.when(i + 1 < pl.num_programs(0))
    def _():
        h_copy(i + 1, 1 - slot).start()
        _gather_tokens(s0_next_ref, y_tok, ybuf.at[1 - slot, 0], ysems.at[1 - slot, 0])
        _gather_tokens(s1_next_ref, y_tok, ybuf.at[1 - slot, 1], ysems.at[1 - slot, 1])

    h_copy(i, slot).wait()
    _wait_tokens(ybuf.at[slot, 0], ysems.at[slot, 0])
    _wait_tokens(ybuf.at[slot, 1], ysems.at[slot, 1])
    out = hbuf[slot] + _tokens_to_rows(ybuf.at[slot, 0]) + _tokens_to_rows(ybuf.at[slot, 1])
    if zero_pad_rows:
        row = (i % tiles_per_batch) * rows + lax.broadcasted_iota(jnp.int32, out.shape, 0)
        out = jnp.where(row >= PAD, out, 0.0)
    out_ref[...] = out


def _combine(slot_of, h, y_tok, batch, lp, frames_only):
    if frames_only:
        rows, first_row = OUT_TILE, PAD + N_META
    else:
        rows, first_row = ROW_TILE, 0
    assert (lp - first_row) % rows == 0
    per_batch = (lp - first_row) // rows
    n_tiles = batch * per_batch
    slots = slot_of[:, first_row:, :].reshape(n_tiles, 1, rows, TOP_K)
    s0, s1 = slots[..., 0], slots[..., 1]
    idx_spec = pl.BlockSpec((1, 1, rows), lambda i: (i, 0, 0), memory_space=pltpu.SMEM)
    nxt_spec = pl.BlockSpec((1, 1, rows), lambda i: (jnp.minimum(i + 1, n_tiles - 1), 0, 0), memory_space=pltpu.SMEM)
    return pl.pallas_call(
        functools.partial(_combine_kernel, rows, per_batch, lp, first_row, not frames_only),
        out_shape=jax.ShapeDtypeStruct((n_tiles * rows, D_MODEL), f32),
        grid=(n_tiles,),
        in_specs=[idx_spec, idx_spec, nxt_spec, nxt_spec, pl.BlockSpec(memory_space=pl.ANY),
                  pl.BlockSpec(memory_space=pl.ANY)],
        out_specs=pl.BlockSpec((rows, D_MODEL), lambda i: (i, 0)),
        scratch_shapes=[pltpu.VMEM((2, 2, rows * SUBLANES, LANES), f32), pltpu.VMEM((2, rows, D_MODEL), f32),
                        pltpu.SemaphoreType.DMA((2, 2)), pltpu.SemaphoreType.DMA((2,))],
        compiler_params=_cparams(("arbitrary",)),
        name="moe_combine",
    )(s0, s1, s0, s1, h, y_tok)


def _moe(h, gain, w_router, b_router, wg, wu, wd, batch, lp, frames_only):
    seq = lp - PAD
    n_tok = batch * seq
    n_assign = n_tok * TOP_K
    w_r = jnp.pad(w_router, ((0, 0), (0, LANES - N_EXPERTS)))
    w_hi = w_r.astype(bf16)
    w_lo = (w_r - w_hi.astype(f32)).astype(bf16)
    b_r = jnp.pad(b_router, (0, LANES - N_EXPERTS)).reshape(1, LANES)
    routed, count_row = _router(h, gain, w_hi, w_lo, b_r, lp // ROW_TILE)
    routed = routed.reshape(batch, lp, LANES)
    gate = routed[:, PAD:, 0:TOP_K].reshape(n_assign)
    expert_p = routed[:, :, TOP_K:2 * TOP_K].astype(jnp.int32)
    rank_p = routed[:, :, 2 * TOP_K:3 * TOP_K].astype(jnp.int32)
    counts = count_row[0, :N_EXPERTS].astype(jnp.int32)

    e_flat = expert_p[:, PAD:].reshape(n_assign)
    idx_bits = max(1, (n_assign - 1).bit_length())
    packed = jnp.sort(e_flat * (1 << idx_bits) + jnp.arange(n_assign, dtype=jnp.int32))
    order = packed & ((1 << idx_bits) - 1)
    start = jnp.cumsum(counts) - counts
    padded = (counts + ROW_TILE - 1) // ROW_TILE * ROW_TILE
    pend = jnp.cumsum(padded)
    pstart = pend - padded
    n_blocks = -(-n_assign // ROW_TILE) + N_EXPERTS
    n_slots = n_blocks * ROW_TILE

    block_first = jnp.arange(n_blocks, dtype=jnp.int32) * ROW_TILE
    block_expert = jnp.minimum(jnp.sum((block_first[:, None] >= pend[None, :]).astype(jnp.int32), axis=1),
                               N_EXPERTS - 1)
    within = (block_first - pstart[block_expert])[:, None] + jnp.arange(ROW_TILE, dtype=jnp.int32)[None, :]
    live = (within < counts[block_expert][:, None]) & (block_first < pend[-1])[:, None]
    assign = order[jnp.clip(start[block_expert][:, None] + within, 0, n_assign - 1)]
    tok = assign // TOP_K
    slot_row = jnp.where(live, (tok // seq) * lp + PAD + tok % seq, 0)
    slot_w = jnp.where(live, gate[assign], 0.0)
    n_used = (pend[-1] // ROW_TILE).astype(jnp.int32).reshape(1)

    y = _experts(block_expert, n_used, slot_row.reshape(n_blocks, 1, ROW_TILE), slot_w.reshape(n_slots, 1),
                 gain, h, wg, wu, wd)

    first_slot = jnp.zeros(expert_p.shape, jnp.int32)
    for e in range(N_EXPERTS):
        first_slot = jnp.where(expert_p == e, pstart[e], first_slot)
    is_pad = (jnp.arange(lp) < PAD)[None, :, None]
    slot_of = jnp.where(is_pad, 0, first_slot + rank_p)
    return _combine(slot_of, h, y, batch, lp, frames_only)


def _pad_cols(w, width):
    return jnp.pad(w, ((0, 0), (0, width - w.shape[1])))


def _in_proj_weight(w):
    offs = np.cumsum((MLA_Q_LORA, MLA_KV_LORA, MLA_ROPE, GLA_KW, GLA_KW, GLA_VW, GLA_GATE_RANK, GLA_VW,
                      HGRN_KW, HGRN_KW, HGRN_VW, HGRN_VW, N_BRANCH * D_MODEL))[:-1]
    (dq, dkv, kr, gq, gk, gv, ggd, gr, hq, hf, hi, hg, gates) = jnp.split(w, offs, axis=1)
    kr_p = jnp.pad(kr, ((0, 0), (MLA_NOPE, HEAD_LANES - MLA_NOPE - MLA_ROPE)))
    parts = [dq, dkv, kr_p, gq, gk, gv, gr, hq, hf, hi, hg, gates, _pad_cols(ggd, LANES), _rotate_half_cols(kr_p)]
    return jnp.concatenate(parts, axis=1).astype(bf16)


def _rotate_half_cols(w, sign=-1.0):
    lead = w.shape[0]
    w3 = w.reshape(lead, -1, HEAD_LANES)
    half = MLA_ROPE // 2
    first, second = w3[:, :, MLA_NOPE:MLA_NOPE + half], w3[:, :, MLA_NOPE + half:MLA_QK]
    zero = lambda n: jnp.zeros(w3.shape[:2] + (n,), w.dtype)
    return jnp.concatenate([zero(MLA_NOPE), sign * second, first, zero(HEAD_LANES - MLA_QK)],
                           axis=-1).reshape(lead, -1)


def _mla_weights(w_uq, w_ukv):
    d_in = w_uq.shape[0]
    wq = w_uq.reshape(d_in, MLA_HEADS, MLA_QK)
    wq = jnp.pad(wq, ((0, 0), (0, 0), (0, HEAD_LANES - MLA_QK))).reshape(d_in, MLA_HEADS * HEAD_LANES)
    d_kv = w_ukv.shape[0]
    wkv = w_ukv.reshape(d_kv, MLA_HEADS, MLA_NOPE + MLA_V)
    wk = jnp.pad(wkv[:, :, :MLA_NOPE], ((0, 0), (0, 0), (0, HEAD_LANES - MLA_NOPE)))
    wv = wkv[:, :, MLA_NOPE:]
    zeros = jnp.zeros_like(wv)
    even = (jnp.arange(MLA_HEADS) % 2 == 0)[None, :, None]
    wv = jnp.concatenate([jnp.where(even, wv, zeros), jnp.where(even, zeros, wv)], axis=-1)
    return (wq.astype(bf16), wk.reshape(d_kv, -1).astype(bf16), wv.reshape(d_kv, -1).astype(bf16))


def _head_gain(g, scale):
    return (jnp.pad(g, (0, HEAD_LANES - MLA_QK)) * scale).reshape(1, HEAD_LANES).astype(f32)


def _rope_tables(lp):
    half = MLA_ROPE // 2
    pos = np.maximum(np.arange(lp) - PAD, 0).astype(np.float32)
    inv = (ROPE_THETA ** (-np.arange(half, dtype=np.float32) / half)).astype(np.float32)
    ang = jnp.asarray(pos)[:, None] * jnp.asarray(inv)[None, :]
    cos, sin = jnp.cos(ang), jnp.sin(ang)
    ones = jnp.ones((lp, MLA_NOPE), f32)
    z = lambda n: jnp.zeros((lp, n), f32)
    tail = HEAD_LANES - MLA_QK
    c = jnp.concatenate([ones, cos, cos, jnp.ones((lp, tail), f32)], axis=1)
    s = jnp.concatenate([z(MLA_NOPE), sin, sin, z(tail)], axis=1)
    return c, s


def _head_indicator():
    lane_head = np.arange(2 * HEAD_LANES) // HEAD_LANES
    return jnp.asarray(lane_head[:, None] == lane_head[None, :], dtype=bf16)


def _chunk_tril():
    r = np.arange(CHUNK)
    return jnp.asarray(r[:, None] >= r[None, :], dtype=bf16)


def kernel(x, meta_tokens, attn_norm, w_in, mla_cq_norm, w_mla_uq, mla_ckv_norm, w_mla_ukv, mla_q_norm, mla_k_norm, w_gla_g2, b_gla_g, gla_o_norm, hgrn_lower_bounds, hgrn_o_norm, w_mla_o, w_gla_o, w_hgrn_o, w_out, ffn_norm, w_ff_gate, w_ff_up, w_ff_down, w_router, b_router, w_exp_gate, w_exp_up, w_exp_down):
    batch, seq, _ = x.shape
    depth = w_in.shape[0]
    lp = PAD + N_META + seq
    assert lp % ROW_TILE == 0 and ROW_TILE % CHUNK == 0 and KV_TILE == ROW_TILE

    meta = jnp.broadcast_to(meta_tokens[None].astype(x.dtype), (batch, N_META, D_MODEL))
    h = jnp.concatenate([jnp.zeros((batch, PAD, D_MODEL), x.dtype), meta, x], axis=1).reshape(batch * lp, D_MODEL)

    lbs = jax.nn.softmax(hgrn_lower_bounds.astype(f32), axis=0)
    lbs = jnp.cumsum(lbs, axis=0) - lbs[0]
    rope_c, rope_s = _rope_tables(lp)
    head_ind = _head_indicator()
    tril = _chunk_tril()
    row1 = lambda v: v.reshape(1, -1).astype(f32)

    for layer in range(depth):
        proj = _in_proj(h, row1(attn_norm[layer]), _in_proj_weight(w_in[layer]))
        wq, wk, wv = _mla_weights(w_mla_uq[layer], w_mla_ukv[layer])
        q_gain = _head_gain(mla_q_norm[layer], MLA_QK ** -0.5 * LOG2_E)
        k_gain = _head_gain(mla_k_norm[layer], 1.0)
        q, k, v = _mla_proj(proj, row1(mla_cq_norm[layer]), row1(mla_ckv_norm[layer]),
                            wq, _rotate_half_cols(wq), wk, wv, head_ind,
                            q_gain, _rotate_half_cols(q_gain, 1.0), k_gain, _rotate_half_cols(k_gain, 1.0),
                            rope_c, rope_s, batch, lp)
        o_mla = _attention(q, k, v)
        wg2 = jnp.pad(w_gla_g2[layer], ((0, LANES - GLA_GATE_RANK), (0, 0))).astype(bf16)
        o_gla = _gla(proj, wg2, row1(b_gla_g[layer]), row1(gla_o_norm[layer]), tril, batch, lp)
        o_hgrn = _hgrn(proj, row1(lbs[layer]), row1(hgrn_o_norm[layer]), tril, batch, lp)
        h = _mix(h, o_mla, o_gla, o_hgrn, proj, w_mla_o[layer].astype(bf16), w_gla_o[layer].astype(bf16),
                 w_hgrn_o[layer].astype(bf16), w_out[layer].astype(bf16), batch, lp)
        i = layer // 2
        if layer % 2 == 0:
            h = _ffn(h, row1(ffn_norm[layer]), w_ff_gate[i].astype(bf16), w_ff_up[i].astype(bf16),
                     w_ff_down[i].astype(bf16))
        else:
            h = _moe(h, row1(ffn_norm[layer]), w_router[i], b_router[i], w_exp_gate[i].astype(bf16),
                     w_exp_up[i].astype(bf16), w_exp_down[i].astype(bf16), batch, lp, layer == depth - 1)
    if depth % 2 == 0:
        return h.reshape(batch, seq, D_MODEL)
    return h.reshape(batch, lp, D_MODEL)[:, PAD + N_META:]
```

```python
import functools

import numpy as np
import jax
import jax.numpy as jnp
from jax import lax
from jax.experimental import pallas as pl
from jax.experimental.pallas import tpu as pltpu

f32 = jnp.float32
bf16 = jnp.bfloat16

D_MODEL = 1024
N_META = 16
PAD = 112
CHUNK = 64
EPS = 1e-6
NEG_INF = -1e30
LOG2_E = 1.4426950408889634
MLA_HEADS = 8
MLA_Q_LORA = 256
MLA_KV_LORA = 128
MLA_NOPE = 64
MLA_ROPE = 32
MLA_V = 64
MLA_QK = MLA_NOPE + MLA_ROPE
ROPE_THETA = 10000.0
GLA_HEADS = 4
GLA_DK = 64
GLA_DV = 128
GLA_KW = GLA_HEADS * GLA_DK
GLA_VW = GLA_HEADS * GLA_DV
GLA_GATE_RANK = 16
GLA_GATE_NORM = 16.0
HGRN_HEADS = 4
HGRN_DK = 128
HGRN_KW = 512
HGRN_VW = 512
N_BRANCH = 3
D_FF = 2816
N_EXPERTS = 8
TOP_K = 2
D_EXPERT = 3584

LANES = 128
SUBLANES = 8
HEAD_LANES = 128
VMEM_LIMIT = 56 * 1024 * 1024

ROW_TILE = 640
KV_TILE = 640
ATTN_GROUP = 2
FF_CHUNKS = (1280, 1536)
EXP_CHUNK = 1792
GATHER_UNROLL = 8
OUT_TILE = 512

COL_MLA = 0
COL_GLA_Q = 512
COL_GLA_K = 768
COL_GLA_V = 1024
COL_GLA_R = 1536
COL_HG_Q = 2048
COL_HG_F = 2560
COL_HG_I = 3072
COL_HG_G = 3584
COL_GATES = 4096
COL_GLA_GD = 7168
COL_KR_ROT = 7296
IN_COLS_P = 7424
IN_COL_TILE = 3712


def _cparams(sem):
    return pltpu.CompilerParams(dimension_semantics=sem, vmem_limit_bytes=VMEM_LIMIT)


def _resident(shape):
    nd = len(shape)
    return pl.BlockSpec(shape, lambda *_: (0,) * nd, pipeline_mode=pl.Buffered(1))


def _silu(x):
    return x * (1.0 / (1.0 + jnp.exp(-x)))


def _sigmoid(x):
    return 1.0 / (1.0 + jnp.exp(-x))


def _rms(x, gain):
    ms = jnp.mean(x * x, axis=-1, keepdims=True)
    return x * lax.rsqrt(ms + EPS) * gain


def _in_proj_kernel(h_ref, g_ref, w_ref, o_ref):
    u = _rms(h_ref[...], g_ref[...]).astype(bf16)
    for j in range(IN_COLS_P // IN_COL_TILE):
        sl = slice(j * IN_COL_TILE, (j + 1) * IN_COL_TILE)
        o_ref[:, sl] = jnp.dot(u, w_ref[:, sl], preferred_element_type=f32).astype(bf16)


def _in_proj(h, gain, w):
    tp = h.shape[0]
    return pl.pallas_call(
        _in_proj_kernel,
        out_shape=jax.ShapeDtypeStruct((tp, IN_COLS_P), bf16),
        grid=(tp // ROW_TILE,),
        in_specs=[
            pl.BlockSpec((ROW_TILE, D_MODEL), lambda i: (i, 0)),
            _resident((1, D_MODEL)),
            _resident((D_MODEL, IN_COLS_P)),
        ],
        out_specs=pl.BlockSpec((ROW_TILE, IN_COLS_P), lambda i: (i, 0)),
        compiler_params=_cparams(("parallel",)),
        name="in_proj",
    )(h, gain, w)


def _head_mean_square(x, ind_ref):
    sq = (x * x).astype(bf16)
    w = ind_ref.shape[0]
    sums = [jnp.dot(sq[:, g * w:(g + 1) * w], ind_ref[...], preferred_element_type=f32)
            for g in range(x.shape[1] // w)]
    return jnp.concatenate(sums, axis=1) * (1.0 / MLA_QK)


def _mla_proj_kernel(p_ref, krr_ref, cqn_ref, ckvn_ref, wuq_ref, wuqr_ref, wk_ref, wv_ref, ind_ref,
                     qg_ref, qgr_ref, kg_ref, kgr_ref, c_ref, s_ref, q_ref, k_ref, v_ref):
    p = p_ref[...].astype(f32)
    cq = _rms(p[:, :MLA_Q_LORA], cqn_ref[...]).astype(bf16)
    ckv = _rms(p[:, MLA_Q_LORA:MLA_Q_LORA + MLA_KV_LORA], ckvn_ref[...]).astype(bf16)
    k_rope = p[:, MLA_Q_LORA + MLA_KV_LORA:]
    q_all = jnp.dot(cq, wuq_ref[...], preferred_element_type=f32)
    q_rot = jnp.dot(cq, wuqr_ref[...], preferred_element_type=f32)
    k_all = jnp.dot(ckv, wk_ref[...], preferred_element_type=f32) + jnp.tile(k_rope, (1, MLA_HEADS))
    v_all = jnp.dot(ckv, wv_ref[...], preferred_element_type=f32)
    q_scale = lax.rsqrt(_head_mean_square(q_all, ind_ref) + EPS)
    k_scale = lax.rsqrt(_head_mean_square(k_all, ind_ref) + EPS)
    c, s = c_ref[...], s_ref[...]
    q_c, q_s = qg_ref[...] * c, qgr_ref[...] * s
    k_c, k_s = kg_ref[...] * c, kgr_ref[...] * s
    k_rot = krr_ref[...].astype(f32) * k_s
    lane = lax.broadcasted_iota(jnp.int32, c.shape, 1)
    for h in range(MLA_HEADS):
        sl = slice(h * HEAD_LANES, (h + 1) * HEAD_LANES)
        q_ref[0, h] = (q_scale[:, sl] * (q_all[:, sl] * q_c + q_rot[:, sl] * q_s)).astype(bf16)
        k_ref[0, h] = (k_scale[:, sl] * (k_all[:, sl] * k_c + k_rot)).T.astype(bf16)
        one_lane = LANES - 1 if h % 2 == 0 else 0
        v_ref[0, h] = jnp.where(lane == one_lane, 1.0, v_all[:, sl]).astype(bf16)


def _mla_proj(proj, cqn, ckvn, wuq, wuqr, wk, wv, ind, qg, qgr, kg, kgr, rope_c, rope_s, batch, lp):
    nt = lp // ROW_TILE
    hl = MLA_HEADS * HEAD_LANES
    head_spec = pl.BlockSpec((1, MLA_HEADS, ROW_TILE, HEAD_LANES), lambda i: (i // nt, 0, i % nt, 0))
    tab_spec = pl.BlockSpec((ROW_TILE, HEAD_LANES), lambda i: (i % nt, 0))
    gain_spec = _resident((1, HEAD_LANES))
    shp = jax.ShapeDtypeStruct((batch, MLA_HEADS, lp, HEAD_LANES), bf16)
    kt_shp = jax.ShapeDtypeStruct((batch, MLA_HEADS, HEAD_LANES, lp), bf16)
    kt_spec = pl.BlockSpec((1, MLA_HEADS, HEAD_LANES, ROW_TILE), lambda i: (i // nt, 0, 0, i % nt))
    return pl.pallas_call(
        _mla_proj_kernel,
        out_shape=(shp, kt_shp, shp),
        grid=(batch * nt,),
        in_specs=[
            pl.BlockSpec((ROW_TILE, 512), lambda i: (i, COL_MLA // 512)),
            pl.BlockSpec((ROW_TILE, LANES), lambda i: (i, COL_KR_ROT // LANES)),
            _resident((1, MLA_Q_LORA)), _resident((1, MLA_KV_LORA)),
            _resident((MLA_Q_LORA, hl)), _resident((MLA_Q_LORA, hl)),
            _resident((MLA_KV_LORA, hl)), _resident((MLA_KV_LORA, hl)),
            _resident((2 * HEAD_LANES, 2 * HEAD_LANES)),
            gain_spec, gain_spec, gain_spec, gain_spec,
            tab_spec, tab_spec,
        ],
        out_specs=(head_spec, kt_spec, head_spec),
        compiler_params=_cparams(("parallel",)),
        name="mla_proj",
    )(proj, proj, cqn, ckvn, wuq, wuqr, wk, wv, ind, qg, qgr, kg, kgr, rope_c, rope_s)


MASK_PAD, MASK_PAD_CAUSAL, MASK_CAUSAL = 0, 1, 2


def _attn_bias():
    r = np.arange(ROW_TILE)[:, None]
    c = np.arange(KV_TILE)[None, :]
    keep = np.stack([np.broadcast_to(c >= PAD, (ROW_TILE, KV_TILE)), (c >= PAD) & (c <= r), c <= r])
    return jnp.asarray(np.where(keep, 0.0, NEG_INF), dtype=f32)


def _attn_kernel(q_ref, k_ref, v_ref, bias_ref, o_ref, m_sc, acc_sc, s_sc):
    qi = pl.program_id(2)
    rep = KV_TILE // LANES
    heads = range(ATTN_GROUP)

    def scores(t):
        start = pl.multiple_of(t * KV_TILE, KV_TILE)
        return [jnp.dot(q_ref[0, hh], k_ref[0, hh, :, pl.ds(start, KV_TILE)], preferred_element_type=f32)
                for hh in heads]

    def absorb(t, s_all, mask):
        start = pl.multiple_of(t * KV_TILE, KV_TILE)
        for hh in heads:
            s = s_all[hh]
            if mask is not None:
                s = s + bias_ref[mask]
            m_prev = m_sc[hh]
            m_next = jnp.maximum(m_prev, jnp.max(s, axis=-1, keepdims=True))
            alpha = jnp.exp2(m_prev - m_next)
            p = jnp.exp2(s - jnp.tile(m_next, (1, rep))).astype(bf16)
            v = v_ref[0, hh, pl.ds(start, KV_TILE), :]
            acc_sc[hh] = alpha * acc_sc[hh] + jnp.dot(p, v, preferred_element_type=f32)
            m_sc[hh] = m_next

    def stash(s_all):
        for hh in heads:
            s_sc[hh] = s_all[hh]

    def step(t, mask):
        cur = [s_sc[hh] for hh in heads]
        stash(scores(t + 1))
        absorb(t, cur, mask)

    m_sc[...] = jnp.full(m_sc.shape, NEG_INF, f32)
    acc_sc[...] = jnp.zeros(acc_sc.shape, f32)
    first = scores(0)

    @pl.when(qi == 0)
    def _():
        absorb(0, first, MASK_PAD_CAUSAL)

    @pl.when(qi > 0)
    def _():
        stash(first)
        step(0, MASK_PAD)

        n_plain = qi - 1

        def body(u, c):
            step(2 * u + 1, None)
            step(2 * u + 2, None)
            return c
        lax.fori_loop(0, n_plain // 2, body, 0)

        @pl.when(n_plain % 2 == 1)
        def _():
            step(qi - 1, None)
        absorb(qi, [s_sc[hh] for hh in heads], MASK_CAUSAL)

    for pr in range(ATTN_GROUP // 2):
        a0, a1 = acc_sc[2 * pr], acc_sc[2 * pr + 1]
        lane = lax.broadcasted_iota(jnp.int32, a0.shape, 1)
        o_ref[0, pr] = jnp.where(lane < MLA_V, a0 * (1.0 / a0[:, LANES - 1:]), a1 * (1.0 / a1[:, :1])).astype(bf16)


def _attention(q, k, v):
    batch, _, lp, _ = q.shape
    nt = lp // ROW_TILE
    whole = lambda b, g, i: (b, g, 0, 0)
    tile = lambda b, g, i: (b, g, i, 0)
    return pl.pallas_call(
        _attn_kernel,
        out_shape=jax.ShapeDtypeStruct((batch, MLA_HEADS // 2, lp, HEAD_LANES), bf16),
        grid=(batch, MLA_HEADS // ATTN_GROUP, nt),
        in_specs=[
            pl.BlockSpec((1, ATTN_GROUP, ROW_TILE, HEAD_LANES), tile),
            pl.BlockSpec((1, ATTN_GROUP, HEAD_LANES, lp), whole),
            pl.BlockSpec((1, ATTN_GROUP, lp, HEAD_LANES), whole),
            _resident((3, ROW_TILE, KV_TILE)),
        ],
        out_specs=pl.BlockSpec((1, ATTN_GROUP // 2, ROW_TILE, HEAD_LANES), tile),
        scratch_shapes=[pltpu.VMEM((ATTN_GROUP, ROW_TILE, LANES), f32)] * 2
        + [pltpu.VMEM((ATTN_GROUP, ROW_TILE, KV_TILE), f32)],
        compiler_params=_cparams(("parallel", "parallel", "arbitrary")),
        name="mla_attention",
    )(q, k, v, _attn_bias())


def _chunk_cumsum(lg, tril):
    hi = lg.astype(bf16)
    r1 = lg - hi.astype(f32)
    mid = r1.astype(bf16)
    lo = (r1 - mid.astype(f32)).astype(bf16)
    return (jnp.dot(tril, hi, preferred_element_type=f32) + jnp.dot(tril, mid, preferred_element_type=f32)
            + jnp.dot(tril, lo, preferred_element_type=f32))


def _gated_scan(sub_heads, q_sc, k_sc, lg_sc, v_ref, tril_ref, s_sc, o_sc):
    n_groups = q_sc.shape[1] // LANES
    n_chunks = q_sc.shape[0] // CHUNK
    gv = sub_heads * LANES
    dk = LANES // sub_heads
    ci = lax.broadcasted_iota(jnp.int32, (CHUNK, CHUNK), 0)
    cj = lax.broadcasted_iota(jnp.int32, (CHUNK, CHUNK), 1)
    causal = ci >= cj
    lane = lax.broadcasted_iota(jnp.int32, (CHUNK, LANES), 1)
    if sub_heads > 1:
        srow = lax.broadcasted_iota(jnp.int32, (gv, LANES), 0) // LANES
        scol = lax.broadcasted_iota(jnp.int32, (gv, LANES), 1) // dk
        own_block = srow == scol
    tril = tril_ref[...]
    rows = [slice(c * CHUNK, (c + 1) * CHUNK) for c in range(n_chunks)]
    nt_dims = (((1,), (1,)), ((), ()))
    tn_dims = (((0,), (0,)), ((), ()))

    bs = [_chunk_cumsum(lg_sc[r, :], tril) for r in rows]

    qe, ke, kd, qb, decay = [], [], [], [], []
    for r, b in zip(rows, bs):
        q = q_sc[r, :]
        k = k_sc[r, :]
        b_mid = b[CHUNK // 2 - 1:CHUNK // 2, :]
        b_last = b[CHUNK - 1:CHUNK, :]
        qe.append((q * jnp.exp(b - b_mid)).astype(bf16))
        ke.append((k * jnp.exp(b_mid - b)).astype(bf16))
        kd.append((k * jnp.exp(b_last - b)).astype(bf16))
        qb.append((q * jnp.exp(b)).astype(bf16))
        decay.append(jnp.exp(b_last))

    o_intra = [[None] * (n_groups * sub_heads) for _ in rows]
    d_state = [[None] * n_groups for _ in rows]
    for c, r in enumerate(rows):
        v = v_ref[r, :]
        for g in range(n_groups):
            kl = slice(g * LANES, (g + 1) * LANES)
            vg = v[:, g * gv:(g + 1) * gv]
            for s in range(sub_heads):
                qs = qe[c][:, kl]
                if sub_heads > 1:
                    qs = jnp.where(lane // dk == s, qs, jnp.zeros_like(qs))
                a = lax.dot_general(qs, ke[c][:, kl], nt_dims, preferred_element_type=f32)
                a = jnp.where(causal, a, 0.0).astype(bf16)
                o_intra[c][g * sub_heads + s] = jnp.dot(a, vg[:, s * LANES:(s + 1) * LANES],
                                                        preferred_element_type=f32)
            d = lax.dot_general(vg, kd[c][:, kl], tn_dims, preferred_element_type=f32)
            d_state[c][g] = jnp.where(own_block, d, 0.0) if sub_heads > 1 else d

    for g in range(n_groups):
        kl = slice(g * LANES, (g + 1) * LANES)
        st = s_sc[g]
        for c, r in enumerate(rows):
            o_inter = lax.dot_general(qb[c][:, kl], st.astype(bf16), nt_dims, preferred_element_type=f32)
            for s in range(sub_heads):
                head = g * sub_heads + s
                o_sc[r, head * LANES:(head + 1) * LANES] = o_intra[c][head] + o_inter[:, s * LANES:(s + 1) * LANES]
            st = st * decay[c][:, kl] + d_state[c][g]
        s_sc[g] = st


def _gate_and_norm(o_sc, gain_ref, r_ref, out_ref, n_heads):
    for h in range(n_heads):
        sl = slice(h * LANES, (h + 1) * LANES)
        out_ref[:, sl] = (_rms(o_sc[:, sl], gain_ref[...]) * _silu(r_ref[:, sl].astype(f32))).astype(bf16)


def _pad_row_mask(shape):
    row = pl.program_id(1) * ROW_TILE + lax.broadcasted_iota(jnp.int32, shape, 0)
    return row >= PAD


def _gla_kernel(q_ref, k_ref, v_ref, r_ref, gd_ref, wg2_ref, bg_ref, gain_ref, tril_ref, out_ref,
                q_sc, k_sc, b_sc, s_sc, o_sc):
    @pl.when(pl.program_id(1) == 0)
    def _():
        s_sc[...] = jnp.zeros(s_sc.shape, f32)

    x = jnp.dot(gd_ref[...], wg2_ref[...], preferred_element_type=f32) + bg_ref[...]
    log_g = (jnp.minimum(x, 0.0) - jnp.log(1.0 + jnp.exp(-jnp.abs(x)))) * (1.0 / GLA_GATE_NORM)
    b_sc[...] = jnp.where(_pad_row_mask(log_g.shape), log_g, 0.0)
    q_sc[...] = q_ref[...].astype(f32) * (GLA_DK ** -0.5)
    k_sc[...] = k_ref[...].astype(f32)
    _gated_scan(LANES // GLA_DK, q_sc, k_sc, b_sc, v_ref, tril_ref, s_sc, o_sc)
    _gate_and_norm(o_sc, gain_ref, r_ref, out_ref, GLA_HEADS)


def _hgrn_kernel(q_ref, f_ref, v_ref, r_ref, lb_ref, gain_ref, tril_ref, out_ref,
                 q_sc, k_sc, b_sc, s_sc, o_sc):
    @pl.when(pl.program_id(1) == 0)
    def _():
        s_sc[...] = jnp.zeros(s_sc.shape, f32)

    lb = lb_ref[...]
    f = lb + (1.0 - lb) * _sigmoid(f_ref[...].astype(f32))
    b_sc[...] = jnp.where(_pad_row_mask(f.shape), jnp.log(f), 0.0)
    q_sc[...] = _silu(q_ref[...].astype(f32))
    k_sc[...] = 1.0 - f
    _gated_scan(1, q_sc, k_sc, b_sc, v_ref, tril_ref, s_sc, o_sc)
    _gate_and_norm(o_sc, gain_ref, r_ref, out_ref, HGRN_HEADS)


def _seq_col_spec(width, col, nt):
    return pl.BlockSpec((ROW_TILE, width), lambda b, i: (b * nt + i, col // width))


def _gla(proj, wg2, bg, gain, tril, batch, lp):
    nt = lp // ROW_TILE
    return pl.pallas_call(
        _gla_kernel,
        out_shape=jax.ShapeDtypeStruct((batch * lp, GLA_VW), bf16),
        grid=(batch, nt),
        in_specs=[
            _seq_col_spec(GLA_KW, COL_GLA_Q, nt), _seq_col_spec(GLA_KW, COL_GLA_K, nt),
            _seq_col_spec(GLA_VW, COL_GLA_V, nt), _seq_col_spec(GLA_VW, COL_GLA_R, nt),
            _seq_col_spec(LANES, COL_GLA_GD, nt),
            _resident((LANES, GLA_KW)), _resident((1, GLA_KW)), _resident((1, GLA_DV)),
            _resident((CHUNK, CHUNK)),
        ],
        out_specs=pl.BlockSpec((ROW_TILE, GLA_VW), lambda b, i: (b * nt + i, 0)),
        scratch_shapes=[
            pltpu.VMEM((ROW_TILE, GLA_KW), f32), pltpu.VMEM((ROW_TILE, GLA_KW), f32),
            pltpu.VMEM((ROW_TILE, GLA_KW), f32),
            pltpu.VMEM((GLA_KW // LANES, 2 * LANES, LANES), f32),
            pltpu.VMEM((ROW_TILE, GLA_VW), f32),
        ],
        compiler_params=_cparams(("parallel", "arbitrary")),
        name="gla",
    )(proj, proj, proj, proj, proj, wg2, bg, gain, tril)


def _hgrn(proj, lb, gain, tril, batch, lp):
    nt = lp // ROW_TILE
    return pl.pallas_call(
        _hgrn_kernel,
        out_shape=jax.ShapeDtypeStruct((batch * lp, HGRN_VW), bf16),
        grid=(batch, nt),
        in_specs=[
            _seq_col_spec(HGRN_KW, COL_HG_Q, nt), _seq_col_spec(HGRN_KW, COL_HG_F, nt),
            _seq_col_spec(HGRN_VW, COL_HG_I, nt), _seq_col_spec(HGRN_VW, COL_HG_G, nt),
            _resident((1, HGRN_KW)), _resident((1, LANES)), _resident((CHUNK, CHUNK)),
        ],
        out_specs=pl.BlockSpec((ROW_TILE, HGRN_VW), lambda b, i: (b * nt + i, 0)),
        scratch_shapes=[
            pltpu.VMEM((ROW_TILE, HGRN_KW), f32), pltpu.VMEM((ROW_TILE, HGRN_KW), f32),
            pltpu.VMEM((ROW_TILE, HGRN_KW), f32),
            pltpu.VMEM((HGRN_KW // LANES, LANES, LANES), f32),
            pltpu.VMEM((ROW_TILE, HGRN_VW), f32),
        ],
        compiler_params=_cparams(("parallel", "arbitrary")),
        name="hgrn2",
    )(proj, proj, proj, proj, lb, gain, tril)


def _mix_kernel(h_ref, om_ref, og_ref, oh_ref, g0_ref, g1_ref, g2_ref, wm_ref, wg_ref, wh_ref, wo_ref, out_ref):
    om = jnp.concatenate([om_ref[0, j] for j in range(MLA_HEADS // 2)], axis=-1)
    mixed = (_sigmoid(g0_ref[...].astype(f32)) * jnp.dot(om, wm_ref[...], preferred_element_type=f32)
             + _sigmoid(g1_ref[...].astype(f32)) * jnp.dot(og_ref[...], wg_ref[...], preferred_element_type=f32)
             + _sigmoid(g2_ref[...].astype(f32)) * jnp.dot(oh_ref[...], wh_ref[...], preferred_element_type=f32))
    h2 = h_ref[...] + jnp.dot(mixed.astype(bf16), wo_ref[...], preferred_element_type=f32)
    out_ref[...] = jnp.where(_pad_row_mask(h2.shape), h2, 0.0)


def _mix(h, o_mla, o_gla, o_hgrn, proj, wm, wg, wh, wo, batch, lp):
    nt = lp // ROW_TILE
    row = lambda b, i: (b * nt + i, 0)
    return pl.pallas_call(
        _mix_kernel,
        out_shape=jax.ShapeDtypeStruct((batch * lp, D_MODEL), f32),
        grid=(batch, nt),
        in_specs=[
            pl.BlockSpec((ROW_TILE, D_MODEL), row),
            pl.BlockSpec((1, MLA_HEADS // 2, ROW_TILE, HEAD_LANES), lambda b, i: (b, 0, i, 0)),
            pl.BlockSpec((ROW_TILE, GLA_VW), row), pl.BlockSpec((ROW_TILE, HGRN_VW), row),
            _seq_col_spec(D_MODEL, COL_GATES, nt), _seq_col_spec(D_MODEL, COL_GATES + D_MODEL, nt),
            _seq_col_spec(D_MODEL, COL_GATES + 2 * D_MODEL, nt),
            _resident((MLA_HEADS * MLA_V, D_MODEL)), _resident((GLA_VW, D_MODEL)),
            _resident((HGRN_VW, D_MODEL)), _resident((D_MODEL, D_MODEL)),
        ],
        out_specs=pl.BlockSpec((ROW_TILE, D_MODEL), row),
        compiler_params=_cparams(("parallel", "arbitrary")),
        name="mix_out",
    )(h, o_mla, o_gla, o_hgrn, proj, proj, proj, wm, wg, wh, wo)


def _ffn_kernel(h_ref, g_ref, wg_ref, wu_ref, wd_ref, out_ref):
    h = h_ref[...]
    u = _rms(h, g_ref[...]).astype(bf16)
    acc = h
    start = 0
    for width in FF_CHUNKS:
        sl = slice(start, start + width)
        start += width
        gate = jnp.dot(u, wg_ref[:, sl], preferred_element_type=f32)
        up = jnp.dot(u, wu_ref[:, sl], preferred_element_type=f32)
        acc = acc + jnp.dot((_silu(gate) * up).astype(bf16), wd_ref[sl, :], preferred_element_type=f32)
    out_ref[...] = acc


def _ffn(h, gain, wg, wu, wd):
    tp = h.shape[0]
    row = lambda i: (i, 0)
    return pl.pallas_call(
        _ffn_kernel,
        out_shape=jax.ShapeDtypeStruct((tp, D_MODEL), f32),
        grid=(tp // ROW_TILE,),
        in_specs=[
            pl.BlockSpec((ROW_TILE, D_MODEL), row), _resident((1, D_MODEL)),
            _resident((D_MODEL, D_FF)), _resident((D_MODEL, D_FF)), _resident((D_FF, D_MODEL)),
        ],
        out_specs=pl.BlockSpec((ROW_TILE, D_MODEL), row),
        compiler_params=_cparams(("parallel",)),
        name="dense_ffn",
    )(h, gain, wg, wu, wd)


def _split3(x):
    hi = x.astype(bf16)
    lo = (x - hi.astype(f32)).astype(bf16)
    return hi, lo


def _router_kernel(tiles_per_batch, h_ref, g_ref, whi_ref, wlo_ref, b_ref, lower_ref, out_ref, count_ref, run_sc):
    i = pl.program_id(0)

    @pl.when(i == 0)
    def _():
        run_sc[...] = jnp.zeros(run_sc.shape, f32)

    u = _rms(h_ref[...], g_ref[...])
    u_hi, u_lo = _split3(u)
    logits = (jnp.dot(u_hi, whi_ref[...], preferred_element_type=f32)
              + jnp.dot(u_lo, whi_ref[...], preferred_element_type=f32)
              + jnp.dot(u_hi, wlo_ref[...], preferred_element_type=f32)) + b_ref[...]
    lane = lax.broadcasted_iota(jnp.int32, logits.shape, 1)
    logits = jnp.where(lane < N_EXPERTS, logits, -jnp.inf)
    v1 = jnp.max(logits, axis=-1, keepdims=True)
    i1 = jnp.min(jnp.where(logits == v1, lane, LANES), axis=-1, keepdims=True)
    rest = jnp.where(lane == i1, -jnp.inf, logits)
    v2 = jnp.max(rest, axis=-1, keepdims=True)
    i2 = jnp.min(jnp.where(rest == v2, lane, LANES), axis=-1, keepdims=True)
    e = jnp.exp(v2 - v1)
    g1 = 1.0 / (1.0 + e)
    g2 = e * g1

    row = (i % tiles_per_batch) * ROW_TILE + lax.broadcasted_iota(jnp.int32, logits.shape, 0)
    chosen = jnp.where(((lane == i1) | (lane == i2)) & (row >= PAD), 1.0, 0.0)
    before = jnp.dot(lower_ref[...], chosen.astype(bf16), preferred_element_type=f32) + run_sc[...]
    r1 = jnp.sum(jnp.where(lane == i1, before, 0.0), axis=-1, keepdims=True)
    r2 = jnp.sum(jnp.where(lane == i2, before, 0.0), axis=-1, keepdims=True)
    total = run_sc[...] + jnp.sum(chosen, axis=0, keepdims=True)
    run_sc[...] = total
    count_ref[...] = total

    out = jnp.where(lane == 0, g1, jnp.where(lane == 1, g2, 0.0))
    out = jnp.where(lane == 2, i1.astype(f32), jnp.where(lane == 3, i2.astype(f32), out))
    out = jnp.where(lane == 4, r1, jnp.where(lane == 5, r2, out))
    out_ref[...] = out


def _router(h, gain, w_hi, w_lo, b, tiles_per_batch):
    tp = h.shape[0]
    r = np.arange(ROW_TILE)
    lower = jnp.asarray(r[:, None] > r[None, :], dtype=bf16)
    return pl.pallas_call(
        functools.partial(_router_kernel, tiles_per_batch),
        out_shape=(jax.ShapeDtypeStruct((tp, LANES), f32), jax.ShapeDtypeStruct((1, LANES), f32)),
        grid=(tp // ROW_TILE,),
        in_specs=[
            pl.BlockSpec((ROW_TILE, D_MODEL), lambda i: (i, 0)), _resident((1, D_MODEL)),
            _resident((D_MODEL, LANES)), _resident((D_MODEL, LANES)), _resident((1, LANES)),
            _resident((ROW_TILE, ROW_TILE)),
        ],
        out_specs=(pl.BlockSpec((ROW_TILE, LANES), lambda i: (i, 0)), pl.BlockSpec((1, LANES), lambda i: (0, 0))),
        scratch_shapes=[pltpu.VMEM((1, LANES), f32)],
        compiler_params=_cparams(("arbitrary",)),
        name="moe_router",
    )(h, gain, w_hi, w_lo, b, lower)


def _token_copy(src_tok, token, dst_tok, r, sem):
    return pltpu.make_async_copy(src_tok.at[token], dst_tok.at[pl.ds(r * SUBLANES, SUBLANES), :], sem)


def _gather_tokens(idx_ref, src_tok, dst_tok, sem):
    def body(g, carry):
        for j in range(GATHER_UNROLL):
            r = g * GATHER_UNROLL + j
            _token_copy(src_tok, idx_ref[0, 0, r], dst_tok, r, sem).start(priority=j % 2)
        return carry
    lax.fori_loop(0, idx_ref.shape[-1] // GATHER_UNROLL, body, 0)


def _wait_tokens(dst_tok, sem):
    pltpu.make_async_copy(dst_tok, dst_tok, sem).wait()


def _tokens_to_rows(buf_tok):
    n = buf_tok.shape[0] // SUBLANES
    return jnp.concatenate([buf_tok[pl.ds(c, n, stride=SUBLANES), :] for c in range(D_MODEL // LANES)], axis=1)


def _rows_to_tokens(rows, dst_tok):
    for c in range(D_MODEL // LANES):
        dst_tok[pl.ds(c, rows.shape[0], stride=SUBLANES), :] = rows[:, c * LANES:(c + 1) * LANES]


def _expert_kernel(be_ref, nu_ref, tok_ref, tok_next_ref, w_ref, g_ref, h_tok, wg_ref, wu_ref, wd_ref, y_ref,
                   xbuf, sem, x_sc, acc_sc):
    i = pl.program_id(0)
    k = pl.program_id(1)
    n_used = nu_ref[0]
    slot = i % 2

    def normalised_rows():
        return _rms(_tokens_to_rows(xbuf), g_ref[...]).astype(bf16)

    def partial_out():
        x = x_sc[slot]
        gate = jnp.dot(x, wg_ref[...], preferred_element_type=f32)
        up = jnp.dot(x, wu_ref[...], preferred_element_type=f32)
        return jnp.dot((_silu(gate) * up).astype(bf16), wd_ref[...], preferred_element_type=f32)

    @pl.when((k == 0) & (i == 0))
    def _():
        _gather_tokens(tok_ref, h_tok, xbuf, sem)
        _wait_tokens(xbuf, sem)
        x_sc[0] = normalised_rows()

    @pl.when((k == 0) & (i < n_used))
    def _():
        token = tok_next_ref[0, 0, 0]
        for r in range(ROW_TILE):
            if r:
                token = tok_next_ref[0, 0, r] + (token >> 31)
            _token_copy(h_tok, token, xbuf, r, sem).start(priority=r % 2)
        acc_sc[...] = partial_out()

    @pl.when((k == 1) & (i < n_used))
    def _():
        _wait_tokens(xbuf, sem)
        x_sc[1 - slot] = normalised_rows()
        _rows_to_tokens((acc_sc[...] + partial_out()) * w_ref[...], y_ref)

    @pl.when((k == 1) & (i >= n_used))
    def _():
        y_ref[...] = jnp.zeros(y_ref.shape, f32)


def _experts(block_expert, n_used, slot_row, slot_w, gain, h, wg, wu, wd):
    n_blocks = slot_row.shape[0]
    nk = D_EXPERT // EXP_CHUNK

    def live(i, nu):
        return jnp.minimum(i, nu[0] - 1)

    def kk(i, k, nu):
        return jnp.where(i < nu[0], k, nk - 1)

    grid_spec = pltpu.PrefetchScalarGridSpec(
        num_scalar_prefetch=2,
        grid=(n_blocks, nk),
        in_specs=[
            pl.BlockSpec((1, 1, ROW_TILE), lambda i, k, be, nu: (i, 0, 0), memory_space=pltpu.SMEM),
            pl.BlockSpec((1, 1, ROW_TILE), lambda i, k, be, nu: (jnp.minimum(i + 1, n_blocks - 1), 0, 0),
                         memory_space=pltpu.SMEM),
            pl.BlockSpec((ROW_TILE, 1), lambda i, k, be, nu: (i, 0)),
            pl.BlockSpec((1, D_MODEL), lambda i, k, be, nu: (0, 0)),
            pl.BlockSpec(memory_space=pl.ANY),
            pl.BlockSpec((None, D_MODEL, EXP_CHUNK), lambda i, k, be, nu: (be[live(i, nu)], 0, kk(i, k, nu))),
            pl.BlockSpec((None, D_MODEL, EXP_CHUNK), lambda i, k, be, nu: (be[live(i, nu)], 0, kk(i, k, nu))),
            pl.BlockSpec((None, EXP_CHUNK, D_MODEL), lambda i, k, be, nu: (be[live(i, nu)], kk(i, k, nu), 0)),
        ],
        out_specs=pl.BlockSpec((ROW_TILE * SUBLANES, LANES), lambda i, k, be, nu: (i, 0)),
        scratch_shapes=[
            pltpu.VMEM((ROW_TILE * SUBLANES, LANES), f32),
            pltpu.SemaphoreType.DMA(()),
            pltpu.VMEM((2, ROW_TILE, D_MODEL), bf16),
            pltpu.VMEM((ROW_TILE, D_MODEL), f32),
        ],
    )
    assert nk == 2
    h_tok = h.reshape(h.shape[0], SUBLANES, LANES)
    y_tok = pl.pallas_call(
        _expert_kernel,
        out_shape=jax.ShapeDtypeStruct((n_blocks * ROW_TILE * SUBLANES, LANES), f32),
        grid_spec=grid_spec,
        compiler_params=_cparams(("arbitrary", "arbitrary")),
        name="moe_experts",
    )(block_expert, n_used, slot_row, slot_row, slot_w, gain, h_tok, wg, wu, wd)
    return y_tok.reshape(n_blocks * ROW_TILE, SUBLANES, LANES)


def _combine_kernel(rows, tiles_per_batch, batch_stride, first_row, zero_pad_rows,
                    s0_ref, s1_ref, s0_next_ref, s1_next_ref, h_hbm, y_tok, out_ref, ybuf, hbuf, ysems, hsems):
    i = pl.program_id(0)
    slot = i % 2

    def h_copy(tile, sl):
        start = (tile // tiles_per_batch) * batch_stride + first_row + (tile % tiles_per_batch) * rows
        return pltpu.make_async_copy(h_hbm.at[pl.ds(pl.multiple_of(start, LANES), rows), :], hbuf.at[sl],
                                     hsems.at[sl])

    @pl.when(i == 0)
    def _():
        h_copy(0, 0).start()
        _gather_tokens(s0_ref, y_tok, ybuf.at[0, 0], ysems.at[0, 0])
        _gather_tokens(s1_ref, y_tok, ybuf.at[0, 1], ysems.at[0, 1])

    @pl.when(i + 1 < pl.num_programs(0))
    def _():
        h_copy(i + 1, 1 - slot).start()
        _gather_tokens(s0_next_ref, y_tok, ybuf.at[1 - slot, 0], ysems.at[1 - slot, 0])
        _gather_tokens(s1_next_ref, y_tok, ybuf.at[1 - slot, 1], ysems.at[1 - slot, 1])

    h_copy(i, slot).wait()
    _wait_tokens(ybuf.at[slot, 0], ysems.at[slot, 0])
    _wait_tokens(ybuf.at[slot, 1], ysems.at[slot, 1])
    out = hbuf[slot] + _tokens_to_rows(ybuf.at[slot, 0]) + _tokens_to_rows(ybuf.at[slot, 1])
    if zero_pad_rows:
        row = (i % tiles_per_batch) * rows + lax.broadcasted_iota(jnp.int32, out.shape, 0)
        out = jnp.where(row >= PAD, out, 0.0)
    out_ref[...] = out


def _combine(slot_of, h, y_tok, batch, lp, frames_only):
    if frames_only:
        rows, first_row = OUT_TILE, PAD + N_META
    else:
        rows, first_row = ROW_TILE, 0
    assert (lp - first_row) % rows == 0
    per_batch = (lp - first_row) // rows
    n_tiles = batch * per_batch
    slots = slot_of[:, first_row:, :].reshape(n_tiles, 1, rows, TOP_K)
    s0, s1 = slots[..., 0], slots[..., 1]
    idx_spec = pl.BlockSpec((1, 1, rows), lambda i: (i, 0, 0), memory_space=pltpu.SMEM)
    nxt_spec = pl.BlockSpec((1, 1, rows), lambda i: (jnp.minimum(i + 1, n_tiles - 1), 0, 0), memory_space=pltpu.SMEM)
    return pl.pallas_call(
        functools.partial(_combine_kernel, rows, per_batch, lp, first_row, not frames_only),
        out_shape=jax.ShapeDtypeStruct((n_tiles * rows, D_MODEL), f32),
        grid=(n_tiles,),
        in_specs=[idx_spec, idx_spec, nxt_spec, nxt_spec, pl.BlockSpec(memory_space=pl.ANY),
                  pl.BlockSpec(memory_space=pl.ANY)],
        out_specs=pl.BlockSpec((rows, D_MODEL), lambda i: (i, 0)),
        scratch_shapes=[pltpu.VMEM((2, 2, rows * SUBLANES, LANES), f32), pltpu.VMEM((2, rows, D_MODEL), f32),
                        pltpu.SemaphoreType.DMA((2, 2)), pltpu.SemaphoreType.DMA((2,))],
        compiler_params=_cparams(("arbitrary",)),
        name="moe_combine",
    )(s0, s1, s0, s1, h, y_tok)


def _moe(h, gain, w_router, b_router, wg, wu, wd, batch, lp, frames_only):
    seq = lp - PAD
    n_tok = batch * seq
    n_assign = n_tok * TOP_K
    w_r = jnp.pad(w_router, ((0, 0), (0, LANES - N_EXPERTS)))
    w_hi = w_r.astype(bf16)
    w_lo = (w_r - w_hi.astype(f32)).astype(bf16)
    b_r = jnp.pad(b_router, (0, LANES - N_EXPERTS)).reshape(1, LANES)
    routed, count_row = _router(h, gain, w_hi, w_lo, b_r, lp // ROW_TILE)
    routed = routed.reshape(batch, lp, LANES)
    gate = routed[:, PAD:, 0:TOP_K].reshape(n_assign)
    expert_p = routed[:, :, TOP_K:2 * TOP_K].astype(jnp.int32)
    rank_p = routed[:, :, 2 * TOP_K:3 * TOP_K].astype(jnp.int32)
    counts = count_row[0, :N_EXPERTS].astype(jnp.int32)

    e_flat = expert_p[:, PAD:].reshape(n_assign)
    idx_bits = max(1, (n_assign - 1).bit_length())
    packed = jnp.sort(e_flat * (1 << idx_bits) + jnp.arange(n_assign, dtype=jnp.int32))
    order = packed & ((1 << idx_bits) - 1)
    start = jnp.cumsum(counts) - counts
    padded = (counts + ROW_TILE - 1) // ROW_TILE * ROW_TILE
    pend = jnp.cumsum(padded)
    pstart = pend - padded
    n_blocks = -(-n_assign // ROW_TILE) + N_EXPERTS
    n_slots = n_blocks * ROW_TILE

    block_first = jnp.arange(n_blocks, dtype=jnp.int32) * ROW_TILE
    block_expert = jnp.minimum(jnp.sum((block_first[:, None] >= pend[None, :]).astype(jnp.int32), axis=1),
                               N_EXPERTS - 1)
    within = (block_first - pstart[block_expert])[:, None] + jnp.arange(ROW_TILE, dtype=jnp.int32)[None, :]
    live = (within < counts[block_expert][:, None]) & (block_first < pend[-1])[:, None]
    assign = order[jnp.clip(start[block_expert][:, None] + within, 0, n_assign - 1)]
    tok = assign // TOP_K
    slot_row = jnp.where(live, (tok // seq) * lp + PAD + tok % seq, 0)
    slot_w = jnp.where(live, gate[assign], 0.0)
    n_used = (pend[-1] // ROW_TILE).astype(jnp.int32).reshape(1)

    y = _experts(block_expert, n_used, slot_row.reshape(n_blocks, 1, ROW_TILE), slot_w.reshape(n_slots, 1),
                 gain, h, wg, wu, wd)

    first_slot = jnp.zeros(expert_p.shape, jnp.int32)
    for e in range(N_EXPERTS):
        first_slot = jnp.where(expert_p == e, pstart[e], first_slot)
    is_pad = (jnp.arange(lp) < PAD)[None, :, None]
    slot_of = jnp.where(is_pad, 0, first_slot + rank_p)
    return _combine(slot_of, h, y, batch, lp, frames_only)


def _pad_cols(w, width):
    return jnp.pad(w, ((0, 0), (0, width - w.shape[1])))


def _in_proj_weight(w):
    offs = np.cumsum((MLA_Q_LORA, MLA_KV_LORA, MLA_ROPE, GLA_KW, GLA_KW, GLA_VW, GLA_GATE_RANK, GLA_VW,
                      HGRN_KW, HGRN_KW, HGRN_VW, HGRN_VW, N_BRANCH * D_MODEL))[:-1]
    (dq, dkv, kr, gq, gk, gv, ggd, gr, hq, hf, hi, hg, gates) = jnp.split(w, offs, axis=1)
    kr_p = jnp.pad(kr, ((0, 0), (MLA_NOPE, HEAD_LANES - MLA_NOPE - MLA_ROPE)))
    parts = [dq, dkv, kr_p, gq, gk, gv, gr, hq, hf, hi, hg, gates, _pad_cols(ggd, LANES), _rotate_half_cols(kr_p)]
    return jnp.concatenate(parts, axis=1).astype(bf16)


def _rotate_half_cols(w, sign=-1.0):
    lead = w.shape[0]
    w3 = w.reshape(lead, -1, HEAD_LANES)
    half = MLA_ROPE // 2
    first, second = w3[:, :, MLA_NOPE:MLA_NOPE + half], w3[:, :, MLA_NOPE + half:MLA_QK]
    zero = lambda n: jnp.zeros(w3.shape[:2] + (n,), w.dtype)
    return jnp.concatenate([zero(MLA_NOPE), sign * second, first, zero(HEAD_LANES - MLA_QK)],
                           axis=-1).reshape(lead, -1)


def _mla_weights(w_uq, w_ukv):
    d_in = w_uq.shape[0]
    wq = w_uq.reshape(d_in, MLA_HEADS, MLA_QK)
    wq = jnp.pad(wq, ((0, 0), (0, 0), (0, HEAD_LANES - MLA_QK))).reshape(d_in, MLA_HEADS * HEAD_LANES)
    d_kv = w_ukv.shape[0]
    wkv = w_ukv.reshape(d_kv, MLA_HEADS, MLA_NOPE + MLA_V)
    wk = jnp.pad(wkv[:, :, :MLA_NOPE], ((0, 0), (0, 0), (0, HEAD_LANES - MLA_NOPE)))
    wv = wkv[:, :, MLA_NOPE:]
    zeros = jnp.zeros_like(wv)
    even = (jnp.arange(MLA_HEADS) % 2 == 0)[None, :, None]
    wv = jnp.concatenate([jnp.where(even, wv, zeros), jnp.where(even, zeros, wv)], axis=-1)
    return (wq.astype(bf16), wk.reshape(d_kv, -1).astype(bf16), wv.reshape(d_kv, -1).astype(bf16))


def _head_gain(g, scale):
    return (jnp.pad(g, (0, HEAD_LANES - MLA_QK)) * scale).reshape(1, HEAD_LANES).astype(f32)


def _rope_tables(lp):
    half = MLA_ROPE // 2
    pos = np.maximum(np.arange(lp) - PAD, 0).astype(np.float32)
    inv = (ROPE_THETA ** (-np.arange(half, dtype=np.float32) / half)).astype(np.float32)
    ang = jnp.asarray(pos)[:, None] * jnp.asarray(inv)[None, :]
    cos, sin = jnp.cos(ang), jnp.sin(ang)
    ones = jnp.ones((lp, MLA_NOPE), f32)
    z = lambda n: jnp.zeros((lp, n), f32)
    tail = HEAD_LANES - MLA_QK
    c = jnp.concatenate([ones, cos, cos, jnp.ones((lp, tail), f32)], axis=1)
    s = jnp.concatenate([z(MLA_NOPE), sin, sin, z(tail)], axis=1)
    return c, s


def _head_indicator():
    lane_head = np.arange(2 * HEAD_LANES) // HEAD_LANES
    return jnp.asarray(lane_head[:, None] == lane_head[None, :], dtype=bf16)


def _chunk_tril():
    r = np.arange(CHUNK)
    return jnp.asarray(r[:, None] >= r[None, :], dtype=bf16)


def kernel(x, meta_tokens, attn_norm, w_in, mla_cq_norm, w_mla_uq, mla_ckv_norm, w_mla_ukv, mla_q_norm, mla_k_norm, w_gla_g2, b_gla_g, gla_o_norm, hgrn_lower_bounds, hgrn_o_norm, w_mla_o, w_gla_o, w_hgrn_o, w_out, ffn_norm, w_ff_gate, w_ff_up, w_ff_down, w_router, b_router, w_exp_gate, w_exp_up, w_exp_down):
    batch, seq, _ = x.shape
    depth = w_in.shape[0]
    lp = PAD + N_META + seq
    assert lp % ROW_TILE == 0 and ROW_TILE % CHUNK == 0 and KV_TILE == ROW_TILE

    meta = jnp.broadcast_to(meta_tokens[None].astype(x.dtype), (batch, N_META, D_MODEL))
    h = jnp.concatenate([jnp.zeros((batch, PAD, D_MODEL), x.dtype), meta, x], axis=1).reshape(batch * lp, D_MODEL)

    lbs = jax.nn.softmax(hgrn_lower_bounds.astype(f32), axis=0)
    lbs = jnp.cumsum(lbs, axis=0) - lbs[0]
    rope_c, rope_s = _rope_tables(lp)
    head_ind = _head_indicator()
    tril = _chunk_tril()
    row1 = lambda v: v.reshape(1, -1).astype(f32)

    for layer in range(depth):
        proj = _in_proj(h, row1(attn_norm[layer]), _in_proj_weight(w_in[layer]))
        wq, wk, wv = _mla_weights(w_mla_uq[layer], w_mla_ukv[layer])
        q_gain = _head_gain(mla_q_norm[layer], MLA_QK ** -0.5 * LOG2_E)
        k_gain = _head_gain(mla_k_norm[layer], 1.0)
        q, k, v = _mla_proj(proj, row1(mla_cq_norm[layer]), row1(mla_ckv_norm[layer]),
                            wq, _rotate_half_cols(wq), wk, wv, head_ind,
                            q_gain, _rotate_half_cols(q_gain, 1.0), k_gain, _rotate_half_cols(k_gain, 1.0),
                            rope_c, rope_s, batch, lp)
        o_mla = _attention(q, k, v)
        wg2 = jnp.pad(w_gla_g2[layer], ((0, LANES - GLA_GATE_RANK), (0, 0))).astype(bf16)
        o_gla = _gla(proj, wg2, row1(b_gla_g[layer]), row1(gla_o_norm[layer]), tril, batch, lp)
        o_hgrn = _hgrn(proj, row1(lbs[layer]), row1(hgrn_o_norm[layer]), tril, batch, lp)
        h = _mix(h, o_mla, o_gla, o_hgrn, proj, w_mla_o[layer].astype(bf16), w_gla_o[layer].astype(bf16),
                 w_hgrn_o[layer].astype(bf16), w_out[layer].astype(bf16), batch, lp)
        i = layer // 2
        if layer % 2 == 0:
            h = _ffn(h, row1(ffn_norm[layer]), w_ff_gate[i].astype(bf16), w_ff_up[i].astype(bf16),
                     w_ff_down[i].astype(bf16))
        else:
            h = _moe(h, row1(ffn_norm[layer]), w_router[i], b_router[i], w_exp_gate[i].astype(bf16),
                     w_exp_up[i].astype(bf16), w_exp_down[i].astype(bf16), batch, lp, layer == depth - 1)
    if depth % 2 == 0:
        return h.reshape(batch, seq, D_MODEL)
    return h.reshape(batch, lp, D_MODEL)[:, PAD + N_META:]
```

```python
import functools

import numpy as np
import jax
import jax.numpy as jnp
from jax import lax
from jax.experimental import pallas as pl
from jax.experimental.pallas import tpu as pltpu

f32 = jnp.float32
bf16 = jnp.bfloat16

D_MODEL = 1024
N_META = 16
PAD = 112
CHUNK = 64
EPS = 1e-6
NEG_INF = -1e30
LOG2_E = 1.4426950408889634
MLA_HEADS = 8
MLA_Q_LORA = 256
MLA_KV_LORA = 128
MLA_NOPE = 64
MLA_ROPE = 32
MLA_V = 64
MLA_QK = MLA_NOPE + MLA_ROPE
ROPE_THETA = 10000.0
GLA_HEADS = 4
GLA_DK = 64
GLA_DV = 128
GLA_KW = GLA_HEADS * GLA_DK
GLA_VW = GLA_HEADS * GLA_DV
GLA_GATE_RANK = 16
GLA_GATE_NORM = 16.0
HGRN_HEADS = 4
HGRN_DK = 128
HGRN_KW = 512
HGRN_VW = 512
N_BRANCH = 3
D_FF = 2816
N_EXPERTS = 8
TOP_K = 2
D_EXPERT = 3584

LANES = 128
SUBLANES = 8
HEAD_LANES = 128
VMEM_LIMIT = 56 * 1024 * 1024

ROW_TILE = 640
KV_TILE = 640
ATTN_GROUP = 2
FF_CHUNKS = (1280, 1536)
EXP_CHUNK = 1792
GATHER_UNROLL = 8
OUT_TILE = 512

COL_MLA = 0
COL_GLA_Q = 512
COL_GLA_K = 768
COL_GLA_V = 1024
COL_GLA_R = 1536
COL_HG_Q = 2048
COL_HG_F = 2560
COL_HG_I = 3072
COL_HG_G = 3584
COL_GATES = 4096
COL_GLA_GD = 7168
COL_KR_ROT = 7296
IN_COLS_P = 7424
IN_COL_TILE = 3712


def _cparams(sem):
    return pltpu.CompilerParams(dimension_semantics=sem, vmem_limit_bytes=VMEM_LIMIT)


def _resident(shape):
    nd = len(shape)
    return pl.BlockSpec(shape, lambda *_: (0,) * nd, pipeline_mode=pl.Buffered(1))


def _silu(x):
    return x * (1.0 / (1.0 + jnp.exp(-x)))


def _sigmoid(x):
    return 1.0 / (1.0 + jnp.exp(-x))


def _rms(x, gain):
    ms = jnp.mean(x * x, axis=-1, keepdims=True)
    return x * lax.rsqrt(ms + EPS) * gain


def _in_proj_kernel(h_ref, g_ref, w_ref, o_ref):
    u = _rms(h_ref[...], g_ref[...]).astype(bf16)
    for j in range(IN_COLS_P // IN_COL_TILE):
        sl = slice(j * IN_COL_TILE, (j + 1) * IN_COL_TILE)
        o_ref[:, sl] = jnp.dot(u, w_ref[:, sl], preferred_element_type=f32).astype(bf16)


def _in_proj(h, gain, w):
    tp = h.shape[0]
    return pl.pallas_call(
        _in_proj_kernel,
        out_shape=jax.ShapeDtypeStruct((tp, IN_COLS_P), bf16),
        grid=(tp // ROW_TILE,),
        in_specs=[
            pl.BlockSpec((ROW_TILE, D_MODEL), lambda i: (i, 0)),
            _resident((1, D_MODEL)),
            _resident((D_MODEL, IN_COLS_P)),
        ],
        out_specs=pl.BlockSpec((ROW_TILE, IN_COLS_P), lambda i: (i, 0)),
        compiler_params=_cparams(("parallel",)),
        name="in_proj",
    )(h, gain, w)


def _head_mean_square(x, ind_ref):
    sq = (x * x).astype(bf16)
    w = ind_ref.shape[0]
    sums = [jnp.dot(sq[:, g * w:(g + 1) * w], ind_ref[...], preferred_element_type=f32)
            for g in range(x.shape[1] // w)]
    return jnp.concatenate(sums, axis=1) * (1.0 / MLA_QK)


def _mla_proj_kernel(p_ref, krr_ref, cqn_ref, ckvn_ref, wuq_ref, wuqr_ref, wk_ref, wv_ref, ind_ref,
                     qg_ref, qgr_ref, kg_ref, kgr_ref, c_ref, s_ref, q_ref, k_ref, v_ref):
    p = p_ref[...].astype(f32)
    cq = _rms(p[:, :MLA_Q_LORA], cqn_ref[...]).astype(bf16)
    ckv = _rms(p[:, MLA_Q_LORA:MLA_Q_LORA + MLA_KV_LORA], ckvn_ref[...]).astype(bf16)
    k_rope = p[:, MLA_Q_LORA + MLA_KV_LORA:]
    q_all = jnp.dot(cq, wuq_ref[...], preferred_element_type=f32)
    q_rot = jnp.dot(cq, wuqr_ref[...], preferred_element_type=f32)
    k_all = jnp.dot(ckv, wk_ref[...], preferred_element_type=f32) + jnp.tile(k_rope, (1, MLA_HEADS))
    v_all = jnp.dot(ckv, wv_ref[...], preferred_element_type=f32)
    q_scale = lax.rsqrt(_head_mean_square(q_all, ind_ref) + EPS)
    k_scale = lax.rsqrt(_head_mean_square(k_all, ind_ref) + EPS)
    c, s = c_ref[...], s_ref[...]
    q_c, q_s = qg_ref[...] * c, qgr_ref[...] * s
    k_c, k_s = kg_ref[...] * c, kgr_ref[...] * s
    k_rot = krr_ref[...].astype(f32) * k_s
    lane = lax.broadcasted_iota(jnp.int32, c.shape, 1)
    for h in range(MLA_HEADS):
        sl = slice(h * HEAD_LANES, (h + 1) * HEAD_LANES)
        q_ref[0, h] = (q_scale[:, sl] * (q_all[:, sl] * q_c + q_rot[:, sl] * q_s)).T.astype(bf16)
        k_ref[0, h] = (k_scale[:, sl] * (k_all[:, sl] * k_c + k_rot)).astype(bf16)
        one_lane = LANES - 1 if h % 2 == 0 else 0
        v_ref[0, h] = jnp.where(lane == one_lane, 1.0, v_all[:, sl]).T.astype(bf16)


def _mla_proj(proj, cqn, ckvn, wuq, wuqr, wk, wv, ind, qg, qgr, kg, kgr, rope_c, rope_s, batch, lp):
    nt = lp // ROW_TILE
    hl = MLA_HEADS * HEAD_LANES
    head_spec = pl.BlockSpec((1, MLA_HEADS, ROW_TILE, HEAD_LANES), lambda i: (i // nt, 0, i % nt, 0))
    tab_spec = pl.BlockSpec((ROW_TILE, HEAD_LANES), lambda i: (i % nt, 0))
    gain_spec = _resident((1, HEAD_LANES))
    shp = jax.ShapeDtypeStruct((batch, MLA_HEADS, lp, HEAD_LANES), bf16)
    kt_shp = jax.ShapeDtypeStruct((batch, MLA_HEADS, HEAD_LANES, lp), bf16)
    kt_spec = pl.BlockSpec((1, MLA_HEADS, HEAD_LANES, ROW_TILE), lambda i: (i // nt, 0, 0, i % nt))
    return pl.pallas_call(
        _mla_proj_kernel,
        out_shape=(kt_shp, shp, kt_shp),
        grid=(batch * nt,),
        in_specs=[
            pl.BlockSpec((ROW_TILE, 512), lambda i: (i, COL_MLA // 512)),
            pl.BlockSpec((ROW_TILE, LANES), lambda i: (i, COL_KR_ROT // LANES)),
            _resident((1, MLA_Q_LORA)), _resident((1, MLA_KV_LORA)),
            _resident((MLA_Q_LORA, hl)), _resident((MLA_Q_LORA, hl)),
            _resident((MLA_KV_LORA, hl)), _resident((MLA_KV_LORA, hl)),
            _resident((2 * HEAD_LANES, 2 * HEAD_LANES)),
            gain_spec, gain_spec, gain_spec, gain_spec,
            tab_spec, tab_spec,
        ],
        out_specs=(kt_spec, head_spec, kt_spec),
        compiler_params=_cparams(("parallel",)),
        name="mla_proj",
    )(proj, proj, cqn, ckvn, wuq, wuqr, wk, wv, ind, qg, qgr, kg, kgr, rope_c, rope_s)


MASK_PAD, MASK_PAD_CAUSAL, MASK_CAUSAL = 0, 1, 2


def _attn_bias():
    k = np.arange(KV_TILE)[:, None]
    q = np.arange(ROW_TILE)[None, :]
    keep = np.stack([np.broadcast_to(k >= PAD, (KV_TILE, ROW_TILE)), (k >= PAD) & (k <= q), k <= q])
    return jnp.asarray(np.where(keep, 0.0, NEG_INF), dtype=f32)


def _attn_kernel(qt_ref, k_ref, vt_ref, bias_ref, o_ref, m_sc, acc_sc, s_sc):
    qi = pl.program_id(2)
    heads = range(ATTN_GROUP)

    def scores(t):
        start = pl.multiple_of(t * KV_TILE, KV_TILE)
        return [jnp.dot(k_ref[0, hh, pl.ds(start, KV_TILE), :], qt_ref[0, hh], preferred_element_type=f32)
                for hh in heads]

    def absorb(t, s_all, mask):
        start = pl.multiple_of(t * KV_TILE, KV_TILE)
        for hh in heads:
            s = s_all[hh]
            if mask is not None:
                s = s + bias_ref[mask]
            m_prev = m_sc[hh]
            m_next = jnp.maximum(m_prev, jnp.max(s, axis=0, keepdims=True))
            alpha = jnp.exp2(m_prev - m_next)
            p = jnp.exp2(s - m_next).astype(bf16)
            vt = vt_ref[0, hh, :, pl.ds(start, KV_TILE)]
            acc_sc[hh] = alpha * acc_sc[hh] + jnp.dot(vt, p, preferred_element_type=f32)
            m_sc[hh] = m_next

    def stash(s_all):
        for hh in heads:
            s_sc[hh] = s_all[hh]

    def step(t, mask):
        cur = [s_sc[hh] for hh in heads]
        stash(scores(t + 1))
        absorb(t, cur, mask)

    m_sc[...] = jnp.full(m_sc.shape, NEG_INF, f32)
    acc_sc[...] = jnp.zeros(acc_sc.shape, f32)
    first = scores(0)

    @pl.when(qi == 0)
    def _():
        absorb(0, first, MASK_PAD_CAUSAL)

    @pl.when(qi > 0)
    def _():
        stash(first)
        step(0, MASK_PAD)

        n_plain = qi - 1

        def body(u, c):
            step(2 * u + 1, None)
            step(2 * u + 2, None)
            return c
        lax.fori_loop(0, n_plain // 2, body, 0)

        @pl.when(n_plain % 2 == 1)
        def _():
            step(qi - 1, None)
        absorb(qi, [s_sc[hh] for hh in heads], MASK_CAUSAL)

    for pr in range(ATTN_GROUP // 2):
        a0, a1 = acc_sc[2 * pr], acc_sc[2 * pr + 1]
        row = lax.broadcasted_iota(jnp.int32, a0.shape, 0)
        pair = jnp.where(row < MLA_V, a0 * (1.0 / a0[LANES - 1:, :]), a1 * (1.0 / a1[:1, :]))
        o_ref[0, pr] = pair.T.astype(bf16)


def _attention(qt, k, vt):
    batch, _, lp, _ = k.shape
    nt = lp // ROW_TILE
    whole = lambda b, g, i: (b, g, 0, 0)
    return pl.pallas_call(
        _attn_kernel,
        out_shape=jax.ShapeDtypeStruct((batch, MLA_HEADS // 2, lp, HEAD_LANES), bf16),
        grid=(batch, MLA_HEADS // ATTN_GROUP, nt),
        in_specs=[
            pl.BlockSpec((1, ATTN_GROUP, HEAD_LANES, ROW_TILE), lambda b, g, i: (b, g, 0, i)),
            pl.BlockSpec((1, ATTN_GROUP, lp, HEAD_LANES), whole),
            pl.BlockSpec((1, ATTN_GROUP, HEAD_LANES, lp), whole),
            _resident((3, KV_TILE, ROW_TILE)),
        ],
        out_specs=pl.BlockSpec((1, ATTN_GROUP // 2, ROW_TILE, HEAD_LANES), lambda b, g, i: (b, g, i, 0)),
        scratch_shapes=[pltpu.VMEM((ATTN_GROUP, 1, ROW_TILE), f32), pltpu.VMEM((ATTN_GROUP, HEAD_LANES, ROW_TILE), f32),
                        pltpu.VMEM((ATTN_GROUP, KV_TILE, ROW_TILE), f32)],
        compiler_params=_cparams(("parallel", "parallel", "arbitrary")),
        name="mla_attention",
    )(qt, k, vt, _attn_bias())


def _chunk_cumsum(lg, tril):
    hi = lg.astype(bf16)
    r1 = lg - hi.astype(f32)
    mid = r1.astype(bf16)
    lo = (r1 - mid.astype(f32)).astype(bf16)
    return (jnp.dot(tril, hi, preferred_element_type=f32) + jnp.dot(tril, mid, preferred_element_type=f32)
            + jnp.dot(tril, lo, preferred_element_type=f32))


def _gated_scan(sub_heads, q_sc, k_sc, lg_sc, v_ref, tril_ref, s_sc, o_sc):
    n_groups = q_sc.shape[1] // LANES
    n_chunks = q_sc.shape[0] // CHUNK
    gv = sub_heads * LANES
    dk = LANES // sub_heads
    ci = lax.broadcasted_iota(jnp.int32, (CHUNK, CHUNK), 0)
    cj = lax.broadcasted_iota(jnp.int32, (CHUNK, CHUNK), 1)
    causal = ci >= cj
    lane = lax.broadcasted_iota(jnp.int32, (CHUNK, LANES), 1)
    if sub_heads > 1:
        srow = lax.broadcasted_iota(jnp.int32, (gv, LANES), 0) // LANES
        scol = lax.broadcasted_iota(jnp.int32, (gv, LANES), 1) // dk
        own_block = srow == scol
    tril = tril_ref[...]
    rows = [slice(c * CHUNK, (c + 1) * CHUNK) for c in range(n_chunks)]
    nt_dims = (((1,), (1,)), ((), ()))
    tn_dims = (((0,), (0,)), ((), ()))

    bs = [_chunk_cumsum(lg_sc[r, :], tril) for r in rows]

    qe, ke, kd, qb, decay = [], [], [], [], []
    for r, b in zip(rows, bs):
        q = q_sc[r, :]
        k = k_sc[r, :]
        b_mid = b[CHUNK // 2 - 1:CHUNK // 2, :]
        b_last = b[CHUNK - 1:CHUNK, :]
        qe.append((q * jnp.exp(b - b_mid)).astype(bf16))
        ke.append((k * jnp.exp(b_mid - b)).astype(bf16))
        kd.append((k * jnp.exp(b_last - b)).astype(bf16))
        qb.append((q * jnp.exp(b)).astype(bf16))
        decay.append(jnp.exp(b_last))

    o_intra = [[None] * (n_groups * sub_heads) for _ in rows]
    d_state = [[None] * n_groups for _ in rows]
    for c, r in enumerate(rows):
        v = v_ref[r, :]
        for g in range(n_groups):
            kl = slice(g * LANES, (g + 1) * LANES)
            vg = v[:, g * gv:(g + 1) * gv]
            for s in range(sub_heads):
                qs = qe[c][:, kl]
                if sub_heads > 1:
                    qs = jnp.where(lane // dk == s, qs, jnp.zeros_like(qs))
                a = lax.dot_general(qs, ke[c][:, kl], nt_dims, preferred_element_type=f32)
                a = jnp.where(causal, a, 0.0).astype(bf16)
                o_intra[c][g * sub_heads + s] = jnp.dot(a, vg[:, s * LANES:(s + 1) * LANES],
                                                        preferred_element_type=f32)
            d = lax.dot_general(vg, kd[c][:, kl], tn_dims, preferred_element_type=f32)
            d_state[c][g] = jnp.where(own_block, d, 0.0) if sub_heads > 1 else d

    for g in range(n_groups):
        kl = slice(g * LANES, (g + 1) * LANES)
        st = s_sc[g]
        for c, r in enumerate(rows):
            o_inter = lax.dot_general(qb[c][:, kl], st.astype(bf16), nt_dims, preferred_element_type=f32)
            for s in range(sub_heads):
                head = g * sub_heads + s
                o_sc[r, head * LANES:(head + 1) * LANES] = o_intra[c][head] + o_inter[:, s * LANES:(s + 1) * LANES]
            st = st * decay[c][:, kl] + d_state[c][g]
        s_sc[g] = st


def _gate_and_norm(o_sc, gain_ref, r_ref, out_ref, n_heads):
    for h in range(n_heads):
        sl = slice(h * LANES, (h + 1) * LANES)
        out_ref[:, sl] = (_rms(o_sc[:, sl], gain_ref[...]) * _silu(r_ref[:, sl].astype(f32))).astype(bf16)


def _pad_row_mask(shape):
    row = pl.program_id(1) * ROW_TILE + lax.broadcasted_iota(jnp.int32, shape, 0)
    return row >= PAD


def _gla_kernel(q_ref, k_ref, v_ref, r_ref, gd_ref, wg2_ref, bg_ref, gain_ref, tril_ref, out_ref,
                q_sc, k_sc, b_sc, s_sc, o_sc):
    @pl.when(pl.program_id(1) == 0)
    def _():
        s_sc[...] = jnp.zeros(s_sc.shape, f32)

    x = jnp.dot(gd_ref[...], wg2_ref[...], preferred_element_type=f32) + bg_ref[...]
    log_g = (jnp.minimum(x, 0.0) - jnp.log(1.0 + jnp.exp(-jnp.abs(x)))) * (1.0 / GLA_GATE_NORM)
    b_sc[...] = jnp.where(_pad_row_mask(log_g.shape), log_g, 0.0)
    q_sc[...] = q_ref[...].astype(f32) * (GLA_DK ** -0.5)
    k_sc[...] = k_ref[...].astype(f32)
    _gated_scan(LANES // GLA_DK, q_sc, k_sc, b_sc, v_ref, tril_ref, s_sc, o_sc)
    _gate_and_norm(o_sc, gain_ref, r_ref, out_ref, GLA_HEADS)


def _hgrn_kernel(q_ref, f_ref, v_ref, r_ref, lb_ref, gain_ref, tril_ref, out_ref,
                 q_sc, k_sc, b_sc, s_sc, o_sc):
    @pl.when(pl.program_id(1) == 0)
    def _():
        s_sc[...] = jnp.zeros(s_sc.shape, f32)

    lb = lb_ref[...]
    f = lb + (1.0 - lb) * _sigmoid(f_ref[...].astype(f32))
    b_sc[...] = jnp.where(_pad_row_mask(f.shape), jnp.log(f), 0.0)
    q_sc[...] = _silu(q_ref[...].astype(f32))
    k_sc[...] = 1.0 - f
    _gated_scan(1, q_sc, k_sc, b_sc, v_ref, tril_ref, s_sc, o_sc)
    _gate_and_norm(o_sc, gain_ref, r_ref, out_ref, HGRN_HEADS)


def _seq_col_spec(width, col, nt):
    return pl.BlockSpec((ROW_TILE, width), lambda b, i: (b * nt + i, col // width))


def _gla(proj, wg2, bg, gain, tril, batch, lp):
    nt = lp // ROW_TILE
    return pl.pallas_call(
        _gla_kernel,
        out_shape=jax.ShapeDtypeStruct((batch * lp, GLA_VW), bf16),
        grid=(batch, nt),
        in_specs=[
            _seq_col_spec(GLA_KW, COL_GLA_Q, nt), _seq_col_spec(GLA_KW, COL_GLA_K, nt),
            _seq_col_spec(GLA_VW, COL_GLA_V, nt), _seq_col_spec(GLA_VW, COL_GLA_R, nt),
            _seq_col_spec(LANES, COL_GLA_GD, nt),
            _resident((LANES, GLA_KW)), _resident((1, GLA_KW)), _resident((1, GLA_DV)),
            _resident((CHUNK, CHUNK)),
        ],
        out_specs=pl.BlockSpec((ROW_TILE, GLA_VW), lambda b, i: (b * nt + i, 0)),
        scratch_shapes=[
            pltpu.VMEM((ROW_TILE, GLA_KW), f32), pltpu.VMEM((ROW_TILE, GLA_KW), f32),
            pltpu.VMEM((ROW_TILE, GLA_KW), f32),
            pltpu.VMEM((GLA_KW // LANES, 2 * LANES, LANES), f32),
            pltpu.VMEM((ROW_TILE, GLA_VW), f32),
        ],
        compiler_params=_cparams(("parallel", "arbitrary")),
        name="gla",
    )(proj, proj, proj, proj, proj, wg2, bg, gain, tril)


def _hgrn(proj, lb, gain, tril, batch, lp):
    nt = lp // ROW_TILE
    return pl.pallas_call(
        _hgrn_kernel,
        out_shape=jax.ShapeDtypeStruct((batch * lp, HGRN_VW), bf16),
        grid=(batch, nt),
        in_specs=[
            _seq_col_spec(HGRN_KW, COL_HG_Q, nt), _seq_col_spec(HGRN_KW, COL_HG_F, nt),
            _seq_col_spec(HGRN_VW, COL_HG_I, nt), _seq_col_spec(HGRN_VW, COL_HG_G, nt),
            _resident((1, HGRN_KW)), _resident((1, LANES)), _resident((CHUNK, CHUNK)),
        ],
        out_specs=pl.BlockSpec((ROW_TILE, HGRN_VW), lambda b, i: (b * nt + i, 0)),
        scratch_shapes=[
            pltpu.VMEM((ROW_TILE, HGRN_KW), f32), pltpu.VMEM((ROW_TILE, HGRN_KW), f32),
            pltpu.VMEM((ROW_TILE, HGRN_KW), f32),
            pltpu.VMEM((HGRN_KW // LANES, LANES, LANES), f32),
            pltpu.VMEM((ROW_TILE, HGRN_VW), f32),
        ],
        compiler_params=_cparams(("parallel", "arbitrary")),
        name="hgrn2",
    )(proj, proj, proj, proj, lb, gain, tril)


def _mix_kernel(h_ref, om_ref, og_ref, oh_ref, g0_ref, g1_ref, g2_ref, wm_ref, wg_ref, wh_ref, wo_ref, out_ref):
    om = jnp.concatenate([om_ref[0, j] for j in range(MLA_HEADS // 2)], axis=-1)
    mixed = (_sigmoid(g0_ref[...].astype(f32)) * jnp.dot(om, wm_ref[...], preferred_element_type=f32)
             + _sigmoid(g1_ref[...].astype(f32)) * jnp.dot(og_ref[...], wg_ref[...], preferred_element_type=f32)
             + _sigmoid(g2_ref[...].astype(f32)) * jnp.dot(oh_ref[...], wh_ref[...], preferred_element_type=f32))
    h2 = h_ref[...] + jnp.dot(mixed.astype(bf16), wo_ref[...], preferred_element_type=f32)
    out_ref[...] = jnp.where(_pad_row_mask(h2.shape), h2, 0.0)


def _mix(h, o_mla, o_gla, o_hgrn, proj, wm, wg, wh, wo, batch, lp):
    nt = lp // ROW_TILE
    row = lambda b, i: (b * nt + i, 0)
    return pl.pallas_call(
        _mix_kernel,
        out_shape=jax.ShapeDtypeStruct((batch * lp, D_MODEL), f32),
        grid=(batch, nt),
        in_specs=[
            pl.BlockSpec((ROW_TILE, D_MODEL), row),
            pl.BlockSpec((1, MLA_HEADS // 2, ROW_TILE, HEAD_LANES), lambda b, i: (b, 0, i, 0)),
            pl.BlockSpec((ROW_TILE, GLA_VW), row), pl.BlockSpec((ROW_TILE, HGRN_VW), row),
            _seq_col_spec(D_MODEL, COL_GATES, nt), _seq_col_spec(D_MODEL, COL_GATES + D_MODEL, nt),
            _seq_col_spec(D_MODEL, COL_GATES + 2 * D_MODEL, nt),
            _resident((MLA_HEADS * MLA_V, D_MODEL)), _resident((GLA_VW, D_MODEL)),
            _resident((HGRN_VW, D_MODEL)), _resident((D_MODEL, D_MODEL)),
        ],
        out_specs=pl.BlockSpec((ROW_TILE, D_MODEL), row),
        compiler_params=_cparams(("parallel", "arbitrary")),
        name="mix_out",
    )(h, o_mla, o_gla, o_hgrn, proj, proj, proj, wm, wg, wh, wo)


def _ffn_kernel(h_ref, g_ref, wg_ref, wu_ref, wd_ref, out_ref):
    h = h_ref[...]
    u = _rms(h, g_ref[...]).astype(bf16)
    acc = h
    start = 0
    for width in FF_CHUNKS:
        sl = slice(start, start + width)
        start += width
        gate = jnp.dot(u, wg_ref[:, sl], preferred_element_type=f32)
        up = jnp.dot(u, wu_ref[:, sl], preferred_element_type=f32)
        acc = acc + jnp.dot((_silu(gate) * up).astype(bf16), wd_ref[sl, :], preferred_element_type=f32)
    out_ref[...] = acc


def _ffn(h, gain, wg, wu, wd):
    tp = h.shape[0]
    row = lambda i: (i, 0)
    return pl.pallas_call(
        _ffn_kernel,
        out_shape=jax.ShapeDtypeStruct((tp, D_MODEL), f32),
        grid=(tp // ROW_TILE,),
        in_specs=[
            pl.BlockSpec((ROW_TILE, D_MODEL), row), _resident((1, D_MODEL)),
            _resident((D_MODEL, D_FF)), _resident((D_MODEL, D_FF)), _resident((D_FF, D_MODEL)),
        ],
        out_specs=pl.BlockSpec((ROW_TILE, D_MODEL), row),
        compiler_params=_cparams(("parallel",)),
        name="dense_ffn",
    )(h, gain, wg, wu, wd)


def _split3(x):
    hi = x.astype(bf16)
    lo = (x - hi.astype(f32)).astype(bf16)
    return hi, lo


def _router_kernel(tiles_per_batch, h_ref, g_ref, whi_ref, wlo_ref, b_ref, lower_ref, out_ref, count_ref, run_sc):
    i = pl.program_id(0)

    @pl.when(i == 0)
    def _():
        run_sc[...] = jnp.zeros(run_sc.shape, f32)

    u = _rms(h_ref[...], g_ref[...])
    u_hi, u_lo = _split3(u)
    logits = (jnp.dot(u_hi, whi_ref[...], preferred_element_type=f32)
              + jnp.dot(u_lo, whi_ref[...], preferred_element_type=f32)
              + jnp.dot(u_hi, wlo_ref[...], preferred_element_type=f32)) + b_ref[...]
    lane = lax.broadcasted_iota(jnp.int32, logits.shape, 1)
    logits = jnp.where(lane < N_EXPERTS, logits, -jnp.inf)
    v1 = jnp.max(logits, axis=-1, keepdims=True)
    i1 = jnp.min(jnp.where(logits == v1, lane, LANES), axis=-1, keepdims=True)
    rest = jnp.where(lane == i1, -jnp.inf, logits)
    v2 = jnp.max(rest, axis=-1, keepdims=True)
    i2 = jnp.min(jnp.where(rest == v2, lane, LANES), axis=-1, keepdims=True)
    e = jnp.exp(v2 - v1)
    g1 = 1.0 / (1.0 + e)
    g2 = e * g1

    row = (i % tiles_per_batch) * ROW_TILE + lax.broadcasted_iota(jnp.int32, logits.shape, 0)
    chosen = jnp.where(((lane == i1) | (lane == i2)) & (row >= PAD), 1.0, 0.0)
    before = jnp.dot(lower_ref[...], chosen.astype(bf16), preferred_element_type=f32) + run_sc[...]
    r1 = jnp.sum(jnp.where(lane == i1, before, 0.0), axis=-1, keepdims=True)
    r2 = jnp.sum(jnp.where(lane == i2, before, 0.0), axis=-1, keepdims=True)
    total = run_sc[...] + jnp.sum(chosen, axis=0, keepdims=True)
    run_sc[...] = total
    count_ref[...] = total

    out = jnp.where(lane == 0, g1, jnp.where(lane == 1, g2, 0.0))
    out = jnp.where(lane == 2, i1.astype(f32), jnp.where(lane == 3, i2.astype(f32), out))
    out = jnp.where(lane == 4, r1, jnp.where(lane == 5, r2, out))
    out_ref[...] = out


def _router(h, gain, w_hi, w_lo, b, tiles_per_batch):
    tp = h.shape[0]
    r = np.arange(ROW_TILE)
    lower = jnp.asarray(r[:, None] > r[None, :], dtype=bf16)
    return pl.pallas_call(
        functools.partial(_router_kernel, tiles_per_batch),
        out_shape=(jax.ShapeDtypeStruct((tp, LANES), f32), jax.ShapeDtypeStruct((1, LANES), f32)),
        grid=(tp // ROW_TILE,),
        in_specs=[
            pl.BlockSpec((ROW_TILE, D_MODEL), lambda i: (i, 0)), _resident((1, D_MODEL)),
            _resident((D_MODEL, LANES)), _resident((D_MODEL, LANES)), _resident((1, LANES)),
            _resident((ROW_TILE, ROW_TILE)),
        ],
        out_specs=(pl.BlockSpec((ROW_TILE, LANES), lambda i: (i, 0)), pl.BlockSpec((1, LANES), lambda i: (0, 0))),
        scratch_shapes=[pltpu.VMEM((1, LANES), f32)],
        compiler_params=_cparams(("arbitrary",)),
        name="moe_router",
    )(h, gain, w_hi, w_lo, b, lower)


def _token_copy(src_tok, token, dst_tok, r, sem):
    return pltpu.make_async_copy(src_tok.at[token], dst_tok.at[pl.ds(r * SUBLANES, SUBLANES), :], sem)


def _gather_tokens(idx_ref, src_tok, dst_tok, sem):
    def body(g, carry):
        for j in range(GATHER_UNROLL):
            r = g * GATHER_UNROLL + j
            _token_copy(src_tok, idx_ref[0, 0, r], dst_tok, r, sem).start(priority=j % 2)
        return carry
    lax.fori_loop(0, idx_ref.shape[-1] // GATHER_UNROLL, body, 0)


def _wait_tokens(dst_tok, sem):
    pltpu.make_async_copy(dst_tok, dst_tok, sem).wait()


def _tokens_to_rows(buf_tok):
    n = buf_tok.shape[0] // SUBLANES
    return jnp.concatenate([buf_tok[pl.ds(c, n, stride=SUBLANES), :] for c in range(D_MODEL // LANES)], axis=1)


def _rows_to_tokens(rows, dst_tok):
    for c in range(D_MODEL // LANES):
        dst_tok[pl.ds(c, rows.shape[0], stride=SUBLANES), :] = rows[:, c * LANES:(c + 1) * LANES]


def _expert_kernel(be_ref, nu_ref, tok_ref, tok_next_ref, w_ref, g_ref, h_tok, wg_ref, wu_ref, wd_ref, y_ref,
                   xbuf, sem, x_sc, acc_sc):
    i = pl.program_id(0)
    k = pl.program_id(1)
    n_used = nu_ref[0]
    slot = i % 2

    def normalised_rows():
        return _rms(_tokens_to_rows(xbuf), g_ref[...]).astype(bf16)

    def partial_out():
        x = x_sc[slot]
        gate = jnp.dot(x, wg_ref[...], preferred_element_type=f32)
        up = jnp.dot(x, wu_ref[...], preferred_element_type=f32)
        return jnp.dot((_silu(gate) * up).astype(bf16), wd_ref[...], preferred_element_type=f32)

    @pl.when((k == 0) & (i == 0))
    def _():
        _gather_tokens(tok_ref, h_tok, xbuf, sem)
        _wait_tokens(xbuf, sem)
        x_sc[0] = normalised_rows()

    @pl.when((k == 0) & (i < n_used))
    def _():
        token = tok_next_ref[0, 0, 0]
        for r in range(ROW_TILE):
            if r:
                token = tok_next_ref[0, 0, r] + (token >> 31)
            _token_copy(h_tok, token, xbuf, r, sem).start(priority=r % 2)
        acc_sc[...] = partial_out()

    @pl.when((k == 1) & (i < n_used))
    def _():
        _wait_tokens(xbuf, sem)
        x_sc[1 - slot] = normalised_rows()
        _rows_to_tokens((acc_sc[...] + partial_out()) * w_ref[...], y_ref)

    @pl.when((k == 1) & (i >= n_used))
    def _():
        y_ref[...] = jnp.zeros(y_ref.shape, f32)


def _experts(block_expert, n_used, slot_row, slot_w, gain, h, wg, wu, wd):
    n_blocks = slot_row.shape[0]
    nk = D_EXPERT // EXP_CHUNK

    def live(i, nu):
        return jnp.minimum(i, nu[0] - 1)

    def kk(i, k, nu):
        return jnp.where(i < nu[0], k, nk - 1)

    grid_spec = pltpu.PrefetchScalarGridSpec(
        num_scalar_prefetch=2,
        grid=(n_blocks, nk),
        in_specs=[
            pl.BlockSpec((1, 1, ROW_TILE), lambda i, k, be, nu: (i, 0, 0), memory_space=pltpu.SMEM),
            pl.BlockSpec((1, 1, ROW_TILE), lambda i, k, be, nu: (jnp.minimum(i + 1, n_blocks - 1), 0, 0),
                         memory_space=pltpu.SMEM),
            pl.BlockSpec((ROW_TILE, 1), lambda i, k, be, nu: (i, 0)),
            pl.BlockSpec((1, D_MODEL), lambda i, k, be, nu: (0, 0)),
            pl.BlockSpec(memory_space=pl.ANY),
            pl.BlockSpec((None, D_MODEL, EXP_CHUNK), lambda i, k, be, nu: (be[live(i, nu)], 0, kk(i, k, nu))),
            pl.BlockSpec((None, D_MODEL, EXP_CHUNK), lambda i, k, be, nu: (be[live(i, nu)], 0, kk(i, k, nu))),
            pl.BlockSpec((None, EXP_CHUNK, D_MODEL), lambda i, k, be, nu: (be[live(i, nu)], kk(i, k, nu), 0)),
        ],
        out_specs=pl.BlockSpec((ROW_TILE * SUBLANES, LANES), lambda i, k, be, nu: (i, 0)),
        scratch_shapes=[
            pltpu.VMEM((ROW_TILE * SUBLANES, LANES), f32),
            pltpu.SemaphoreType.DMA(()),
            pltpu.VMEM((2, ROW_TILE, D_MODEL), bf16),
            pltpu.VMEM((ROW_TILE, D_MODEL), f32),
        ],
    )
    assert nk == 2
    h_tok = h.reshape(h.shape[0], SUBLANES, LANES)
    y_tok = pl.pallas_call(
        _expert_kernel,
        out_shape=jax.ShapeDtypeStruct((n_blocks * ROW_TILE * SUBLANES, LANES), f32),
        grid_spec=grid_spec,
        compiler_params=_cparams(("arbitrary", "arbitrary")),
        name="moe_experts",
    )(block_expert, n_used, slot_row, slot_row, slot_w, gain, h_tok, wg, wu, wd)
    return y_tok.reshape(n_blocks * ROW_TILE, SUBLANES, LANES)


def _combine_kernel(rows, tiles_per_batch, batch_stride, first_row, zero_pad_rows,
                    s0_ref, s1_ref, s0_next_ref, s1_next_ref, h_hbm, y_tok, out_ref, ybuf, hbuf, ysems, hsems):
    i = pl.program_id(0)
    slot = i % 2

    def h_copy(tile, sl):
        start = (tile // tiles_per_batch) * batch_stride + first_row + (tile % tiles_per_batch) * rows
        return pltpu.make_async_copy(h_hbm.at[pl.ds(pl.multiple_of(start, LANES), rows), :], hbuf.at[sl],
                                     hsems.at[sl])

    @pl.when(i == 0)
    def _():
        h_copy(0, 0).start()
        _gather_tokens(s0_ref, y_tok, ybuf.at[0, 0], ysems.at[0, 0])
        _gather_tokens(s1_ref, y_tok, ybuf.at[0, 1], ysems.at[0, 1])

    @pl.when(i + 1 < pl.num_programs(0))
    def _():
        h_copy(i + 1, 1 - slot).start()
        _gather_tokens(s0_next_ref, y_tok, ybuf.at[1 - slot, 0], ysems.at[1 - slot, 0])
        _gather_tokens(s1_next_ref, y_tok, ybuf.at[1 - slot, 1], ysems.at[1 - slot, 1])

    h_copy(i, slot).wait()
    _wait_tokens(ybuf.at[slot, 0], ysems.at[slot, 0])
    _wait_tokens(ybuf.at[slot, 1], ysems.at[slot, 1])
    out = hbuf[slot] + _tokens_to_rows(ybuf.at[slot, 0]) + _tokens_to_rows(ybuf.at[slot, 1])
    if zero_pad_rows:
        row = (i % tiles_per_batch) * rows + lax.broadcasted_iota(jnp.int32, out.shape, 0)
        out = jnp.where(row >= PAD, out, 0.0)
    out_ref[...] = out


def _combine(slot_of, h, y_tok, batch, lp, frames_only):
    if frames_only:
        rows, first_row = OUT_TILE, PAD + N_META
    else:
        rows, first_row = ROW_TILE, 0
    assert (lp - first_row) % rows == 0
    per_batch = (lp - first_row) // rows
    n_tiles = batch * per_batch
    slots = slot_of[:, first_row:, :].reshape(n_tiles, 1, rows, TOP_K)
    s0, s1 = slots[..., 0], slots[..., 1]
    idx_spec = pl.BlockSpec((1, 1, rows), lambda i: (i, 0, 0), memory_space=pltpu.SMEM)
    nxt_spec = pl.BlockSpec((1, 1, rows), lambda i: (jnp.minimum(i + 1, n_tiles - 1), 0, 0), memory_space=pltpu.SMEM)
    return pl.pallas_call(
        functools.partial(_combine_kernel, rows, per_batch, lp, first_row, not frames_only),
        out_shape=jax.ShapeDtypeStruct((n_tiles * rows, D_MODEL), f32),
        grid=(n_tiles,),
        in_specs=[idx_spec, idx_spec, nxt_spec, nxt_spec, pl.BlockSpec(memory_space=pl.ANY),
                  pl.BlockSpec(memory_space=pl.ANY)],
        out_specs=pl.BlockSpec((rows, D_MODEL), lambda i: (i, 0)),
        scratch_shapes=[pltpu.VMEM((2, 2, rows * SUBLANES, LANES), f32), pltpu.VMEM((2, rows, D_MODEL), f32),
                        pltpu.SemaphoreType.DMA((2, 2)), pltpu.SemaphoreType.DMA((2,))],
        compiler_params=_cparams(("arbitrary",)),
        name="moe_combine",
    )(s0, s1, s0, s1, h, y_tok)


def _moe(h, gain, w_router, b_router, wg, wu, wd, batch, lp, frames_only):
    seq = lp - PAD
    n_tok = batch * seq
    n_assign = n_tok * TOP_K
    w_r = jnp.pad(w_router, ((0, 0), (0, LANES - N_EXPERTS)))
    w_hi = w_r.astype(bf16)
    w_lo = (w_r - w_hi.astype(f32)).astype(bf16)
    b_r = jnp.pad(b_router, (0, LANES - N_EXPERTS)).reshape(1, LANES)
    routed, count_row = _router(h, gain, w_hi, w_lo, b_r, lp // ROW_TILE)
    routed = routed.reshape(batch, lp, LANES)
    gate = routed[:, PAD:, 0:TOP_K].reshape(n_assign)
    expert_p = routed[:, :, TOP_K:2 * TOP_K].astype(jnp.int32)
    rank_p = routed[:, :, 2 * TOP_K:3 * TOP_K].astype(jnp.int32)
    counts = count_row[0, :N_EXPERTS].astype(jnp.int32)

    e_flat = expert_p[:, PAD:].reshape(n_assign)
    idx_bits = max(1, (n_assign - 1).bit_length())
    packed = jnp.sort(e_flat * (1 << idx_bits) + jnp.arange(n_assign, dtype=jnp.int32))
    order = packed & ((1 << idx_bits) - 1)
    start = jnp.cumsum(counts) - counts
    padded = (counts + ROW_TILE - 1) // ROW_TILE * ROW_TILE
    pend = jnp.cumsum(padded)
    pstart = pend - padded
    n_blocks = -(-n_assign // ROW_TILE) + N_EXPERTS
    n_slots = n_blocks * ROW_TILE

    block_first = jnp.arange(n_blocks, dtype=jnp.int32) * ROW_TILE
    block_expert = jnp.minimum(jnp.sum((block_first[:, None] >= pend[None, :]).astype(jnp.int32), axis=1),
                               N_EXPERTS - 1)
    within = (block_first - pstart[block_expert])[:, None] + jnp.arange(ROW_TILE, dtype=jnp.int32)[None, :]
    live = (within < counts[block_expert][:, None]) & (block_first < pend[-1])[:, None]
    assign = order[jnp.clip(start[block_expert][:, None] + within, 0, n_assign - 1)]
    tok = assign // TOP_K
    slot_row = jnp.where(live, (tok // seq) * lp + PAD + tok % seq, 0)
    slot_w = jnp.where(live, gate[assign], 0.0)
    n_used = (pend[-1] // ROW_TILE).astype(jnp.int32).reshape(1)

    y = _experts(block_expert, n_used, slot_row.reshape(n_blocks, 1, ROW_TILE), slot_w.reshape(n_slots, 1),
                 gain, h, wg, wu, wd)

    first_slot = jnp.zeros(expert_p.shape, jnp.int32)
    for e in range(N_EXPERTS):
        first_slot = jnp.where(expert_p == e, pstart[e], first_slot)
    is_pad = (jnp.arange(lp) < PAD)[None, :, None]
    slot_of = jnp.where(is_pad, 0, first_slot + rank_p)
    return _combine(slot_of, h, y, batch, lp, frames_only)


def _pad_cols(w, width):
    return jnp.pad(w, ((0, 0), (0, width - w.shape[1])))


def _in_proj_weight(w):
    offs = np.cumsum((MLA_Q_LORA, MLA_KV_LORA, MLA_ROPE, GLA_KW, GLA_KW, GLA_VW, GLA_GATE_RANK, GLA_VW,
                      HGRN_KW, HGRN_KW, HGRN_VW, HGRN_VW, N_BRANCH * D_MODEL))[:-1]
    (dq, dkv, kr, gq, gk, gv, ggd, gr, hq, hf, hi, hg, gates) = jnp.split(w, offs, axis=1)
    kr_p = jnp.pad(kr, ((0, 0), (MLA_NOPE, HEAD_LANES - MLA_NOPE - MLA_ROPE)))
    parts = [dq, dkv, kr_p, gq, gk, gv, gr, hq, hf, hi, hg, gates, _pad_cols(ggd, LANES), _rotate_half_cols(kr_p)]
    return jnp.concatenate(parts, axis=1).astype(bf16)


def _rotate_half_cols(w, sign=-1.0):
    lead = w.shape[0]
    w3 = w.reshape(lead, -1, HEAD_LANES)
    half = MLA_ROPE // 2
    first, second = w3[:, :, MLA_NOPE:MLA_NOPE + half], w3[:, :, MLA_NOPE + half:MLA_QK]
    zero = lambda n: jnp.zeros(w3.shape[:2] + (n,), w.dtype)
    return jnp.concatenate([zero(MLA_NOPE), sign * second, first, zero(HEAD_LANES - MLA_QK)],
                           axis=-1).reshape(lead, -1)


def _mla_weights(w_uq, w_ukv):
    d_in = w_uq.shape[0]
    wq = w_uq.reshape(d_in, MLA_HEADS, MLA_QK)
    wq = jnp.pad(wq, ((0, 0), (0, 0), (0, HEAD_LANES - MLA_QK))).reshape(d_in, MLA_HEADS * HEAD_LANES)
    d_kv = w_ukv.shape[0]
    wkv = w_ukv.reshape(d_kv, MLA_HEADS, MLA_NOPE + MLA_V)
    wk = jnp.pad(wkv[:, :, :MLA_NOPE], ((0, 0), (0, 0), (0, HEAD_LANES - MLA_NOPE)))
    wv = wkv[:, :, MLA_NOPE:]
    zeros = jnp.zeros_like(wv)
    even = (jnp.arange(MLA_HEADS) % 2 == 0)[None, :, None]
    wv = jnp.concatenate([jnp.where(even, wv, zeros), jnp.where(even, zeros, wv)], axis=-1)
    return (wq.astype(bf16), wk.reshape(d_kv, -1).astype(bf16), wv.reshape(d_kv, -1).astype(bf16))


def _head_gain(g, scale):
    return (jnp.pad(g, (0, HEAD_LANES - MLA_QK)) * scale).reshape(1, HEAD_LANES).astype(f32)


def _rope_tables(lp):
    half = MLA_ROPE // 2
    pos = np.maximum(np.arange(lp) - PAD, 0).astype(np.float32)
    inv = (ROPE_THETA ** (-np.arange(half, dtype=np.float32) / half)).astype(np.float32)
    ang = jnp.asarray(pos)[:, None] * jnp.asarray(inv)[None, :]
    cos, sin = jnp.cos(ang), jnp.sin(ang)
    ones = jnp.ones((lp, MLA_NOPE), f32)
    z = lambda n: jnp.zeros((lp, n), f32)
    tail = HEAD_LANES - MLA_QK
    c = jnp.concatenate([ones, cos, cos, jnp.ones((lp, tail), f32)], axis=1)
    s = jnp.concatenate([z(MLA_NOPE), sin, sin, z(tail)], axis=1)
    return c, s


def _head_indicator():
    lane_head = np.arange(2 * HEAD_LANES) // HEAD_LANES
    return jnp.asarray(lane_head[:, None] == lane_head[None, :], dtype=bf16)


def _chunk_tril():
    r = np.arange(CHUNK)
    return jnp.asarray(r[:, None] >= r[None, :], dtype=bf16)


def kernel(x, meta_tokens, attn_norm, w_in, mla_cq_norm, w_mla_uq, mla_ckv_norm, w_mla_ukv, mla_q_norm, mla_k_norm, w_gla_g2, b_gla_g, gla_o_norm, hgrn_lower_bounds, hgrn_o_norm, w_mla_o, w_gla_o, w_hgrn_o, w_out, ffn_norm, w_ff_gate, w_ff_up, w_ff_down, w_router, b_router, w_exp_gate, w_exp_up, w_exp_down):
    batch, seq, _ = x.shape
    depth = w_in.shape[0]
    lp = PAD + N_META + seq
    assert lp % ROW_TILE == 0 and ROW_TILE % CHUNK == 0 and KV_TILE == ROW_TILE

    meta = jnp.broadcast_to(meta_tokens[None].astype(x.dtype), (batch, N_META, D_MODEL))
    h = jnp.concatenate([jnp.zeros((batch, PAD, D_MODEL), x.dtype), meta, x], axis=1).reshape(batch * lp, D_MODEL)

    lbs = jax.nn.softmax(hgrn_lower_bounds.astype(f32), axis=0)
    lbs = jnp.cumsum(lbs, axis=0) - lbs[0]
    rope_c, rope_s = _rope_tables(lp)
    head_ind = _head_indicator()
    tril = _chunk_tril()
    row1 = lambda v: v.reshape(1, -1).astype(f32)

    for layer in range(depth):
        proj = _in_proj(h, row1(attn_norm[layer]), _in_proj_weight(w_in[layer]))
        wq, wk, wv = _mla_weights(w_mla_uq[layer], w_mla_ukv[layer])
        q_gain = _head_gain(mla_q_norm[layer], MLA_QK ** -0.5 * LOG2_E)
        k_gain = _head_gain(mla_k_norm[layer], 1.0)
        q, k, v = _mla_proj(proj, row1(mla_cq_norm[layer]), row1(mla_ckv_norm[layer]),
                            wq, _rotate_half_cols(wq), wk, wv, head_ind,
                            q_gain, _rotate_half_cols(q_gain, 1.0), k_gain, _rotate_half_cols(k_gain, 1.0),
                            rope_c, rope_s, batch, lp)
        o_mla = _attention(q, k, v)
        wg2 = jnp.pad(w_gla_g2[layer], ((0, LANES - GLA_GATE_RANK), (0, 0))).astype(bf16)
        o_gla = _gla(proj, wg2, row1(b_gla_g[layer]), row1(gla_o_norm[layer]), tril, batch, lp)
        o_hgrn = _hgrn(proj, row1(lbs[layer]), row1(hgrn_o_norm[layer]), tril, batch, lp)
        h = _mix(h, o_mla, o_gla, o_hgrn, proj, w_mla_o[layer].astype(bf16), w_gla_o[layer].astype(bf16),
                 w_hgrn_o[layer].astype(bf16), w_out[layer].astype(bf16), batch, lp)
        i = layer // 2
        if layer % 2 == 0:
            h = _ffn(h, row1(ffn_norm[layer]), w_ff_gate[i].astype(bf16), w_ff_up[i].astype(bf16),
                     w_ff_down[i].astype(bf16))
        else:
            h = _moe(h, row1(ffn_norm[layer]), w_router[i], b_router[i], w_exp_gate[i].astype(bf16),
                     w_exp_up[i].astype(bf16), w_exp_down[i].astype(bf16), batch, lp, layer == depth - 1)
    if depth % 2 == 0:
        return h.reshape(batch, seq, D_MODEL)
    return h.reshape(batch, lp, D_MODEL)[:, PAD + N_META:]
```

```python
import functools

import numpy as np
import jax
import jax.numpy as jnp
from jax import lax
from jax.experimental import pallas as pl
from jax.experimental.pallas import tpu as pltpu

f32 = jnp.float32
bf16 = jnp.bfloat16

D_MODEL = 1024
N_META = 16
PAD = 112
CHUNK = 64
EPS = 1e-6
NEG_INF = -1e30
LOG2_E = 1.4426950408889634
MLA_HEADS = 8
MLA_Q_LORA = 256
MLA_KV_LORA = 128
MLA_NOPE = 64
MLA_ROPE = 32
MLA_V = 64
MLA_QK = MLA_NOPE + MLA_ROPE
ROPE_THETA = 10000.0
GLA_HEADS = 4
GLA_DK = 64
GLA_DV = 128
GLA_KW = GLA_HEADS * GLA_DK
GLA_VW = GLA_HEADS * GLA_DV
GLA_GATE_RANK = 16
GLA_GATE_NORM = 16.0
HGRN_HEADS = 4
HGRN_DK = 128
HGRN_KW = 512
HGRN_VW = 512
N_BRANCH = 3
D_FF = 2816
N_EXPERTS = 8
TOP_K = 2
D_EXPERT = 3584

LANES = 128
SUBLANES = 8
HEAD_LANES = 128
VMEM_LIMIT = 56 * 1024 * 1024

ROW_TILE = 640
KV_TILE = 640
ATTN_GROUP = 2
FF_CHUNKS = (1280, 1536)
EXP_CHUNK = 1792
GATHER_UNROLL = 8
OUT_TILE = 512

COL_MLA = 0
COL_GLA_Q = 512
COL_GLA_K = 768
COL_GLA_V = 1024
COL_GLA_R = 1536
COL_HG_Q = 2048
COL_HG_F = 2560
COL_HG_I = 3072
COL_HG_G = 3584
COL_GATES = 4096
COL_GLA_GD = 7168
COL_KR_ROT = 7296
IN_COLS_P = 7424
IN_COL_TILE = 3712


def _cparams(sem):
    return pltpu.CompilerParams(dimension_semantics=sem, vmem_limit_bytes=VMEM_LIMIT)


def _resident(shape):
    nd = len(shape)
    return pl.BlockSpec(shape, lambda *_: (0,) * nd, pipeline_mode=pl.Buffered(1))


def _silu(x):
    return x * (1.0 / (1.0 + jnp.exp(-x)))


def _sigmoid(x):
    return 1.0 / (1.0 + jnp.exp(-x))


def _rms(x, gain):
    ms = jnp.mean(x * x, axis=-1, keepdims=True)
    return x * lax.rsqrt(ms + EPS) * gain


def _in_proj_kernel(h_ref, g_ref, w_ref, o_ref):
    u = _rms(h_ref[...], g_ref[...]).astype(bf16)
    for j in range(IN_COLS_P // IN_COL_TILE):
        sl = slice(j * IN_COL_TILE, (j + 1) * IN_COL_TILE)
        o_ref[:, sl] = jnp.dot(u, w_ref[:, sl], preferred_element_type=f32).astype(bf16)


def _in_proj(h, gain, w):
    tp = h.shape[0]
    return pl.pallas_call(
        _in_proj_kernel,
        out_shape=jax.ShapeDtypeStruct((tp, IN_COLS_P), bf16),
        grid=(tp // ROW_TILE,),
        in_specs=[
            pl.BlockSpec((ROW_TILE, D_MODEL), lambda i: (i, 0)),
            _resident((1, D_MODEL)),
            _resident((D_MODEL, IN_COLS_P)),
        ],
        out_specs=pl.BlockSpec((ROW_TILE, IN_COLS_P), lambda i: (i, 0)),
        compiler_params=_cparams(("parallel",)),
        name="in_proj",
    )(h, gain, w)


def _head_mean_square(x, ind_ref):
    sq = (x * x).astype(bf16)
    w = ind_ref.shape[0]
    sums = [jnp.dot(sq[:, g * w:(g + 1) * w], ind_ref[...], preferred_element_type=f32)
            for g in range(x.shape[1] // w)]
    return jnp.concatenate(sums, axis=1) * (1.0 / MLA_QK)


def _mla_proj_kernel(p_ref, krr_ref, cqn_ref, ckvn_ref, wuqt_ref, wuqrt_ref, wk_ref, wvt_ref, ind_ref,
                     qg_ref, qgr_ref, kg_ref, kgr_ref, c_ref, s_ref, ct_ref, st_ref, q_ref, k_ref, v_ref):
    p = p_ref[...].astype(f32)
    cq = _rms(p[:, :MLA_Q_LORA], cqn_ref[...])
    ckv = _rms(p[:, MLA_Q_LORA:MLA_Q_LORA + MLA_KV_LORA], ckvn_ref[...])
    k_rope = p[:, MLA_Q_LORA + MLA_KV_LORA:]
    cq_t = cq.T.astype(bf16)
    ckv_t = ckv.T.astype(bf16)
    q_all = jnp.dot(wuqt_ref[...], cq_t, preferred_element_type=f32)
    q_rot = jnp.dot(wuqrt_ref[...], cq_t, preferred_element_type=f32)
    v_all = jnp.dot(wvt_ref[...], ckv_t, preferred_element_type=f32)
    k_all = jnp.dot(ckv.astype(bf16), wk_ref[...], preferred_element_type=f32) + jnp.tile(k_rope, (1, MLA_HEADS))
    k_scale = lax.rsqrt(_head_mean_square(k_all, ind_ref) + EPS)
    k_c, k_s = kg_ref[...] * c_ref[...], kgr_ref[...] * s_ref[...]
    k_rot = krr_ref[...].astype(f32) * k_s
    q_c, q_s = qg_ref[...] * ct_ref[...], qgr_ref[...] * st_ref[...]
    dim = lax.broadcasted_iota(jnp.int32, q_c.shape, 0)
    for h in range(MLA_HEADS):
        sl = slice(h * HEAD_LANES, (h + 1) * HEAD_LANES)
        qh = q_all[sl, :]
        q_scale = lax.rsqrt(jnp.sum(qh * qh, axis=0, keepdims=True) * (1.0 / MLA_QK) + EPS)
        q_ref[0, h] = (q_scale * (qh * q_c + q_rot[sl, :] * q_s)).astype(bf16)
        k_ref[0, h] = (k_scale[:, sl] * (k_all[:, sl] * k_c + k_rot)).astype(bf16)
        one_dim = LANES - 1 if h % 2 == 0 else 0
        v_ref[0, h] = jnp.where(dim == one_dim, 1.0, v_all[sl, :]).astype(bf16)


def _mla_proj(proj, cqn, ckvn, wuq, wuqr, wk, wv, ind, qg, qgr, kg, kgr, rope_c, rope_s, batch, lp):
    nt = lp // ROW_TILE
    hl = MLA_HEADS * HEAD_LANES
    head_spec = pl.BlockSpec((1, MLA_HEADS, ROW_TILE, HEAD_LANES), lambda i: (i // nt, 0, i % nt, 0))
    tab_spec = pl.BlockSpec((ROW_TILE, HEAD_LANES), lambda i: (i % nt, 0))
    tabt_spec = pl.BlockSpec((HEAD_LANES, ROW_TILE), lambda i: (0, i % nt))
    gain_spec = _resident((1, HEAD_LANES))
    gaint_spec = _resident((HEAD_LANES, 1))
    shp = jax.ShapeDtypeStruct((batch, MLA_HEADS, lp, HEAD_LANES), bf16)
    kt_shp = jax.ShapeDtypeStruct((batch, MLA_HEADS, HEAD_LANES, lp), bf16)
    kt_spec = pl.BlockSpec((1, MLA_HEADS, HEAD_LANES, ROW_TILE), lambda i: (i // nt, 0, 0, i % nt))
    return pl.pallas_call(
        _mla_proj_kernel,
        out_shape=(kt_shp, shp, kt_shp),
        grid=(batch * nt,),
        in_specs=[
            pl.BlockSpec((ROW_TILE, 512), lambda i: (i, COL_MLA // 512)),
            pl.BlockSpec((ROW_TILE, LANES), lambda i: (i, COL_KR_ROT // LANES)),
            _resident((1, MLA_Q_LORA)), _resident((1, MLA_KV_LORA)),
            _resident((hl, MLA_Q_LORA)), _resident((hl, MLA_Q_LORA)),
            _resident((MLA_KV_LORA, hl)), _resident((hl, MLA_KV_LORA)),
            _resident((2 * HEAD_LANES, 2 * HEAD_LANES)),
            gaint_spec, gaint_spec, gain_spec, gain_spec,
            tab_spec, tab_spec, tabt_spec, tabt_spec,
        ],
        out_specs=(kt_spec, head_spec, kt_spec),
        compiler_params=_cparams(("parallel",)),
        name="mla_proj",
    )(proj, proj, cqn, ckvn, wuq.T, wuqr.T, wk, wv.T, ind, qg.T, qgr.T, kg, kgr, rope_c, rope_s, rope_c.T, rope_s.T)


MASK_PAD, MASK_PAD_CAUSAL, MASK_CAUSAL = 0, 1, 2


def _attn_bias():
    k = np.arange(KV_TILE)[:, None]
    q = np.arange(ROW_TILE)[None, :]
    keep = np.stack([np.broadcast_to(k >= PAD, (KV_TILE, ROW_TILE)), (k >= PAD) & (k <= q), k <= q])
    return jnp.asarray(np.where(keep, 0.0, NEG_INF), dtype=f32)


def _attn_kernel(qt_ref, k_ref, vt_ref, bias_ref, o_ref, m_sc, acc_sc, s_sc):
    qi = pl.program_id(2)
    heads = range(ATTN_GROUP)

    def scores(t):
        start = pl.multiple_of(t * KV_TILE, KV_TILE)
        return [jnp.dot(k_ref[0, hh, pl.ds(start, KV_TILE), :], qt_ref[0, hh], preferred_element_type=f32)
                for hh in heads]

    def absorb(t, s_all, mask):
        start = pl.multiple_of(t * KV_TILE, KV_TILE)
        for hh in heads:
            s = s_all[hh]
            if mask is not None:
                s = s + bias_ref[mask]
            m_prev = m_sc[hh]
            m_next = jnp.maximum(m_prev, jnp.max(s, axis=0, keepdims=True))
            alpha = jnp.exp2(m_prev - m_next)
            p = jnp.exp2(s - m_next).astype(bf16)
            vt = vt_ref[0, hh, :, pl.ds(start, KV_TILE)]
            acc_sc[hh] = alpha * acc_sc[hh] + jnp.dot(vt, p, preferred_element_type=f32)
            m_sc[hh] = m_next

    def stash(s_all):
        for hh in heads:
            s_sc[hh] = s_all[hh]

    def step(t, mask):
        cur = [s_sc[hh] for hh in heads]
        stash(scores(t + 1))
        absorb(t, cur, mask)

    m_sc[...] = jnp.full(m_sc.shape, NEG_INF, f32)
    acc_sc[...] = jnp.zeros(acc_sc.shape, f32)
    first = scores(0)

    @pl.when(qi == 0)
    def _():
        absorb(0, first, MASK_PAD_CAUSAL)

    @pl.when(qi > 0)
    def _():
        stash(first)
        step(0, MASK_PAD)

        n_plain = qi - 1

        def body(u, c):
            step(2 * u + 1, None)
            step(2 * u + 2, None)
            return c
        lax.fori_loop(0, n_plain // 2, body, 0)

        @pl.when(n_plain % 2 == 1)
        def _():
            step(qi - 1, None)
        absorb(qi, [s_sc[hh] for hh in heads], MASK_CAUSAL)

    for pr in range(ATTN_GROUP // 2):
        a0, a1 = acc_sc[2 * pr], acc_sc[2 * pr + 1]
        row = lax.broadcasted_iota(jnp.int32, a0.shape, 0)
        pair = jnp.where(row < MLA_V, a0 * (1.0 / a0[LANES - 1:, :]), a1 * (1.0 / a1[:1, :]))
        o_ref[0, pr] = pair.T.astype(bf16)


def _attention(qt, k, vt):
    batch, _, lp, _ = k.shape
    nt = lp // ROW_TILE
    whole = lambda b, g, i: (b, g, 0, 0)
    return pl.pallas_call(
        _attn_kernel,
        out_shape=jax.ShapeDtypeStruct((batch, MLA_HEADS // 2, lp, HEAD_LANES), bf16),
        grid=(batch, MLA_HEADS // ATTN_GROUP, nt),
        in_specs=[
            pl.BlockSpec((1, ATTN_GROUP, HEAD_LANES, ROW_TILE), lambda b, g, i: (b, g, 0, i)),
            pl.BlockSpec((1, ATTN_GROUP, lp, HEAD_LANES), whole),
            pl.BlockSpec((1, ATTN_GROUP, HEAD_LANES, lp), whole),
            _resident((3, KV_TILE, ROW_TILE)),
        ],
        out_specs=pl.BlockSpec((1, ATTN_GROUP // 2, ROW_TILE, HEAD_LANES), lambda b, g, i: (b, g, i, 0)),
        scratch_shapes=[pltpu.VMEM((ATTN_GROUP, 1, ROW_TILE), f32), pltpu.VMEM((ATTN_GROUP, HEAD_LANES, ROW_TILE), f32),
                        pltpu.VMEM((ATTN_GROUP, KV_TILE, ROW_TILE), f32)],
        compiler_params=_cparams(("parallel", "parallel", "arbitrary")),
        name="mla_attention",
    )(qt, k, vt, _attn_bias())


def _chunk_cumsum(lg, tril):
    hi = lg.astype(bf16)
    r1 = lg - hi.astype(f32)
    mid = r1.astype(bf16)
    lo = (r1 - mid.astype(f32)).astype(bf16)
    return (jnp.dot(tril, hi, preferred_element_type=f32) + jnp.dot(tril, mid, preferred_element_type=f32)
            + jnp.dot(tril, lo, preferred_element_type=f32))


def _gated_scan(sub_heads, q_sc, k_sc, lg_sc, v_ref, tril_ref, s_sc, o_sc):
    n_groups = q_sc.shape[1] // LANES
    n_chunks = q_sc.shape[0] // CHUNK
    gv = sub_heads * LANES
    dk = LANES // sub_heads
    ci = lax.broadcasted_iota(jnp.int32, (CHUNK, CHUNK), 0)
    cj = lax.broadcasted_iota(jnp.int32, (CHUNK, CHUNK), 1)
    causal = ci >= cj
    lane = lax.broadcasted_iota(jnp.int32, (CHUNK, LANES), 1)
    if sub_heads > 1:
        srow = lax.broadcasted_iota(jnp.int32, (gv, LANES), 0) // LANES
        scol = lax.broadcasted_iota(jnp.int32, (gv, LANES), 1) // dk
        own_block = srow == scol
    tril = tril_ref[...]
    rows = [slice(c * CHUNK, (c + 1) * CHUNK) for c in range(n_chunks)]
    nt_dims = (((1,), (1,)), ((), ()))
    tn_dims = (((0,), (0,)), ((), ()))

    bs = [_chunk_cumsum(lg_sc[r, :], tril) for r in rows]

    qe, ke, kd, qb, decay = [], [], [], [], []
    for r, b in zip(rows, bs):
        q = q_sc[r, :]
        k = k_sc[r, :]
        b_mid = b[CHUNK // 2 - 1:CHUNK // 2, :]
        b_last = b[CHUNK - 1:CHUNK, :]
        qe.append((q * jnp.exp(b - b_mid)).astype(bf16))
        ke.append((k * jnp.exp(b_mid - b)).astype(bf16))
        kd.append((k * jnp.exp(b_last - b)).astype(bf16))
        qb.append((q * jnp.exp(b)).astype(bf16))
        decay.append(jnp.exp(b_last))

    o_intra = [[None] * (n_groups * sub_heads) for _ in rows]
    d_state = [[None] * n_groups for _ in rows]
    for c, r in enumerate(rows):
        v = v_ref[r, :]
        for g in range(n_groups):
            kl = slice(g * LANES, (g + 1) * LANES)
            vg = v[:, g * gv:(g + 1) * gv]
            for s in range(sub_heads):
                qs = qe[c][:, kl]
                if sub_heads > 1:
                    qs = jnp.where(lane // dk == s, qs, jnp.zeros_like(qs))
                a = lax.dot_general(qs, ke[c][:, kl], nt_dims, preferred_element_type=f32)
                a = jnp.where(causal, a, 0.0).astype(bf16)
                o_intra[c][g * sub_heads + s] = jnp.dot(a, vg[:, s * LANES:(s + 1) * LANES],
                                                        preferred_element_type=f32)
            d = lax.dot_general(vg, kd[c][:, kl], tn_dims, preferred_element_type=f32)
            d_state[c][g] = jnp.where(own_block, d, 0.0) if sub_heads > 1 else d

    for g in range(n_groups):
        kl = slice(g * LANES, (g + 1) * LANES)
        st = s_sc[g]
        for c, r in enumerate(rows):
            o_inter = lax.dot_general(qb[c][:, kl], st.astype(bf16), nt_dims, preferred_element_type=f32)
            for s in range(sub_heads):
                head = g * sub_heads + s
                o_sc[r, head * LANES:(head + 1) * LANES] = o_intra[c][head] + o_inter[:, s * LANES:(s + 1) * LANES]
            st = st * decay[c][:, kl] + d_state[c][g]
        s_sc[g] = st


def _gate_and_norm(o_sc, gain_ref, r_ref, out_ref, n_heads):
    for h in range(n_heads):
        sl = slice(h * LANES, (h + 1) * LANES)
        out_ref[:, sl] = (_rms(o_sc[:, sl], gain_ref[...]) * _silu(r_ref[:, sl].astype(f32))).astype(bf16)


def _pad_row_mask(shape):
    row = pl.program_id(1) * ROW_TILE + lax.broadcasted_iota(jnp.int32, shape, 0)
    return row >= PAD


def _gla_kernel(q_ref, k_ref, v_ref, r_ref, gd_ref, wg2_ref, bg_ref, gain_ref, tril_ref, out_ref,
                q_sc, k_sc, b_sc, s_sc, o_sc):
    @pl.when(pl.program_id(1) == 0)
    def _():
        s_sc[...] = jnp.zeros(s_sc.shape, f32)

    x = jnp.dot(gd_ref[...], wg2_ref[...], preferred_element_type=f32) + bg_ref[...]
    log_g = (jnp.minimum(x, 0.0) - jnp.log(1.0 + jnp.exp(-jnp.abs(x)))) * (1.0 / GLA_GATE_NORM)
    b_sc[...] = jnp.where(_pad_row_mask(log_g.shape), log_g, 0.0)
    q_sc[...] = q_ref[...].astype(f32) * (GLA_DK ** -0.5)
    k_sc[...] = k_ref[...].astype(f32)
    _gated_scan(LANES // GLA_DK, q_sc, k_sc, b_sc, v_ref, tril_ref, s_sc, o_sc)
    _gate_and_norm(o_sc, gain_ref, r_ref, out_ref, GLA_HEADS)


def _hgrn_kernel(q_ref, f_ref, v_ref, r_ref, lb_ref, gain_ref, tril_ref, out_ref,
                 q_sc, k_sc, b_sc, s_sc, o_sc):
    @pl.when(pl.program_id(1) == 0)
    def _():
        s_sc[...] = jnp.zeros(s_sc.shape, f32)

    lb = lb_ref[...]
    f = lb + (1.0 - lb) * _sigmoid(f_ref[...].astype(f32))
    b_sc[...] = jnp.where(_pad_row_mask(f.shape), jnp.log(f), 0.0)
    q_sc[...] = _silu(q_ref[...].astype(f32))
    k_sc[...] = 1.0 - f
    _gated_scan(1, q_sc, k_sc, b_sc, v_ref, tril_ref, s_sc, o_sc)
    _gate_and_norm(o_sc, gain_ref, r_ref, out_ref, HGRN_HEADS)


def _seq_col_spec(width, col, nt):
    return pl.BlockSpec((ROW_TILE, width), lambda b, i: (b * nt + i, col // width))


def _gla(proj, wg2, bg, gain, tril, batch, lp):
    nt = lp // ROW_TILE
    return pl.pallas_call(
        _gla_kernel,
        out_shape=jax.ShapeDtypeStruct((batch * lp, GLA_VW), bf16),
        grid=(batch, nt),
        in_specs=[
            _seq_col_spec(GLA_KW, COL_GLA_Q, nt), _seq_col_spec(GLA_KW, COL_GLA_K, nt),
            _seq_col_spec(GLA_VW, COL_GLA_V, nt), _seq_col_spec(GLA_VW, COL_GLA_R, nt),
            _seq_col_spec(LANES, COL_GLA_GD, nt),
            _resident((LANES, GLA_KW)), _resident((1, GLA_KW)), _resident((1, GLA_DV)),
            _resident((CHUNK, CHUNK)),
        ],
        out_specs=pl.BlockSpec((ROW_TILE, GLA_VW), lambda b, i: (b * nt + i, 0)),
        scratch_shapes=[
            pltpu.VMEM((ROW_TILE, GLA_KW), f32), pltpu.VMEM((ROW_TILE, GLA_KW), f32),
            pltpu.VMEM((ROW_TILE, GLA_KW), f32),
            pltpu.VMEM((GLA_KW // LANES, 2 * LANES, LANES), f32),
            pltpu.VMEM((ROW_TILE, GLA_VW), f32),
        ],
        compiler_params=_cparams(("parallel", "arbitrary")),
        name="gla",
    )(proj, proj, proj, proj, proj, wg2, bg, gain, tril)


def _hgrn(proj, lb, gain, tril, batch, lp):
    nt = lp // ROW_TILE
    return pl.pallas_call(
        _hgrn_kernel,
        out_shape=jax.ShapeDtypeStruct((batch * lp, HGRN_VW), bf16),
        grid=(batch, nt),
        in_specs=[
            _seq_col_spec(HGRN_KW, COL_HG_Q, nt), _seq_col_spec(HGRN_KW, COL_HG_F, nt),
            _seq_col_spec(HGRN_VW, COL_HG_I, nt), _seq_col_spec(HGRN_VW, COL_HG_G, nt),
            _resident((1, HGRN_KW)), _resident((1, LANES)), _resident((CHUNK, CHUNK)),
        ],
        out_specs=pl.BlockSpec((ROW_TILE, HGRN_VW), lambda b, i: (b * nt + i, 0)),
        scratch_shapes=[
            pltpu.VMEM((ROW_TILE, HGRN_KW), f32), pltpu.VMEM((ROW_TILE, HGRN_KW), f32),
            pltpu.VMEM((ROW_TILE, HGRN_KW), f32),
            pltpu.VMEM((HGRN_KW // LANES, LANES, LANES), f32),
            pltpu.VMEM((ROW_TILE, HGRN_VW), f32),
        ],
        compiler_params=_cparams(("parallel", "arbitrary")),
        name="hgrn2",
    )(proj, proj, proj, proj, lb, gain, tril)


def _mix_kernel(h_ref, om_ref, og_ref, oh_ref, g0_ref, g1_ref, g2_ref, wm_ref, wg_ref, wh_ref, wo_ref, out_ref):
    om = jnp.concatenate([om_ref[0, j] for j in range(MLA_HEADS // 2)], axis=-1)
    mixed = (_sigmoid(g0_ref[...].astype(f32)) * jnp.dot(om, wm_ref[...], preferred_element_type=f32)
             + _sigmoid(g1_ref[...].astype(f32)) * jnp.dot(og_ref[...], wg_ref[...], preferred_element_type=f32)
             + _sigmoid(g2_ref[...].astype(f32)) * jnp.dot(oh_ref[...], wh_ref[...], preferred_element_type=f32))
    h2 = h_ref[...] + jnp.dot(mixed.astype(bf16), wo_ref[...], preferred_element_type=f32)
    out_ref[...] = jnp.where(_pad_row_mask(h2.shape), h2, 0.0)


def _mix(h, o_mla, o_gla, o_hgrn, proj, wm, wg, wh, wo, batch, lp):
    nt = lp // ROW_TILE
    row = lambda b, i: (b * nt + i, 0)
    return pl.pallas_call(
        _mix_kernel,
        out_shape=jax.ShapeDtypeStruct((batch * lp, D_MODEL), f32),
        grid=(batch, nt),
        in_specs=[
            pl.BlockSpec((ROW_TILE, D_MODEL), row),
            pl.BlockSpec((1, MLA_HEADS // 2, ROW_TILE, HEAD_LANES), lambda b, i: (b, 0, i, 0)),
            pl.BlockSpec((ROW_TILE, GLA_VW), row), pl.BlockSpec((ROW_TILE, HGRN_VW), row),
            _seq_col_spec(D_MODEL, COL_GATES, nt), _seq_col_spec(D_MODEL, COL_GATES + D_MODEL, nt),
            _seq_col_spec(D_MODEL, COL_GATES + 2 * D_MODEL, nt),
            _resident((MLA_HEADS * MLA_V, D_MODEL)), _resident((GLA_VW, D_MODEL)),
            _resident((HGRN_VW, D_MODEL)), _resident((D_MODEL, D_MODEL)),
        ],
        out_specs=pl.BlockSpec((ROW_TILE, D_MODEL), row),
        compiler_params=_cparams(("parallel", "arbitrary")),
        name="mix_out",
    )(h, o_mla, o_gla, o_hgrn, proj, proj, proj, wm, wg, wh, wo)


def _ffn_kernel(h_ref, g_ref, wg_ref, wu_ref, wd_ref, out_ref):
    h = h_ref[...]
    u = _rms(h, g_ref[...]).astype(bf16)
    acc = h
    start = 0
    for width in FF_CHUNKS:
        sl = slice(start, start + width)
        start += width
        gate = jnp.dot(u, wg_ref[:, sl], preferred_element_type=f32)
        up = jnp.dot(u, wu_ref[:, sl], preferred_element_type=f32)
        acc = acc + jnp.dot((_silu(gate) * up).astype(bf16), wd_ref[sl, :], preferred_element_type=f32)
    out_ref[...] = acc


def _ffn(h, gain, wg, wu, wd):
    tp = h.shape[0]
    row = lambda i: (i, 0)
    return pl.pallas_call(
        _ffn_kernel,
        out_shape=jax.ShapeDtypeStruct((tp, D_MODEL), f32),
        grid=(tp // ROW_TILE,),
        in_specs=[
            pl.BlockSpec((ROW_TILE, D_MODEL), row), _resident((1, D_MODEL)),
            _resident((D_MODEL, D_FF)), _resident((D_MODEL, D_FF)), _resident((D_FF, D_MODEL)),
        ],
        out_specs=pl.BlockSpec((ROW_TILE, D_MODEL), row),
        compiler_params=_cparams(("parallel",)),
        name="dense_ffn",
    )(h, gain, wg, wu, wd)


def _split3(x):
    hi = x.astype(bf16)
    lo = (x - hi.astype(f32)).astype(bf16)
    return hi, lo


def _router_kernel(tiles_per_batch, h_ref, g_ref, whi_ref, wlo_ref, b_ref, lower_ref, out_ref, count_ref, run_sc):
    i = pl.program_id(0)

    @pl.when(i == 0)
    def _():
        run_sc[...] = jnp.zeros(run_sc.shape, f32)

    u = _rms(h_ref[...], g_ref[...])
    u_hi, u_lo = _split3(u)
    logits = (jnp.dot(u_hi, whi_ref[...], preferred_element_type=f32)
              + jnp.dot(u_lo, whi_ref[...], preferred_element_type=f32)
              + jnp.dot(u_hi, wlo_ref[...], preferred_element_type=f32)) + b_ref[...]
    lane = lax.broadcasted_iota(jnp.int32, logits.shape, 1)
    logits = jnp.where(lane < N_EXPERTS, logits, -jnp.inf)
    v1 = jnp.max(logits, axis=-1, keepdims=True)
    i1 = jnp.min(jnp.where(logits == v1, lane, LANES), axis=-1, keepdims=True)
    rest = jnp.where(lane == i1, -jnp.inf, logits)
    v2 = jnp.max(rest, axis=-1, keepdims=True)
    i2 = jnp.min(jnp.where(rest == v2, lane, LANES), axis=-1, keepdims=True)
    e = jnp.exp(v2 - v1)
    g1 = 1.0 / (1.0 + e)
    g2 = e * g1

    row = (i % tiles_per_batch) * ROW_TILE + lax.broadcasted_iota(jnp.int32, logits.shape, 0)
    chosen = jnp.where(((lane == i1) | (lane == i2)) & (row >= PAD), 1.0, 0.0)
    before = jnp.dot(lower_ref[...], chosen.astype(bf16), preferred_element_type=f32) + run_sc[...]
    r1 = jnp.sum(jnp.where(lane == i1, before, 0.0), axis=-1, keepdims=True)
    r2 = jnp.sum(jnp.where(lane == i2, before, 0.0), axis=-1, keepdims=True)
    total = run_sc[...] + jnp.sum(chosen, axis=0, keepdims=True)
    run_sc[...] = total
    count_ref[...] = total

    out = jnp.where(lane == 0, g1, jnp.where(lane == 1, g2, 0.0))
    out = jnp.where(lane == 2, i1.astype(f32), jnp.where(lane == 3, i2.astype(f32), out))
    out = jnp.where(lane == 4, r1, jnp.where(lane == 5, r2, out))
    out_ref[...] = out


def _router(h, gain, w_hi, w_lo, b, tiles_per_batch):
    tp = h.shape[0]
    r = np.arange(ROW_TILE)
    lower = jnp.asarray(r[:, None] > r[None, :], dtype=bf16)
    return pl.pallas_call(
        functools.partial(_router_kernel, tiles_per_batch),
        out_shape=(jax.ShapeDtypeStruct((tp, LANES), f32), jax.ShapeDtypeStruct((1, LANES), f32)),
        grid=(tp // ROW_TILE,),
        in_specs=[
            pl.BlockSpec((ROW_TILE, D_MODEL), lambda i: (i, 0)), _resident((1, D_MODEL)),
            _resident((D_MODEL, LANES)), _resident((D_MODEL, LANES)), _resident((1, LANES)),
            _resident((ROW_TILE, ROW_TILE)),
        ],
        out_specs=(pl.BlockSpec((ROW_TILE, LANES), lambda i: (i, 0)), pl.BlockSpec((1, LANES), lambda i: (0, 0))),
        scratch_shapes=[pltpu.VMEM((1, LANES), f32)],
        compiler_params=_cparams(("arbitrary",)),
        name="moe_router",
    )(h, gain, w_hi, w_lo, b, lower)


def _token_copy(src_tok, token, dst_tok, r, sem):
    return pltpu.make_async_copy(src_tok.at[token], dst_tok.at[pl.ds(r * SUBLANES, SUBLANES), :], sem)


def _gather_tokens(idx_ref, src_tok, dst_tok, sem):
    def body(g, carry):
        for j in range(GATHER_UNROLL):
            r = g * GATHER_UNROLL + j
            _token_copy(src_tok, idx_ref[0, 0, r], dst_tok, r, sem).start(priority=j % 2)
        return carry
    lax.fori_loop(0, idx_ref.shape[-1] // GATHER_UNROLL, body, 0)


def _wait_tokens(dst_tok, sem):
    pltpu.make_async_copy(dst_tok, dst_tok, sem).wait()


def _tokens_to_rows(buf_tok):
    n = buf_tok.shape[0] // SUBLANES
    return jnp.concatenate([buf_tok[pl.ds(c, n, stride=SUBLANES), :] for c in range(D_MODEL // LANES)], axis=1)


def _rows_to_tokens(rows, dst_tok):
    for c in range(D_MODEL // LANES):
        dst_tok[pl.ds(c, rows.shape[0], stride=SUBLANES), :] = rows[:, c * LANES:(c + 1) * LANES]


def _expert_kernel(be_ref, nu_ref, tok_ref, tok_next_ref, w_ref, g_ref, h_tok, wg_ref, wu_ref, wd_ref, y_ref,
                   xbuf, sem, x_sc, acc_sc):
    i = pl.program_id(0)
    k = pl.program_id(1)
    n_used = nu_ref[0]
    slot = i % 2

    def normalised_rows():
        return _rms(_tokens_to_rows(xbuf), g_ref[...]).astype(bf16)

    def partial_out():
        x = x_sc[slot]
        gate = jnp.dot(x, wg_ref[...], preferred_element_type=f32)
        up = jnp.dot(x, wu_ref[...], preferred_element_type=f32)
        return jnp.dot((_silu(gate) * up).astype(bf16), wd_ref[...], preferred_element_type=f32)

    @pl.when((k == 0) & (i == 0))
    def _():
        _gather_tokens(tok_ref, h_tok, xbuf, sem)
        _wait_tokens(xbuf, sem)
        x_sc[0] = normalised_rows()

    @pl.when((k == 0) & (i < n_used))
    def _():
        token = tok_next_ref[0, 0, 0]
        for r in range(ROW_TILE):
            if r:
                token = tok_next_ref[0, 0, r] + (token >> 31)
            _token_copy(h_tok, token, xbuf, r, sem).start(priority=r % 2)
        acc_sc[...] = partial_out()

    @pl.when((k == 1) & (i < n_used))
    def _():
        _wait_tokens(xbuf, sem)
        x_sc[1 - slot] = normalised_rows()
        _rows_to_tokens((acc_sc[...] + partial_out()) * w_ref[...], y_ref)

    @pl.when((k == 1) & (i >= n_used))
    def _():
        y_ref[...] = jnp.zeros(y_ref.shape, f32)


def _experts(block_expert, n_used, slot_row, slot_w, gain, h, wg, wu, wd):
    n_blocks = slot_row.shape[0]
    nk = D_EXPERT // EXP_CHUNK

    def live(i, nu):
        return jnp.minimum(i, nu[0] - 1)

    def kk(i, k, nu):
        return jnp.where(i < nu[0], k, nk - 1)

    grid_spec = pltpu.PrefetchScalarGridSpec(
        num_scalar_prefetch=2,
        grid=(n_blocks, nk),
        in_specs=[
            pl.BlockSpec((1, 1, ROW_TILE), lambda i, k, be, nu: (i, 0, 0), memory_space=pltpu.SMEM),
            pl.BlockSpec((1, 1, ROW_TILE), lambda i, k, be, nu: (jnp.minimum(i + 1, n_blocks - 1), 0, 0),
                         memory_space=pltpu.SMEM),
            pl.BlockSpec((ROW_TILE, 1), lambda i, k, be, nu: (i, 0)),
            pl.BlockSpec((1, D_MODEL), lambda i, k, be, nu: (0, 0)),
            pl.BlockSpec(memory_space=pl.ANY),
            pl.BlockSpec((None, D_MODEL, EXP_CHUNK), lambda i, k, be, nu: (be[live(i, nu)], 0, kk(i, k, nu))),
            pl.BlockSpec((None, D_MODEL, EXP_CHUNK), lambda i, k, be, nu: (be[live(i, nu)], 0, kk(i, k, nu))),
            pl.BlockSpec((None, EXP_CHUNK, D_MODEL), lambda i, k, be, nu: (be[live(i, nu)], kk(i, k, nu), 0)),
        ],
        out_specs=pl.BlockSpec((ROW_TILE * SUBLANES, LANES), lambda i, k, be, nu: (i, 0)),
        scratch_shapes=[
            pltpu.VMEM((ROW_TILE * SUBLANES, LANES), f32),
            pltpu.SemaphoreType.DMA(()),
            pltpu.VMEM((2, ROW_TILE, D_MODEL), bf16),
            pltpu.VMEM((ROW_TILE, D_MODEL), f32),
        ],
    )
    assert nk == 2
    h_tok = h.reshape(h.shape[0], SUBLANES, LANES)
    y_tok = pl.pallas_call(
        _expert_kernel,
        out_shape=jax.ShapeDtypeStruct((n_blocks * ROW_TILE * SUBLANES, LANES), f32),
        grid_spec=grid_spec,
        compiler_params=_cparams(("arbitrary", "arbitrary")),
        name="moe_experts",
    )(block_expert, n_used, slot_row, slot_row, slot_w, gain, h_tok, wg, wu, wd)
    return y_tok.reshape(n_blocks * ROW_TILE, SUBLANES, LANES)


def _combine_kernel(rows, tiles_per_batch, batch_stride, first_row, zero_pad_rows,
                    s0_ref, s1_ref, s0_next_ref, s1_next_ref, h_hbm, y_tok, out_ref, ybuf, hbuf, ysems, hsems):
    i = pl.program_id(0)
    slot = i % 2

    def h_copy(tile, sl):
        start = (tile // tiles_per_batch) * batch_stride + first_row + (tile % tiles_per_batch) * rows
        return pltpu.make_async_copy(h_hbm.at[pl.ds(pl.multiple_of(start, LANES), rows), :], hbuf.at[sl],
                                     hsems.at[sl])

    @pl.when(i == 0)
    def _():
        h_copy(0, 0).start()
        _gather_tokens(s0_ref, y_tok, ybuf.at[0, 0], ysems.at[0, 0])
        _gather_tokens(s1_ref, y_tok, ybuf.at[0, 1], ysems.at[0, 1])

    @pl.when(i + 1 < pl.num_programs(0))
    def _():
        h_copy(i + 1, 1 - slot).start()
        _gather_tokens(s0_next_ref, y_tok, ybuf.at[1 - slot, 0], ysems.at[1 - slot, 0])
        _gather_tokens(s1_next_ref, y_tok, ybuf.at[1 - slot, 1], ysems.at[1 - slot, 1])

    h_copy(i, slot).wait()
    _wait_tokens(ybuf.at[slot, 0], ysems.at[slot, 0])
    _wait_tokens(ybuf.at[slot, 1], ysems.at[slot, 1])
    out = hbuf[slot] + _tokens_to_rows(ybuf.at[slot, 0]) + _tokens_to_rows(ybuf.at[slot, 1])
    if zero_pad_rows:
        row = (i % tiles_per_batch) * rows + lax.broadcasted_iota(jnp.int32, out.shape, 0)
        out = jnp.where(row >= PAD, out, 0.0)
    out_ref[...] = out


def _combine(slot_of, h, y_tok, batch, lp, frames_only):
    if frames_only:
        rows, first_row = OUT_TILE, PAD + N_META
    else:
        rows, first_row = ROW_TILE, 0
    assert (lp - first_row) % rows == 0
    per_batch = (lp - first_row) // rows
    n_tiles = batch * per_batch
    slots = slot_of[:, first_row:, :].reshape(n_tiles, 1, rows, TOP_K)
    s0, s1 = slots[..., 0], slots[..., 1]
    idx_spec = pl.BlockSpec((1, 1, rows), lambda i: (i, 0, 0), memory_space=pltpu.SMEM)
    nxt_spec = pl.BlockSpec((1, 1, rows), lambda i: (jnp.minimum(i + 1, n_tiles - 1), 0, 0), memory_space=pltpu.SMEM)
    return pl.pallas_call(
        functools.partial(_combine_kernel, rows, per_batch, lp, first_row, not frames_only),
        out_shape=jax.ShapeDtypeStruct((n_tiles * rows, D_MODEL), f32),
        grid=(n_tiles,),
        in_specs=[idx_spec, idx_spec, nxt_spec, nxt_spec, pl.BlockSpec(memory_space=pl.ANY),
                  pl.BlockSpec(memory_space=pl.ANY)],
        out_specs=pl.BlockSpec((rows, D_MODEL), lambda i: (i, 0)),
        scratch_shapes=[pltpu.VMEM((2, 2, rows * SUBLANES, LANES), f32), pltpu.VMEM((2, rows, D_MODEL), f32),
                        pltpu.SemaphoreType.DMA((2, 2)), pltpu.SemaphoreType.DMA((2,))],
        compiler_params=_cparams(("arbitrary",)),
        name="moe_combine",
    )(s0, s1, s0, s1, h, y_tok)


def _moe(h, gain, w_router, b_router, wg, wu, wd, batch, lp, frames_only):
    seq = lp - PAD
    n_tok = batch * seq
    n_assign = n_tok * TOP_K
    w_r = jnp.pad(w_router, ((0, 0), (0, LANES - N_EXPERTS)))
    w_hi = w_r.astype(bf16)
    w_lo = (w_r - w_hi.astype(f32)).astype(bf16)
    b_r = jnp.pad(b_router, (0, LANES - N_EXPERTS)).reshape(1, LANES)
    routed, count_row = _router(h, gain, w_hi, w_lo, b_r, lp // ROW_TILE)
    routed = routed.reshape(batch, lp, LANES)
    gate = routed[:, PAD:, 0:TOP_K].reshape(n_assign)
    expert_p = routed[:, :, TOP_K:2 * TOP_K].astype(jnp.int32)
    rank_p = routed[:, :, 2 * TOP_K:3 * TOP_K].astype(jnp.int32)
    counts = count_row[0, :N_EXPERTS].astype(jnp.int32)

    e_flat = expert_p[:, PAD:].reshape(n_assign)
    idx_bits = max(1, (n_assign - 1).bit_length())
    packed = jnp.sort(e_flat * (1 << idx_bits) + jnp.arange(n_assign, dtype=jnp.int32))
    order = packed & ((1 << idx_bits) - 1)
    start = jnp.cumsum(counts) - counts
    padded = (counts + ROW_TILE - 1) // ROW_TILE * ROW_TILE
    pend = jnp.cumsum(padded)
    pstart = pend - padded
    n_blocks = -(-n_assign // ROW_TILE) + N_EXPERTS
    n_slots = n_blocks * ROW_TILE

    block_first = jnp.arange(n_blocks, dtype=jnp.int32) * ROW_TILE
    block_expert = jnp.minimum(jnp.sum((block_first[:, None] >= pend[None, :]).astype(jnp.int32), axis=1),
                               N_EXPERTS - 1)
    within = (block_first - pstart[block_expert])[:, None] + jnp.arange(ROW_TILE, dtype=jnp.int32)[None, :]
    live = (within < counts[block_expert][:, None]) & (block_first < pend[-1])[:, None]
    assign = order[jnp.clip(start[block_expert][:, None] + within, 0, n_assign - 1)]
    tok = assign // TOP_K
    slot_row = jnp.where(live, (tok // seq) * lp + PAD + tok % seq, 0)
    slot_w = jnp.where(live, gate[assign], 0.0)
    n_used = (pend[-1] // ROW_TILE).astype(jnp.int32).reshape(1)

    y = _experts(block_expert, n_used, slot_row.reshape(n_blocks, 1, ROW_TILE), slot_w.reshape(n_slots, 1),
                 gain, h, wg, wu, wd)

    first_slot = jnp.zeros(expert_p.shape, jnp.int32)
    for e in range(N_EXPERTS):
        first_slot = jnp.where(expert_p == e, pstart[e], first_slot)
    is_pad = (jnp.arange(lp) < PAD)[None, :, None]
    slot_of = jnp.where(is_pad, 0, first_slot + rank_p)
    return _combine(slot_of, h, y, batch, lp, frames_only)


def _pad_cols(w, width):
    return jnp.pad(w, ((0, 0), (0, width - w.shape[1])))


def _in_proj_weight(w):
    offs = np.cumsum((MLA_Q_LORA, MLA_KV_LORA, MLA_ROPE, GLA_KW, GLA_KW, GLA_VW, GLA_GATE_RANK, GLA_VW,
                      HGRN_KW, HGRN_KW, HGRN_VW, HGRN_VW, N_BRANCH * D_MODEL))[:-1]
    (dq, dkv, kr, gq, gk, gv, ggd, gr, hq, hf, hi, hg, gates) = jnp.split(w, offs, axis=1)
    kr_p = jnp.pad(kr, ((0, 0), (MLA_NOPE, HEAD_LANES - MLA_NOPE - MLA_ROPE)))
    parts = [dq, dkv, kr_p, gq, gk, gv, gr, hq, hf, hi, hg, gates, _pad_cols(ggd, LANES), _rotate_half_cols(kr_p)]
    return jnp.concatenate(parts, axis=1).astype(bf16)


def _rotate_half_cols(w, sign=-1.0):
    lead = w.shape[0]
    w3 = w.reshape(lead, -1, HEAD_LANES)
    half = MLA_ROPE // 2
    first, second = w3[:, :, MLA_NOPE:MLA_NOPE + half], w3[:, :, MLA_NOPE + half:MLA_QK]
    zero = lambda n: jnp.zeros(w3.shape[:2] + (n,), w.dtype)
    return jnp.concatenate([zero(MLA_NOPE), sign * second, first, zero(HEAD_LANES - MLA_QK)],
                           axis=-1).reshape(lead, -1)


def _mla_weights(w_uq, w_ukv):
    d_in = w_uq.shape[0]
    wq = w_uq.reshape(d_in, MLA_HEADS, MLA_QK)
    wq = jnp.pad(wq, ((0, 0), (0, 0), (0, HEAD_LANES - MLA_QK))).reshape(d_in, MLA_HEADS * HEAD_LANES)
    d_kv = w_ukv.shape[0]
    wkv = w_ukv.reshape(d_kv, MLA_HEADS, MLA_NOPE + MLA_V)
    wk = jnp.pad(wkv[:, :, :MLA_NOPE], ((0, 0), (0, 0), (0, HEAD_LANES - MLA_NOPE)))
    wv = wkv[:, :, MLA_NOPE:]
    zeros = jnp.zeros_like(wv)
    even = (jnp.arange(MLA_HEADS) % 2 == 0)[None, :, None]
    wv = jnp.concatenate([jnp.where(even, wv, zeros), jnp.where(even, zeros, wv)], axis=-1)
    return (wq.astype(bf16), wk.reshape(d_kv, -1).astype(bf16), wv.reshape(d_kv, -1).astype(bf16))


def _head_gain(g, scale):
    return (jnp.pad(g, (0, HEAD_LANES - MLA_QK)) * scale).reshape(1, HEAD_LANES).astype(f32)


def _rope_tables(lp):
    half = MLA_ROPE // 2
    pos = np.maximum(np.arange(lp) - PAD, 0).astype(np.float32)
    inv = (ROPE_THETA ** (-np.arange(half, dtype=np.float32) / half)).astype(np.float32)
    ang = jnp.asarray(pos)[:, None] * jnp.asarray(inv)[None, :]
    cos, sin = jnp.cos(ang), jnp.sin(ang)
    ones = jnp.ones((lp, MLA_NOPE), f32)
    z = lambda n: jnp.zeros((lp, n), f32)
    tail = HEAD_LANES - MLA_QK
    c = jnp.concatenate([ones, cos, cos, jnp.ones((lp, tail), f32)], axis=1)
    s = jnp.concatenate([z(MLA_NOPE), sin, sin, z(tail)], axis=1)
    return c, s


def _head_indicator():
    lane_head = np.arange(2 * HEAD_LANES) // HEAD_LANES
    return jnp.asarray(lane_head[:, None] == lane_head[None, :], dtype=bf16)


def _chunk_tril():
    r = np.arange(CHUNK)
    return jnp.asarray(r[:, None] >= r[None, :], dtype=bf16)


def kernel(x, meta_tokens, attn_norm, w_in, mla_cq_norm, w_mla_uq, mla_ckv_norm, w_mla_ukv, mla_q_norm, mla_k_norm, w_gla_g2, b_gla_g, gla_o_norm, hgrn_lower_bounds, hgrn_o_norm, w_mla_o, w_gla_o, w_hgrn_o, w_out, ffn_norm, w_ff_gate, w_ff_up, w_ff_down, w_router, b_router, w_exp_gate, w_exp_up, w_exp_down):
    batch, seq, _ = x.shape
    depth = w_in.shape[0]
    lp = PAD + N_META + seq
    assert lp % ROW_TILE == 0 and ROW_TILE % CHUNK == 0 and KV_TILE == ROW_TILE

    meta = jnp.broadcast_to(meta_tokens[None].astype(x.dtype), (batch, N_META, D_MODEL))
    h = jnp.concatenate([jnp.zeros((batch, PAD, D_MODEL), x.dtype), meta, x], axis=1).reshape(batch * lp, D_MODEL)

    lbs = jax.nn.softmax(hgrn_lower_bounds.astype(f32), axis=0)
    lbs = jnp.cumsum(lbs, axis=0) - lbs[0]
    rope_c, rope_s = _rope_tables(lp)
    head_ind = _head_indicator()
    tril = _chunk_tril()
    row1 = lambda v: v.reshape(1, -1).astype(f32)

    for layer in range(depth):
        proj = _in_proj(h, row1(attn_norm[layer]), _in_proj_weight(w_in[layer]))
        wq, wk, wv = _mla_weights(w_mla_uq[layer], w_mla_ukv[layer])
        q_gain = _head_gain(mla_q_norm[layer], MLA_QK ** -0.5 * LOG2_E)
        k_gain = _head_gain(mla_k_norm[layer], 1.0)
        q, k, v = _mla_proj(proj, row1(mla_cq_norm[layer]), row1(mla_ckv_norm[layer]),
                            wq, _rotate_half_cols(wq), wk, wv, head_ind,
                            q_gain, _rotate_half_cols(q_gain, 1.0), k_gain, _rotate_half_cols(k_gain, 1.0),
                            rope_c, rope_s, batch, lp)
        o_mla = _attention(q, k, v)
        wg2 = jnp.pad(w_gla_g2[layer], ((0, LANES - GLA_GATE_RANK), (0, 0))).astype(bf16)
        o_gla = _gla(proj, wg2, row1(b_gla_g[layer]), row1(gla_o_norm[layer]), tril, batch, lp)
        o_hgrn = _hgrn(proj, row1(lbs[layer]), row1(hgrn_o_norm[layer]), tril, batch, lp)
        h = _mix(h, o_mla, o_gla, o_hgrn, proj, w_mla_o[layer].astype(bf16), w_gla_o[layer].astype(bf16),
                 w_hgrn_o[layer].astype(bf16), w_out[layer].astype(bf16), batch, lp)
        i = layer // 2
        if layer % 2 == 0:
            h = _ffn(h, row1(ffn_norm[layer]), w_ff_gate[i].astype(bf16), w_ff_up[i].astype(bf16),
                     w_ff_down[i].astype(bf16))
        else:
            h = _moe(h, row1(ffn_norm[layer]), w_router[i], b_router[i], w_exp_gate[i].astype(bf16),
                     w_exp_up[i].astype(bf16), w_exp_down[i].astype(bf16), batch, lp, layer == depth - 1)
    if depth % 2 == 0:
        return h.reshape(batch, seq, D_MODEL)
    return h.reshape(batch, lp, D_MODEL)[:, PAD + N_META:]
```
